```python
import jax, jax.numpy as jnp
from jax import lax
import numpy as np

D_MODEL = 1024
BATCH = 8
SEQ = 2048
DEPTH = 1
DEC_BATCH = 128
DEC_SEQ = 8
PAST_LEN = 16384
PAGE_SIZE = 128

C_CONV = D_MODEL // 2
C_GMLP = D_MODEL - C_CONV
GMLP_HEADS = 8
GMLP_HEAD_DIM = C_GMLP // GMLP_HEADS
CONV_WIDTH = 31
CHUNK = 128
PLE_DIM = 256
N_GROUPS = 4
EXPERTS_PER_GROUP = 4
N_EXPERTS = N_GROUPS * EXPERTS_PER_GROUP
TOP_K = 2
D_EXPERT = D_MODEL // 2
D_IN = 2 * C_CONV + 2 * C_GMLP
EPS = 1e-6

kernel_name = "hymba_conv_gmlp_hmoe_step"


def rms_norm(x, g):
    xf = x.astype(jnp.float32)
    y = xf * lax.rsqrt(jnp.mean(xf * xf, axis=-1, keepdims=True) + EPS)
    return (y * g.astype(jnp.float32)).astype(x.dtype)


def layer_norm(x, g, b):
    xf = x.astype(jnp.float32)
    mu = jnp.mean(xf, axis=-1, keepdims=True)
    xc = xf - mu
    var = jnp.mean(xc * xc, axis=-1, keepdims=True)
    y = xc * lax.rsqrt(var + EPS) * g.astype(jnp.float32) + b.astype(jnp.float32)
    return y.astype(x.dtype)


def causal_depthwise_conv(buf, x, w, bias):
    xs = jnp.concatenate([buf, x], axis=1)
    y = lax.conv_general_dilated(
        xs, w[:, None, :], window_strides=(1,), padding='VALID',
        dimension_numbers=('NWC', 'WIO', 'NWC'), feature_group_count=xs.shape[-1])
    return y + bias, xs[:, -(CONV_WIDTH - 1):]


def chunk_spatial_mix(v, w_s, b_s, chunk_len):
    n, t, c = v.shape
    nc = t // chunk_len
    vh = v.reshape(n, nc, chunk_len, GMLP_HEADS, GMLP_HEAD_DIM)
    mask = jnp.tril(jnp.ones((chunk_len, chunk_len), dtype=bool))
    ws = jnp.where(mask[None], w_s[:, :chunk_len, :chunk_len], 0)
    mixed = jnp.einsum('hij,ncjhd->ncihd', ws, vh)
    mixed = mixed + jnp.transpose(b_s[:, :chunk_len])[None, None, :, :, None]
    return mixed.reshape(n, t, c)


def token_mixers(a, buf, chunk_len, w_in, conv_w, conv_b, conv_ln_g, conv_ln_b,
                 gmlp_ln_g, gmlp_ln_b, w_s, b_s, w_out):
    z = a @ w_in
    a_val, a_gate, b_u, b_v = jnp.split(
        z, [C_CONV, 2 * C_CONV, 2 * C_CONV + C_GMLP], axis=-1)
    glu = a_val * jax.nn.sigmoid(a_gate)
    conv_out, new_buf = causal_depthwise_conv(buf, glu, conv_w, conv_b)
    ya = jax.nn.silu(layer_norm(conv_out, conv_ln_g, conv_ln_b))
    u = jax.nn.gelu(b_u)
    v = layer_norm(jax.nn.gelu(b_v), gmlp_ln_g, gmlp_ln_b)
    yb = u * chunk_spatial_mix(v, w_s, b_s, chunk_len)
    out = jnp.concatenate([ya, yb], axis=-1) @ w_out
    return out, new_buf, v


def hierarchical_moe(c, w_rg, b_rg, w_re, b_re, w_gate, w_up, w_down):
    n, t, d = c.shape
    xf = c.reshape(n * t, d)
    gl = (xf @ w_rg + b_rg).astype(jnp.float32)
    gp = jax.nn.softmax(gl, axis=-1)
    g = jnp.argmax(gl, axis=-1)
    g_w = jnp.take_along_axis(gp, g[:, None], axis=1)
    el = (xf @ w_re + b_re).astype(jnp.float32).reshape(-1, N_GROUPS, EXPERTS_PER_GROUP)
    el_g = jnp.take_along_axis(el, g[:, None, None], axis=1)[:, 0]
    top_v, top_i = lax.top_k(el_g, TOP_K)
    w = jax.nn.softmax(top_v, axis=-1) * g_w
    ids = g[:, None] * EXPERTS_PER_GROUP + top_i
    combine = jnp.sum(jax.nn.one_hot(ids, N_EXPERTS, dtype=jnp.float32) * w[..., None], axis=1)
    y = jnp.zeros(xf.shape, jnp.float32)
    for e in range(N_EXPERTS):
        h = jax.nn.silu(xf @ w_gate[e]) * (xf @ w_up[e])
        y = y + combine[:, e:e + 1] * (h @ w_down[e]).astype(jnp.float32)
    return y.astype(c.dtype).reshape(n, t, d)


def decoder_layer(x, p, buf, chunk_len, norm_mix_g, w_in, conv_w, conv_b, conv_ln_g,
                  conv_ln_b, gmlp_ln_g, gmlp_ln_b, w_s, b_s, w_out, norm_ffn_g,
                  w_router_group, b_router_group, w_router_expert, b_router_expert,
                  w_exp_gate, w_exp_up, w_exp_down, norm_ple_g, w_ple_gate, w_ple_proj):
    mix, new_buf, v = token_mixers(rms_norm(x, norm_mix_g), buf, chunk_len, w_in, conv_w,
                                   conv_b, conv_ln_g, conv_ln_b, gmlp_ln_g, gmlp_ln_b,
                                   w_s, b_s, w_out)
    h = x + mix
    h = h + hierarchical_moe(rms_norm(h, norm_ffn_g), w_router_group, b_router_group,
                             w_router_expert, b_router_expert, w_exp_gate, w_exp_up,
                             w_exp_down)
    gate = jax.nn.sigmoid(rms_norm(h, norm_ple_g) @ w_ple_gate)
    h = h + (p @ w_ple_proj) * gate
    return h, new_buf, v


def setup_inputs(seed: int = 0) -> dict:
    key = jax.random.key(seed)
    ks = jax.random.split(key, 32)
    f32 = jnp.float32

    def nrm(k, shape, scale):
        return jax.random.normal(k, shape, f32) * scale

    def gain(k, shape):
        return 1.0 + 0.05 * jax.random.normal(k, shape, f32)

    return {
        "x_prompt": nrm(ks[0], (BATCH, SEQ, D_MODEL), 1.0),
        "x_sample": nrm(ks[1], (DEC_BATCH, DEC_SEQ, D_MODEL), 1.0),
        "p_prompt": nrm(ks[2], (DEPTH, BATCH, SEQ, PLE_DIM), 1.0),
        "p_sample": nrm(ks[3], (DEPTH, DEC_BATCH, DEC_SEQ, PLE_DIM), 1.0),
        "state_conv": nrm(ks[4], (DEPTH, DEC_BATCH, CONV_WIDTH - 1, C_CONV), 0.5),
        "norm_mix_g": gain(ks[5], (DEPTH, D_MODEL)),
        "w_in": nrm(ks[6], (DEPTH, D_MODEL, D_IN), D_MODEL ** -0.5),
        "conv_w": nrm(ks[7], (DEPTH, CONV_WIDTH, C_CONV), CONV_WIDTH ** -0.5),
        "conv_b": nrm(ks[8], (DEPTH, C_CONV), 0.02),
        "conv_ln_g": gain(ks[9], (DEPTH, C_CONV)),
        "conv_ln_b": nrm(ks[10], (DEPTH, C_CONV), 0.02),
        "gmlp_ln_g": gain(ks[11], (DEPTH, C_GMLP)),
        "gmlp_ln_b": nrm(ks[12], (DEPTH, C_GMLP), 0.02),
        "w_s": nrm(ks[13], (DEPTH, GMLP_HEADS, CHUNK, CHUNK), CHUNK ** -0.5),
        "b_s": gain(ks[14], (DEPTH, GMLP_HEADS, CHUNK)),
        "w_out": nrm(ks[15], (DEPTH, C_CONV + C_GMLP, D_MODEL), (C_CONV + C_GMLP) ** -0.5),
        "norm_ffn_g": gain(ks[16], (DEPTH, D_MODEL)),
        "w_router_group": nrm(ks[17], (DEPTH, D_MODEL, N_GROUPS), D_MODEL ** -0.5),
        "b_router_group": nrm(ks[18], (DEPTH, N_GROUPS), 0.01),
        "w_router_expert": nrm(ks[19], (DEPTH, D_MODEL, N_EXPERTS), D_MODEL ** -0.5),
        "b_router_expert": nrm(ks[20], (DEPTH, N_EXPERTS), 0.01),
        "w_exp_gate": nrm(ks[21], (DEPTH, N_EXPERTS, D_MODEL, D_EXPERT), D_MODEL ** -0.5),
        "w_exp_up": nrm(ks[22], (DEPTH, N_EXPERTS, D_MODEL, D_EXPERT), D_MODEL ** -0.5),
        "w_exp_down": nrm(ks[23], (DEPTH, N_EXPERTS, D_EXPERT, D_MODEL), D_EXPERT ** -0.5),
        "norm_ple_g": gain(ks[24], (DEPTH, D_MODEL)),
        "w_ple_gate": nrm(ks[25], (DEPTH, D_MODEL, D_MODEL), D_MODEL ** -0.5),
        "w_ple_proj": nrm(ks[26], (DEPTH, PLE_DIM, D_MODEL), PLE_DIM ** -0.5),
        "norm_final_g": gain(ks[27], (D_MODEL,)),
    }


def reference(x_prompt, x_sample, p_prompt, p_sample, state_conv, norm_mix_g, w_in, conv_w,
              conv_b, conv_ln_g, conv_ln_b, gmlp_ln_g, gmlp_ln_b, w_s, b_s, w_out,
              norm_ffn_g, w_router_group, b_router_group, w_router_expert, b_router_expert,
              w_exp_gate, w_exp_up, w_exp_down, norm_ple_g, w_ple_gate, w_ple_proj,
              norm_final_g):
    hp = x_prompt
    hs = x_sample
    dec_seq = x_sample.shape[1]
    conv_p_list, conv_s_list, v_s_list = [], [], []
    for i in range(DEPTH):
        lw = (norm_mix_g[i], w_in[i], conv_w[i], conv_b[i], conv_ln_g[i], conv_ln_b[i],
              gmlp_ln_g[i], gmlp_ln_b[i], w_s[i], b_s[i], w_out[i], norm_ffn_g[i],
              w_router_group[i], b_router_group[i], w_router_expert[i], b_router_expert[i],
              w_exp_gate[i], w_exp_up[i], w_exp_down[i], norm_ple_g[i], w_ple_gate[i],
              w_ple_proj[i])
        zero_buf = jnp.zeros((hp.shape[0], CONV_WIDTH - 1, C_CONV), hp.dtype)
        hp, conv_p, _ = decoder_layer(hp, p_prompt[i], zero_buf, CHUNK, *lw)
        hs, conv_s, v_s = decoder_layer(hs, p_sample[i], state_conv[i], dec_seq, *lw)
        conv_p_list.append(conv_p)
        conv_s_list.append(conv_s)
        v_s_list.append(v_s)
    y_prompt = rms_norm(hp, norm_final_g)
    y_sample = rms_norm(hs, norm_final_g)
    state_conv_prompt = jnp.stack(conv_p_list, axis=0)
    state_conv_sample = jnp.stack(conv_s_list, axis=0)
    state_gmlp_v_sample = jnp.stack(v_s_list, axis=0)
    return (y_prompt, y_sample, state_conv_prompt, state_conv_sample, state_gmlp_v_sample)
```

```python
import functools

import numpy as np
import jax
import jax.numpy as jnp
from jax import lax
from jax.experimental import pallas as pl
from jax.experimental.pallas import tpu as pltpu

F32 = jnp.float32
BF16 = jnp.bfloat16

D_MODEL = 1024
C_CONV = 512
C_GMLP = 512
N_HEADS = 8
HEAD_DIM = 64
CONV_WIDTH = 31
CHUNK = 128
PLE_DIM = 256
N_GROUPS = 4
EPG = 4
N_EXPERTS = 16
D_EXPERT = 512
EPS = 1e-6

N_CLASSES = N_GROUPS * 6
CLS_ROWS = 32
HIST = 32
HIST_OFF = HIST - (CONV_WIDTH - 1)
ROW_W = D_MODEL + 256

TP = 512
TM = 256
TK = 512
CONV_ROWS = 64

VMEM_LIMIT = 56 * 1024 * 1024

_PAIRS = ((0, 1), (0, 2), (0, 3), (1, 2), (1, 3), (2, 3))


def _rms(x, g):
    ms = jnp.mean(x * x, axis=-1, keepdims=True)
    return x * lax.rsqrt(ms + EPS) * g


def _ln(x, g, b):
    mu = jnp.mean(x, axis=-1, keepdims=True)
    xc = x - mu
    var = jnp.mean(xc * xc, axis=-1, keepdims=True)
    return xc * lax.rsqrt(var + EPS) * g + b


def _max4(v):
    return jnp.maximum(jnp.maximum(v[0], v[1]), jnp.maximum(v[2], v[3]))


def _first4(v, m):
    return jnp.where(v[0] == m, 0, jnp.where(v[1] == m, 1, jnp.where(v[2] == m, 2, 3))).astype(jnp.int32)


def _route(lt):
    gl = [lt[i:i + 1, :] for i in range(N_GROUPS)]
    m = _max4(gl)
    g = _first4(gl, m)
    den = jnp.exp(gl[0] - m) + jnp.exp(gl[1] - m) + jnp.exp(gl[2] - m) + jnp.exp(gl[3] - m)
    g_w = 1.0 / den
    a = []
    for j in range(EPG):
        rows = [lt[N_GROUPS + EPG * q + j:N_GROUPS + EPG * q + j + 1, :] for q in range(N_GROUPS)]
        a.append(jnp.where(g == 0, rows[0], jnp.where(g == 1, rows[1], jnp.where(g == 2, rows[2], rows[3]))))
    v1 = _max4(a)
    i1 = _first4(a, v1)
    a2 = [jnp.where(i1 == j, -jnp.inf, a[j]) for j in range(EPG)]
    v2 = _max4(a2)
    i2 = _first4(a2, v2)
    e = jnp.exp(v2 - v1)
    s = 1.0 + e
    w1 = g_w / s
    w2 = g_w * e / s
    first_lo = i1 < i2
    lo = jnp.minimum(i1, i2)
    hi = jnp.maximum(i1, i2)
    w_lo = jnp.where(first_lo, w1, w2)
    w_hi = jnp.where(first_lo, w2, w1)
    pair = jnp.where(lo == 0, hi - 1, jnp.where(lo == 1, hi + 1, 5))
    return g * 6 + pair, w_lo, w_hi


def _router_tail(h, g_ffn_ref, w_rt_ref, b_r_ref, tri_ref, carry_ref, h1a_ref, meta_ref, cnt_ref, t):
    c = _rms(h, g_ffn_ref[...]).astype(BF16)
    lt = lax.dot_general(w_rt_ref[...], c, (((1,), (1,)), ((), ())), preferred_element_type=F32)
    lt = lt + b_r_ref[...]
    cls, w_lo, w_hi = _route(lt)
    rows = lax.broadcasted_iota(jnp.int32, (CLS_ROWS, t), 0)
    ohf = jnp.where(rows == cls, 1.0, 0.0).astype(F32)
    pre = jnp.dot(ohf.astype(BF16), tri_ref[...], preferred_element_type=F32)
    carry = carry_ref[:, 0:1]
    rank = jnp.sum(ohf * (pre - 1.0 + carry), axis=0, keepdims=True)
    new_cnt = jnp.broadcast_to(carry + jnp.sum(ohf, axis=1, keepdims=True), (CLS_ROWS, 128))
    carry_ref[...] = new_cnt
    cnt_ref[...] = new_cnt
    meta_ref[0] = jnp.concatenate([cls.astype(F32), rank, jnp.zeros((6, t), F32)], axis=0)
    h1a_ref[:, 0:D_MODEL] = h
    h1a_ref[:, D_MODEL:D_MODEL + 128] = jnp.transpose(jnp.broadcast_to(w_lo, (128, t)))
    h1a_ref[:, D_MODEL + 128:ROW_W] = jnp.transpose(jnp.broadcast_to(w_hi, (128, t)))


def _mixer_front(x, g_mix_ref, w_in_ref):
    a = _rms(x, g_mix_ref[...]).astype(BF16)
    z = jnp.dot(a, w_in_ref[...], preferred_element_type=F32)
    glu = z[:, 0:C_CONV] * jax.nn.sigmoid(z[:, C_CONV:2 * C_CONV])
    u = jax.nn.gelu(z[:, 2 * C_CONV:2 * C_CONV + C_GMLP])
    gv = jax.nn.gelu(z[:, 2 * C_CONV + C_GMLP:])
    return glu, u, gv


def _prompt_mixer_body(x_ref, g_mix_ref, w_in_ref, cw_ref, cb_ref, clg_ref, clb_ref, glg_ref, glb_ref,
                       ws_ref, bs_ref, w_out_ref, g_ffn_ref, w_rt_ref, b_r_ref, tri_ref,
                       h1a_ref, meta_ref, cnt_ref, cst_ref,
                       glu_scr, cat_scr, carry_ref):
    b = pl.program_id(0)
    j = pl.program_id(1)

    @pl.when(jnp.logical_and(b == 0, j == 0))
    def _():
        carry_ref[...] = jnp.zeros_like(carry_ref)

    x = x_ref[0]
    glu, u, gv = _mixer_front(x, g_mix_ref, w_in_ref)

    @pl.when(j == 0)
    def _():
        glu_scr[0:HIST, :] = jnp.zeros((HIST, C_CONV), F32)

    @pl.when(j > 0)
    def _():
        glu_scr[0:HIST, :] = glu_scr[TP:TP + HIST, :]

    glu_scr[HIST:HIST + TP, :] = glu
    cst_ref[0] = glu_scr[TP:TP + HIST, :]

    cb = cb_ref[...]
    for r0 in range(0, TP, CONV_ROWS):
        acc = jnp.zeros((CONV_ROWS, C_CONV), F32)
        for k in range(CONV_WIDTH):
            s0 = r0 + HIST_OFF + k
            acc = acc + cw_ref[k:k + 1, :] * glu_scr[s0:s0 + CONV_ROWS, :]
        ya = jax.nn.silu(_ln(acc + cb, clg_ref[...], clb_ref[...]))
        cat_scr[r0:r0 + CONV_ROWS, 0:C_CONV] = ya.astype(BF16)

    v = _ln(gv, glg_ref[...], glb_ref[...])
    vb = v.astype(BF16)
    lane = lax.broadcasted_iota(jnp.int32, (CHUNK, 128), 1)
    is_lo = lane < HEAD_DIM
    zero = jnp.zeros((CHUNK, 128), BF16)
    for c in range(TP // CHUNK):
        for p in range(N_HEADS // 2):
            blk = vb[c * CHUNK:(c + 1) * CHUNK, p * 128:(p + 1) * 128]
            rhs = jnp.concatenate([jnp.where(is_lo, blk, zero), jnp.where(is_lo, zero, blk)], axis=0)
            mixed = jnp.dot(ws_ref[p], rhs, preferred_element_type=F32) + bs_ref[:, p * 128:(p + 1) * 128]
            yb = u[c * CHUNK:(c + 1) * CHUNK, p * 128:(p + 1) * 128] * mixed
            cat_scr[c * CHUNK:(c + 1) * CHUNK, C_CONV + p * 128:C_CONV + (p + 1) * 128] = yb.astype(BF16)

    h = x + jnp.dot(cat_scr[...], w_out_ref[...], preferred_element_type=F32)
    _router_tail(h, g_ffn_ref, w_rt_ref, b_r_ref, tri_ref, carry_ref, h1a_ref, meta_ref, cnt_ref, TP)


def _sample_mixer_body(x_ref, hist_ref, g_mix_ref, w_in_ref, cw_ref, cb_ref, clg_ref, clb_ref, glg_ref, glb_ref,
                       wsl_ref, bsl_ref, w_out_ref, g_ffn_ref, w_rt_ref, b_r_ref, tri_ref, h1a_in_ref,
                       h1a_ref, meta_ref, cnt_ref, cst_ref, vs_ref,
                       xs_scr, cat_scr, carry_ref, *, dec_seq, sblk):
    del h1a_in_ref
    i = pl.program_id(0)
    t_tok = dec_seq * sblk

    @pl.when(i == 0)
    def _():
        carry_ref[...] = jnp.zeros_like(carry_ref)

    x = x_ref[...].reshape(t_tok, D_MODEL)
    glu, u, gv = _mixer_front(x, g_mix_ref, w_in_ref)

    nh = CONV_WIDTH - 1
    xs_scr[0:nh] = hist_ref[...]
    xs_scr[nh:nh + dec_seq] = glu.reshape(dec_seq, sblk, C_CONV)
    cst_ref[...] = xs_scr[dec_seq:dec_seq + nh]

    cb = cb_ref[...]
    for t in range(dec_seq):
        acc = jnp.zeros((sblk, C_CONV), F32)
        for k in range(CONV_WIDTH):
            acc = acc + cw_ref[k:k + 1, :] * xs_scr[t + k]
        ya = jax.nn.silu(_ln(acc + cb, clg_ref[...], clb_ref[...]))
        cat_scr[t * sblk:(t + 1) * sblk, 0:C_CONV] = ya.astype(BF16)

    v = _ln(gv, glg_ref[...], glb_ref[...])
    vs_ref[...] = v.reshape(dec_seq, sblk, C_GMLP)
    for t in range(dec_seq):
        mixed = jnp.broadcast_to(bsl_ref[t:t + 1, :], (sblk, C_GMLP))
        for tp in range(t + 1):
            r = t * dec_seq + tp
            mixed = mixed + wsl_ref[r:r + 1, :] * v[tp * sblk:(tp + 1) * sblk, :]
        yb = u[t * sblk:(t + 1) * sblk, :] * mixed
        cat_scr[t * sblk:(t + 1) * sblk, C_CONV:] = yb.astype(BF16)

    h = x + jnp.dot(cat_scr[...], w_out_ref[...], preferred_element_type=F32)
    _router_tail(h, g_ffn_ref, w_rt_ref, b_r_ref, tri_ref, carry_ref, h1a_ref, meta_ref, cnt_ref, t_tok)


def _row_gather_issue(idx_ref, src_hbm, buf, sem, slot, n_rows):
    def body(r, carry):
        row = idx_ref[0, 0, r]
        pltpu.make_async_copy(src_hbm.at[pl.ds(row, 1), :], buf.at[slot, pl.ds(r, 1), :], sem.at[slot]).start()
        return carry
    lax.fori_loop(0, n_rows, body, 0, unroll=8)


def _row_gather_wait(src_hbm, buf, sem, slot, n_rows):
    def body(r, carry):
        pltpu.make_async_copy(src_hbm.at[pl.ds(0, 1), :], buf.at[slot, pl.ds(r, 1), :], sem.at[slot]).wait()
        return carry
    lax.fori_loop(0, n_rows, body, 0, unroll=8)


def _moe_body(elo_ref, ehi_ref, valid_ref, oidx_ref,
              src0_ref, srcn_ref, h1a_hbm, g_ffn_ref, wgu_lo_ref, wgu_hi_ref, wd_lo_ref, wd_hi_ref,
              out_ref, buf, sem):
    del elo_ref, ehi_ref, oidx_ref
    i = pl.program_id(0)
    n = pl.num_programs(0)
    slot = lax.rem(i, 2)

    @pl.when(i == 0)
    def _():
        _row_gather_issue(src0_ref, h1a_hbm, buf, sem, 0, TM)

    nxt = jnp.minimum(i + 1, n - 1)

    @pl.when(jnp.logical_and(i + 1 < n, valid_ref[nxt] == 1))
    def _():
        _row_gather_issue(srcn_ref, h1a_hbm, buf, sem, 1 - slot, TM)

    @pl.when(valid_ref[i] == 1)
    def _():
        _row_gather_wait(h1a_hbm, buf, sem, slot, TM)
        h = buf[slot, :, 0:D_MODEL]
        wl = buf[slot, :, D_MODEL:D_MODEL + 128]
        wh = buf[slot, :, D_MODEL + 128:ROW_W]
        c = _rms(h, g_ffn_ref[...]).astype(BF16)
        gu = jnp.dot(c, wgu_lo_ref[0], preferred_element_type=F32)
        hl = (jax.nn.silu(gu[:, 0:D_EXPERT]) * gu[:, D_EXPERT:]).astype(BF16)
        gu = jnp.dot(c, wgu_hi_ref[0], preferred_element_type=F32)
        hh = (jax.nn.silu(gu[:, 0:D_EXPERT]) * gu[:, D_EXPERT:]).astype(BF16)
        yl = jnp.dot(hl, wd_lo_ref[0], preferred_element_type=F32)
        yh = jnp.dot(hh, wd_hi_ref[0], preferred_element_type=F32)
        wl8 = jnp.concatenate([wl] * (D_MODEL // 128), axis=1)
        wh8 = jnp.concatenate([wh] * (D_MODEL // 128), axis=1)
        out_ref[...] = h + (wl8 * yl + wh8 * yh)


def _ple_body(dst0_ref, dstn_ref, hs_hbm, p_ref, g_ple_ref, w_gate_ref, w_proj_ref, g_fin_ref,
              y_ref, buf, sem):
    i = pl.program_id(0)
    n = pl.num_programs(0)
    slot = lax.rem(i, 2)

    @pl.when(i == 0)
    def _():
        _row_gather_issue(dst0_ref, hs_hbm, buf, sem, 0, TK)

    @pl.when(i + 1 < n)
    def _():
        _row_gather_issue(dstn_ref, hs_hbm, buf, sem, 1 - slot, TK)

    _row_gather_wait(hs_hbm, buf, sem, slot, TK)
    h = buf[slot]
    c = _rms(h, g_ple_ref[...]).astype(BF16)
    gate = jax.nn.sigmoid(jnp.dot(c, w_gate_ref[...], preferred_element_type=F32))
    proj = jnp.dot(p_ref[...].astype(BF16), w_proj_ref[...], preferred_element_type=F32)
    h = h + proj * gate
    y_ref[...] = _rms(h, g_fin_ref[...])


def _full(shape):
    nd = len(shape)
    return pl.BlockSpec(shape, lambda *_: (0,) * nd)


def _prompt_mixer(x, weights, tri, n_rows_total):
    batch, seq, _ = x.shape
    nj = seq // TP
    n_tiles = batch * nj
    w_specs = [_full(w.shape) for w in weights]
    return pl.pallas_call(
        _prompt_mixer_body,
        grid=(batch, nj),
        in_specs=[pl.BlockSpec((1, TP, D_MODEL), lambda b, j: (b, j, 0))] + w_specs + [_full(tri.shape)],
        out_specs=[
            pl.BlockSpec((TP, ROW_W), lambda b, j: (b * nj + j, 0)),
            pl.BlockSpec((1, 8, TP), lambda b, j: (b * nj + j, 0, 0)),
            pl.BlockSpec((CLS_ROWS, 128), lambda b, j: (0, 0)),
            pl.BlockSpec((1, HIST, C_CONV), lambda b, j: (b, 0, 0)),
        ],
        out_shape=[
            jax.ShapeDtypeStruct((n_rows_total, ROW_W), F32),
            jax.ShapeDtypeStruct((n_tiles, 8, TP), F32),
            jax.ShapeDtypeStruct((CLS_ROWS, 128), F32),
            jax.ShapeDtypeStruct((batch, HIST, C_CONV), F32),
        ],
        scratch_shapes=[
            pltpu.VMEM((HIST + TP, C_CONV), F32),
            pltpu.VMEM((TP, D_MODEL), BF16),
            pltpu.VMEM((CLS_ROWS, 128), F32),
        ],
        compiler_params=pltpu.CompilerParams(
            dimension_semantics=("arbitrary", "arbitrary"), vmem_limit_bytes=VMEM_LIMIT),
        name="prompt_mixer",
    )(x, *weights, tri)


def _sample_mixer(x_tm, hist_tm, weights, tri, h1a, row_block0):
    dec_seq, dec_batch, _ = x_tm.shape
    sblk = TP // dec_seq
    n_tiles = dec_batch // sblk
    nh = CONV_WIDTH - 1
    w_specs = [_full(w.shape) for w in weights]
    body = functools.partial(_sample_mixer_body, dec_seq=dec_seq, sblk=sblk)
    n_in = 2 + len(weights) + 2
    return pl.pallas_call(
        body,
        grid=(n_tiles,),
        in_specs=[pl.BlockSpec((dec_seq, sblk, D_MODEL), lambda i: (0, i, 0)),
                  pl.BlockSpec((nh, sblk, C_CONV), lambda i: (0, i, 0))]
        + w_specs + [_full(tri.shape), pl.BlockSpec(memory_space=pl.ANY)],
        out_specs=[
            pl.BlockSpec((TP, ROW_W), lambda i: (row_block0 + i, 0)),
            pl.BlockSpec((1, 8, TP), lambda i: (i, 0, 0)),
            pl.BlockSpec((CLS_ROWS, 128), lambda i: (0, 0)),
            pl.BlockSpec((nh, sblk, C_CONV), lambda i: (0, i, 0)),
            pl.BlockSpec((dec_seq, sblk, C_GMLP), lambda i: (0, i, 0)),
        ],
        out_shape=[
            jax.ShapeDtypeStruct(h1a.shape, F32),
            jax.ShapeDtypeStruct((n_tiles, 8, TP), F32),
            jax.ShapeDtypeStruct((CLS_ROWS, 128), F32),
            jax.ShapeDtypeStruct((nh, dec_batch, C_CONV), F32),
            jax.ShapeDtypeStruct((dec_seq, dec_batch, C_GMLP), F32),
        ],
        scratch_shapes=[
            pltpu.VMEM((nh + dec_seq, sblk, C_CONV), F32),
            pltpu.VMEM((TP, D_MODEL), BF16),
            pltpu.VMEM((CLS_ROWS, 128), F32),
        ],
        input_output_aliases={n_in - 1: 0},
        compiler_params=pltpu.CompilerParams(
            dimension_semantics=("arbitrary",), vmem_limit_bytes=VMEM_LIMIT),
        name="sample_mixer",
    )(x_tm, hist_tm, *weights, tri, h1a)


def _moe(tile_elo, tile_ehi, tile_valid, tile_oidx, src, h1a, g_ffn, w_gu, w_d):
    n_tiles = src.shape[0]
    smem_blk = functools.partial(pl.BlockSpec, (1, 1, TM), memory_space=pltpu.SMEM)
    grid_spec = pltpu.PrefetchScalarGridSpec(
        num_scalar_prefetch=4,
        grid=(n_tiles,),
        in_specs=[
            smem_blk(lambda i, *_: (0, 0, 0)),
            smem_blk(lambda i, *_: (jnp.minimum(i + 1, n_tiles - 1), 0, 0)),
            pl.BlockSpec(memory_space=pl.ANY),
            pl.BlockSpec((1, D_MODEL), lambda i, *_: (0, 0)),
            pl.BlockSpec((1, D_MODEL, 2 * D_EXPERT), lambda i, elo, ehi, va, oi: (elo[i], 0, 0)),
            pl.BlockSpec((1, D_MODEL, 2 * D_EXPERT), lambda i, elo, ehi, va, oi: (ehi[i], 0, 0)),
            pl.BlockSpec((1, D_EXPERT, D_MODEL), lambda i, elo, ehi, va, oi: (elo[i], 0, 0)),
            pl.BlockSpec((1, D_EXPERT, D_MODEL), lambda i, elo, ehi, va, oi: (ehi[i], 0, 0)),
        ],
        out_specs=pl.BlockSpec((TM, D_MODEL), lambda i, elo, ehi, va, oi: (oi[i], 0)),
        scratch_shapes=[pltpu.VMEM((2, TM, ROW_W), F32), pltpu.SemaphoreType.DMA((2,))],
    )
    return pl.pallas_call(
        _moe_body,
        grid_spec=grid_spec,
        out_shape=jax.ShapeDtypeStruct((n_tiles * TM, D_MODEL), F32),
        compiler_params=pltpu.CompilerParams(
            dimension_semantics=("arbitrary",), vmem_limit_bytes=VMEM_LIMIT),
        name="moe",
    )(tile_elo, tile_ehi, tile_valid, tile_oidx, src, src, h1a, g_ffn, w_gu, w_gu, w_d, w_d)


def _ple(dst, hs, p, g_ple, w_gate, w_proj, g_fin, name):
    n_tiles = dst.shape[0]
    smem_blk = functools.partial(pl.BlockSpec, (1, 1, TK), memory_space=pltpu.SMEM)
    return pl.pallas_call(
        _ple_body,
        grid=(n_tiles,),
        in_specs=[
            smem_blk(lambda i: (0, 0, 0)),
            smem_blk(lambda i: (jnp.minimum(i + 1, n_tiles - 1), 0, 0)),
            pl.BlockSpec(memory_space=pl.ANY),
            pl.BlockSpec((TK, PLE_DIM), lambda i: (i, 0)),
            _full(g_ple.shape), _full(w_gate.shape), _full(w_proj.shape), _full(g_fin.shape),
        ],
        out_specs=pl.BlockSpec((TK, D_MODEL), lambda i: (i, 0)),
        out_shape=jax.ShapeDtypeStruct((n_tiles * TK, D_MODEL), F32),
        scratch_shapes=[pltpu.VMEM((2, TK, D_MODEL), F32), pltpu.SemaphoreType.DMA((2,))],
        compiler_params=pltpu.CompilerParams(
            dimension_semantics=("arbitrary",), vmem_limit_bytes=VMEM_LIMIT),
        name=name,
    )(dst, dst, hs, p, g_ple, w_gate, w_proj, g_fin)


def kernel(x_prompt, x_sample, p_prompt, p_sample, state_conv, norm_mix_g, w_in, conv_w, conv_b, conv_ln_g, conv_ln_b, gmlp_ln_g, gmlp_ln_b, w_s, b_s, w_out, norm_ffn_g, w_router_group, b_router_group, w_router_expert, b_router_expert, w_exp_gate, w_exp_up, w_exp_down, norm_ple_g, w_ple_gate, w_ple_proj, norm_final_g):
    depth = w_in.shape[0]
    assert depth == 1, "single-layer pipeline"
    batch, seq, _ = x_prompt.shape
    dec_batch, dec_seq, _ = x_sample.shape
    assert seq % TP == 0 and TP % CHUNK == 0 and TP % dec_seq == 0 and dec_batch % (TP // dec_seq) == 0
    n_prompt = batch * seq
    n_sample = dec_batch * dec_seq
    n_tok = n_prompt + n_sample
    assert n_prompt % TK == 0 and n_sample % TK == 0

    row = lambda a: a.reshape(1, -1)
    w_in_b = w_in[0].astype(BF16)
    w_out_b = w_out[0].astype(BF16)
    cw = jnp.concatenate([conv_w[0], jnp.zeros((1, C_CONV), F32)], axis=0)
    tril = jnp.tril(jnp.ones((CHUNK, CHUNK), bool))
    ws_m = jnp.where(tril[None], w_s[0], 0.0)
    ws_cat = jnp.concatenate([ws_m[0::2], ws_m[1::2]], axis=2).astype(BF16)
    bs_lane = jnp.repeat(jnp.transpose(b_s[0]), HEAD_DIM, axis=1)
    w_rt = jnp.zeros((CLS_ROWS, D_MODEL), F32)
    w_rt = w_rt.at[0:N_GROUPS].set(jnp.transpose(w_router_group[0]))
    w_rt = w_rt.at[N_GROUPS:N_GROUPS + N_EXPERTS].set(jnp.transpose(w_router_expert[0])).astype(BF16)
    b_r = jnp.zeros((CLS_ROWS, 1), F32)
    b_r = b_r.at[0:N_GROUPS, 0].set(b_router_group[0]).at[N_GROUPS:N_GROUPS + N_EXPERTS, 0].set(b_router_expert[0])
    tri = jnp.triu(jnp.ones((TP, TP), F32)).astype(BF16)
    wsl = jnp.where(jnp.tril(jnp.ones((dec_seq, dec_seq), bool))[None], w_s[0][:, :dec_seq, :dec_seq], 0.0)
    wsl = jnp.repeat(jnp.transpose(wsl, (1, 2, 0)).reshape(dec_seq * dec_seq, N_HEADS), HEAD_DIM, axis=1)
    bsl = jnp.repeat(jnp.transpose(b_s[0][:, :dec_seq]), HEAD_DIM, axis=1)
    w_gu = jnp.concatenate([w_exp_gate[0], w_exp_up[0]], axis=2).astype(BF16)
    w_d = w_exp_down[0].astype(BF16)

    common = (row(norm_mix_g[0]), w_in_b, cw, row(conv_b[0]), row(conv_ln_g[0]), row(conv_ln_b[0]),
              row(gmlp_ln_g[0]), row(gmlp_ln_b[0]))
    tail = (w_out_b, row(norm_ffn_g[0]), w_rt, b_r)

    h1a, meta_p, cnt_p, cst_p = _prompt_mixer(x_prompt, common + (ws_cat, bs_lane) + tail, tri, n_tok)
    x_tm = jnp.transpose(x_sample, (1, 0, 2))
    hist_tm = jnp.transpose(state_conv[0], (1, 0, 2))
    h1a, meta_s, cnt_s, cst_s, v_s = _sample_mixer(x_tm, hist_tm, common + (wsl, bsl) + tail, tri, h1a,
                                                   n_prompt // TP)

    cls_p = meta_p[:, 0, :].reshape(-1).astype(jnp.int32)
    rank_p = meta_p[:, 1, :].reshape(-1).astype(jnp.int32)
    cls_s = meta_s[:, 0, :].reshape(-1).astype(jnp.int32)
    rank_s = meta_s[:, 1, :].reshape(-1).astype(jnp.int32)
    cnt_p = cnt_p[:N_CLASSES, 0].astype(jnp.int32)
    cnt_s = cnt_s[:N_CLASSES, 0].astype(jnp.int32)
    padded = ((cnt_p + cnt_s + TM - 1) // TM) * TM
    ends = jnp.cumsum(padded)
    offs = ends - padded
    cls_ids = jnp.arange(N_CLASSES, dtype=jnp.int32)
    oh_p = cls_p[:, None] == cls_ids[None, :]
    oh_s = cls_s[:, None] == cls_ids[None, :]
    dest_p = jnp.sum(jnp.where(oh_p, offs[None, :], 0), axis=1) + rank_p
    dest_s = jnp.sum(jnp.where(oh_s, (offs + cnt_p)[None, :], 0), axis=1) + rank_s
    dest = jnp.concatenate([dest_p, dest_s])
    n_tiles = (n_tok + N_CLASSES * (TM - 1)) // TM + 1
    src = jnp.zeros((n_tiles * TM,), jnp.int32).at[dest].set(jnp.arange(n_tok, dtype=jnp.int32))
    tile_start = jnp.arange(n_tiles, dtype=jnp.int32) * TM
    n_used = ends[-1] // TM
    tile_valid = (tile_start < ends[-1]).astype(jnp.int32)
    tile_oidx = jnp.minimum(jnp.arange(n_tiles, dtype=jnp.int32), n_used - 1)
    tile_cls = jnp.sum((tile_start[:, None] >= ends[None, :]).astype(jnp.int32), axis=1)
    tile_cls = jnp.take(tile_cls, tile_oidx)
    pair_lo = jnp.asarray(np.array([p[0] for p in _PAIRS], np.int32))
    pair_hi = jnp.asarray(np.array([p[1] for p in _PAIRS], np.int32))
    tile_elo = (tile_cls // 6) * EPG + jnp.take(pair_lo, tile_cls % 6)
    tile_ehi = (tile_cls // 6) * EPG + jnp.take(pair_hi, tile_cls % 6)

    hs = _moe(tile_elo, tile_ehi, tile_valid, tile_oidx, src.reshape(n_tiles, 1, TM), h1a,
              row(norm_ffn_g[0]), w_gu, w_d)

    ple_w = (row(norm_ple_g[0]), w_ple_gate[0].astype(BF16), w_ple_proj[0].astype(BF16), row(norm_final_g))
    y_p = _ple(dest_p.reshape(-1, 1, TK), hs, p_prompt[0].reshape(n_prompt, PLE_DIM), *ple_w, name="ple_prompt")
    sblk = TP // dec_seq
    p_s_tm = jnp.transpose(p_sample[0].reshape(dec_batch // sblk, sblk, dec_seq, PLE_DIM), (0, 2, 1, 3))
    y_s = _ple(dest_s.reshape(-1, 1, TK), hs, p_s_tm.reshape(n_sample, PLE_DIM), *ple_w, name="ple_sample")

    y_prompt = y_p.reshape(batch, seq, D_MODEL)
    y_sample = jnp.transpose(y_s.reshape(dec_batch // sblk, dec_seq, sblk, D_MODEL), (0, 2, 1, 3))
    y_sample = y_sample.reshape(dec_batch, dec_seq, D_MODEL)
    state_conv_prompt = cst_p[:, HIST_OFF:, :][None]
    state_conv_sample = jnp.transpose(cst_s, (1, 0, 2))[None]
    state_gmlp_v_sample = jnp.transpose(v_s, (1, 0, 2))[None]
    return (y_prompt, y_sample, state_conv_prompt, state_conv_sample, state_gmlp_v_sample)
```

```python
import functools

import numpy as np
import jax
import jax.numpy as jnp
from jax import lax
from jax.experimental import pallas as pl
from jax.experimental.pallas import tpu as pltpu

F32 = jnp.float32
BF16 = jnp.bfloat16

D_MODEL = 1024
C_CONV = 512
C_GMLP = 512
N_HEADS = 8
HEAD_DIM = 64
CONV_WIDTH = 31
CHUNK = 128
PLE_DIM = 256
N_GROUPS = 4
EPG = 4
N_EXPERTS = 16
D_EXPERT = 512
EPS = 1e-6
LANES = 128

N_CLASSES = N_GROUPS * 6
CLS_ROWS = 32
HIST = 32
HIST_OFF = HIST - (CONV_WIDTH - 1)
D_ROWS = D_MODEL // LANES
ROW_PITCH = D_ROWS + 2

TP = 512
TM = 256
TK = 512
CONV_ROWS = 64

VMEM_LIMIT = 56 * 1024 * 1024

_PAIRS = ((0, 1), (0, 2), (0, 3), (1, 2), (1, 3), (2, 3))


def _rms(x, g):
    ms = jnp.mean(x * x, axis=-1, keepdims=True)
    return x * lax.rsqrt(ms + EPS) * g


def _ln(x, g, b):
    mu = jnp.mean(x, axis=-1, keepdims=True)
    xc = x - mu
    var = jnp.mean(xc * xc, axis=-1, keepdims=True)
    return xc * lax.rsqrt(var + EPS) * g + b


def _max4(v):
    return jnp.maximum(jnp.maximum(v[0], v[1]), jnp.maximum(v[2], v[3]))


def _first4(v, m):
    return jnp.where(v[0] == m, 0, jnp.where(v[1] == m, 1, jnp.where(v[2] == m, 2, 3))).astype(jnp.int32)


def _route(lt):
    gl = [lt[i:i + 1, :] for i in range(N_GROUPS)]
    m = _max4(gl)
    g = _first4(gl, m)
    den = jnp.exp(gl[0] - m) + jnp.exp(gl[1] - m) + jnp.exp(gl[2] - m) + jnp.exp(gl[3] - m)
    g_w = 1.0 / den
    a = []
    for j in range(EPG):
        rows = [lt[N_GROUPS + EPG * q + j:N_GROUPS + EPG * q + j + 1, :] for q in range(N_GROUPS)]
        a.append(jnp.where(g == 0, rows[0], jnp.where(g == 1, rows[1], jnp.where(g == 2, rows[2], rows[3]))))
    v1 = _max4(a)
    i1 = _first4(a, v1)
    a2 = [jnp.where(i1 == j, -jnp.inf, a[j]) for j in range(EPG)]
    v2 = _max4(a2)
    i2 = _first4(a2, v2)
    e = jnp.exp(v2 - v1)
    s = 1.0 + e
    w1 = g_w / s
    w2 = g_w * e / s
    first_lo = i1 < i2
    lo = jnp.minimum(i1, i2)
    hi = jnp.maximum(i1, i2)
    w_lo = jnp.where(first_lo, w1, w2)
    w_hi = jnp.where(first_lo, w2, w1)
    pair = jnp.where(lo == 0, hi - 1, jnp.where(lo == 1, hi + 1, 5))
    return g * 6 + pair, w_lo, w_hi


def _load_token_rows(ref, n_tok, pitch):
    return jnp.concatenate([ref[pl.ds(s, n_tok, stride=pitch), :] for s in range(D_ROWS)], axis=1)


def _store_token_rows(ref, x, n_tok, pitch):
    for s in range(D_ROWS):
        ref[pl.ds(s, n_tok, stride=pitch), :] = x[:, s * LANES:(s + 1) * LANES]


def _router_tail(h, g_ffn_ref, w_rt_ref, b_r_ref, tri_ref, carry_ref, h1a_ref, meta_ref, cnt_ref, t):
    c = _rms(h, g_ffn_ref[...]).astype(BF16)
    lt = lax.dot_general(w_rt_ref[...], c, (((1,), (1,)), ((), ())), preferred_element_type=F32)
    lt = lt + b_r_ref[...]
    cls, w_lo, w_hi = _route(lt)
    rows = lax.broadcasted_iota(jnp.int32, (CLS_ROWS, t), 0)
    ohf = jnp.where(rows == cls, 1.0, 0.0).astype(F32)
    pre = jnp.dot(ohf.astype(BF16), tri_ref[...], preferred_element_type=F32)
    carry = carry_ref[:, 0:1]
    rank = jnp.sum(ohf * (pre - 1.0 + carry), axis=0, keepdims=True)
    new_cnt = jnp.broadcast_to(carry + jnp.sum(ohf, axis=1, keepdims=True), (CLS_ROWS, LANES))
    carry_ref[...] = new_cnt
    cnt_ref[...] = new_cnt
    meta_ref[0] = jnp.concatenate([cls.astype(F32), rank, jnp.zeros((6, t), F32)], axis=0)
    _store_token_rows(h1a_ref, h, t, ROW_PITCH)
    h1a_ref[pl.ds(D_ROWS, t, stride=ROW_PITCH), :] = jnp.transpose(jnp.broadcast_to(w_lo, (LANES, t)))
    h1a_ref[pl.ds(D_ROWS + 1, t, stride=ROW_PITCH), :] = jnp.transpose(jnp.broadcast_to(w_hi, (LANES, t)))


def _mixer_front(x, g_mix_ref, w_in_ref):
    a = _rms(x, g_mix_ref[...]).astype(BF16)
    z = jnp.dot(a, w_in_ref[...], preferred_element_type=F32)
    glu = z[:, 0:C_CONV] * jax.nn.sigmoid(z[:, C_CONV:2 * C_CONV])
    u = jax.nn.gelu(z[:, 2 * C_CONV:2 * C_CONV + C_GMLP])
    gv = jax.nn.gelu(z[:, 2 * C_CONV + C_GMLP:])
    return glu, u, gv


def _mixer_body(xp_ref, xs_ref, hist_ref, g_mix_ref, w_in_ref, cw_ref, cb_ref, clg_ref, clb_ref, glg_ref, glb_ref,
                ws_ref, bs_ref, wsl_ref, bsl_ref, w_out_ref, g_ffn_ref, w_rt_ref, b_r_ref, tri_ref,
                h1a_ref, meta_ref, cnt_ref, cstp_ref, csts_ref, vs_ref,
                glu_scr, xs_scr, cat_scr, carry_ref, *, n_prompt_tiles, nj, dec_seq, sblk):
    i = pl.program_id(0)

    @pl.when(i == 0)
    def _():
        carry_ref[...] = jnp.zeros_like(carry_ref)

    def tail(x):
        h = x + jnp.dot(cat_scr[...], w_out_ref[...], preferred_element_type=F32)
        _router_tail(h, g_ffn_ref, w_rt_ref, b_r_ref, tri_ref, carry_ref, h1a_ref, meta_ref, cnt_ref, TP)

    @pl.when(i < n_prompt_tiles)
    def _prompt():
        j = lax.rem(i, nj)
        x = xp_ref[0]
        glu, u, gv = _mixer_front(x, g_mix_ref, w_in_ref)

        @pl.when(j == 0)
        def _():
            glu_scr[0:HIST, :] = jnp.zeros((HIST, C_CONV), F32)

        @pl.when(j > 0)
        def _():
            glu_scr[0:HIST, :] = glu_scr[TP:TP + HIST, :]

        glu_scr[HIST:HIST + TP, :] = glu
        cstp_ref[0] = glu_scr[TP:TP + HIST, :]

        cb = cb_ref[...]
        for r0 in range(0, TP, CONV_ROWS):
            acc = jnp.zeros((CONV_ROWS, C_CONV), F32)
            for k in range(CONV_WIDTH):
                s0 = r0 + HIST_OFF + k
                acc = acc + cw_ref[k:k + 1, :] * glu_scr[s0:s0 + CONV_ROWS, :]
            ya = jax.nn.silu(_ln(acc + cb, clg_ref[...], clb_ref[...]))
            cat_scr[r0:r0 + CONV_ROWS, 0:C_CONV] = ya.astype(BF16)

        v = _ln(gv, glg_ref[...], glb_ref[...])
        vb = v.astype(BF16)
        lane = lax.broadcasted_iota(jnp.int32, (CHUNK, LANES), 1)
        is_lo = lane < HEAD_DIM
        zero = jnp.zeros((CHUNK, LANES), BF16)
        for c in range(TP // CHUNK):
            for p in range(N_HEADS // 2):
                blk = vb[c * CHUNK:(c + 1) * CHUNK, p * LANES:(p + 1) * LANES]
                rhs = jnp.concatenate([jnp.where(is_lo, blk, zero), jnp.where(is_lo, zero, blk)], axis=0)
                mixed = jnp.dot(ws_ref[p], rhs, preferred_element_type=F32) + bs_ref[:, p * LANES:(p + 1) * LANES]
                yb = u[c * CHUNK:(c + 1) * CHUNK, p * LANES:(p + 1) * LANES] * mixed
                cat_scr[c * CHUNK:(c + 1) * CHUNK, C_CONV + p * LANES:C_CONV + (p + 1) * LANES] = yb.astype(BF16)
        tail(x)

    @pl.when(i >= n_prompt_tiles)
    def _sample():
        x = xs_ref[...].reshape(TP, D_MODEL)
        glu, u, gv = _mixer_front(x, g_mix_ref, w_in_ref)

        nh = CONV_WIDTH - 1
        xs_scr[0:nh] = hist_ref[...]
        xs_scr[nh:nh + dec_seq] = glu.reshape(dec_seq, sblk, C_CONV)
        csts_ref[...] = xs_scr[dec_seq:dec_seq + nh]

        cb = cb_ref[...]
        for t in range(dec_seq):
            acc = jnp.zeros((sblk, C_CONV), F32)
            for k in range(CONV_WIDTH):
                acc = acc + cw_ref[k:k + 1, :] * xs_scr[t + k]
            ya = jax.nn.silu(_ln(acc + cb, clg_ref[...], clb_ref[...]))
            cat_scr[t * sblk:(t + 1) * sblk, 0:C_CONV] = ya.astype(BF16)

        v = _ln(gv, glg_ref[...], glb_ref[...])
        vs_ref[...] = v.reshape(dec_seq, sblk, C_GMLP)
        for t in range(dec_seq):
            mixed = jnp.broadcast_to(bsl_ref[t:t + 1, :], (sblk, C_GMLP))
            for tp in range(t + 1):
                r = t * dec_seq + tp
                mixed = mixed + wsl_ref[r:r + 1, :] * v[tp * sblk:(tp + 1) * sblk, :]
            yb = u[t * sblk:(t + 1) * sblk, :] * mixed
            cat_scr[t * sblk:(t + 1) * sblk, C_CONV:] = yb.astype(BF16)
        tail(x)


def _for_rows(n, fn):
    ng = lax.shift_right_logical(n, 3)

    def group(g, carry):
        for k in range(8):
            fn(g * 8 + k)
        return carry

    lax.fori_loop(0, ng, group, 0)

    def single(r, carry):
        fn(r)
        return carry

    lax.fori_loop(ng * 8, n, single, 0)


def _gather_copy(idx_ref, h1a_hbm, buf, sem, slot, r):
    tok = 0 if idx_ref is None else idx_ref[0, 0, r]
    return pltpu.make_async_copy(h1a_hbm.at[pl.ds(tok * ROW_PITCH, ROW_PITCH), :],
                                 buf.at[slot, pl.ds(r * ROW_PITCH, ROW_PITCH), :], sem.at[slot])


def _scatter_copy(idx_ref, obuf, h2_hbm, sem, slot, r):
    tok = 0 if idx_ref is None else idx_ref[0, 0, r]
    return pltpu.make_async_copy(obuf.at[slot, pl.ds(r * D_ROWS, D_ROWS), :],
                                 h2_hbm.at[pl.ds(tok * D_ROWS, D_ROWS), :], sem.at[slot])


def _moe_body(elo_ref, ehi_ref, nval_ref,
              src0_ref, srcn_ref, srcc_ref,
              h1a_hbm, g_ffn_ref, wgu_lo_ref, wgu_hi_ref, wd_lo_ref, wd_hi_ref,
              h2_hbm, buf, obuf, gsem, ssem):
    del elo_ref, ehi_ref
    i = pl.program_id(0)
    n = pl.num_programs(0)
    slot = lax.rem(i, 2)
    all_rows = jnp.int32(TM)

    @pl.when(i == 0)
    def _():
        _for_rows(all_rows, lambda r: _gather_copy(src0_ref, h1a_hbm, buf, gsem, 0, r).start())

    nxt = jnp.minimum(i + 1, n - 1)

    @pl.when(jnp.logical_and(i + 1 < n, nval_ref[nxt] > 0))
    def _():
        _for_rows(all_rows, lambda r: _gather_copy(srcn_ref, h1a_hbm, buf, gsem, 1 - slot, r).start())

    @pl.when(i >= 2)
    def _():
        _for_rows(nval_ref[jnp.maximum(i - 2, 0)],
                  lambda r: _scatter_copy(None, obuf, h2_hbm, ssem, slot, 0).wait())

    @pl.when(nval_ref[i] > 0)
    def _():
        _for_rows(all_rows, lambda r: _gather_copy(None, h1a_hbm, buf, gsem, slot, 0).wait())
        tile = buf.at[slot]
        h = _load_token_rows(tile, TM, ROW_PITCH)
        wl = tile[pl.ds(D_ROWS, TM, stride=ROW_PITCH), :]
        wh = tile[pl.ds(D_ROWS + 1, TM, stride=ROW_PITCH), :]
        c = _rms(h, g_ffn_ref[...]).astype(BF16)
        gu = jnp.dot(c, wgu_lo_ref[0], preferred_element_type=F32)
        hl = (jax.nn.silu(gu[:, 0:D_EXPERT]) * gu[:, D_EXPERT:]).astype(BF16)
        gu = jnp.dot(c, wgu_hi_ref[0], preferred_element_type=F32)
        hh = (jax.nn.silu(gu[:, 0:D_EXPERT]) * gu[:, D_EXPERT:]).astype(BF16)
        yl = jnp.dot(hl, wd_lo_ref[0], preferred_element_type=F32)
        yh = jnp.dot(hh, wd_hi_ref[0], preferred_element_type=F32)
        wl8 = jnp.concatenate([wl] * D_ROWS, axis=1)
        wh8 = jnp.concatenate([wh] * D_ROWS, axis=1)
        _store_token_rows(obuf.at[slot], h + (wl8 * yl + wh8 * yh), TM, D_ROWS)
        _for_rows(nval_ref[i], lambda r: _scatter_copy(srcc_ref, obuf, h2_hbm, ssem, slot, r).start())

    @pl.when(i == n - 1)
    def _():
        _for_rows(nval_ref[jnp.maximum(i - 1, 0)],
                  lambda r: _scatter_copy(None, obuf, h2_hbm, ssem, 1 - slot, 0).wait())
        _for_rows(nval_ref[i], lambda r: _scatter_copy(None, obuf, h2_hbm, ssem, slot, 0).wait())


def _ple_body(h2_ref, p_ref, g_ple_ref, w_gate_ref, w_proj_ref, g_fin_ref, y_ref):
    h = _load_token_rows(h2_ref, TK, D_ROWS)
    c = _rms(h, g_ple_ref[...]).astype(BF16)
    gate = jax.nn.sigmoid(jnp.dot(c, w_gate_ref[...], preferred_element_type=F32))
    proj = jnp.dot(p_ref[...].astype(BF16), w_proj_ref[...], preferred_element_type=F32)
    h = h + proj * gate
    y_ref[...] = _rms(h, g_fin_ref[...])


def _full(shape, single=False):
    nd = len(shape)
    if single:
        return pl.BlockSpec(shape, lambda *_: (0,) * nd, pipeline_mode=pl.Buffered(1))
    return pl.BlockSpec(shape, lambda *_: (0,) * nd)


def _mixer(x_prompt, x_tm, hist_tm, weights, tri):
    batch, seq, _ = x_prompt.shape
    dec_seq, dec_batch, _ = x_tm.shape
    nj = seq // TP
    npt = batch * nj
    sblk = TP // dec_seq
    nst = dec_batch // sblk
    n_tiles = npt + nst
    nh = CONV_WIDTH - 1
    w_specs = [_full(w.shape, single=True) for w in weights]
    body = functools.partial(_mixer_body, n_prompt_tiles=npt, nj=nj, dec_seq=dec_seq, sblk=sblk)

    def p_idx(i):
        return jnp.minimum(i, npt - 1)

    def s_idx(i):
        return jnp.maximum(i - npt, 0)

    return pl.pallas_call(
        body,
        grid=(n_tiles,),
        in_specs=[pl.BlockSpec((1, TP, D_MODEL), lambda i: (p_idx(i) // nj, p_idx(i) % nj, 0)),
                  pl.BlockSpec((dec_seq, sblk, D_MODEL), lambda i: (0, s_idx(i), 0), pipeline_mode=pl.Buffered(1)),
                  pl.BlockSpec((nh, sblk, C_CONV), lambda i: (0, s_idx(i), 0), pipeline_mode=pl.Buffered(1))]
        + w_specs + [_full(tri.shape, single=True)],
        out_specs=[
            pl.BlockSpec((TP * ROW_PITCH, LANES), lambda i: (i, 0)),
            pl.BlockSpec((1, 8, TP), lambda i: (i, 0, 0)),
            pl.BlockSpec((CLS_ROWS, LANES), lambda i: (0, 0)),
            pl.BlockSpec((1, HIST, C_CONV), lambda i: (p_idx(i) // nj, 0, 0)),
            pl.BlockSpec((nh, sblk, C_CONV), lambda i: (0, s_idx(i), 0)),
            pl.BlockSpec((dec_seq, sblk, C_GMLP), lambda i: (0, s_idx(i), 0)),
        ],
        out_shape=[
            jax.ShapeDtypeStruct((n_tiles * TP * ROW_PITCH, LANES), F32),
            jax.ShapeDtypeStruct((n_tiles, 8, TP), F32),
            jax.ShapeDtypeStruct((CLS_ROWS, LANES), F32),
            jax.ShapeDtypeStruct((batch, HIST, C_CONV), F32),
            jax.ShapeDtypeStruct((nh, dec_batch, C_CONV), F32),
            jax.ShapeDtypeStruct((dec_seq, dec_batch, C_GMLP), F32),
        ],
        scratch_shapes=[
            pltpu.VMEM((HIST + TP, C_CONV), F32),
            pltpu.VMEM((nh + dec_seq, sblk, C_CONV), F32),
            pltpu.VMEM((TP, D_MODEL), BF16),
            pltpu.VMEM((CLS_ROWS, LANES), F32),
        ],
        compiler_params=pltpu.CompilerParams(
            dimension_semantics=("arbitrary",), vmem_limit_bytes=VMEM_LIMIT),
        name="mixer",
    )(x_prompt, x_tm, hist_tm, *weights, tri)


def _moe(tile_elo, tile_ehi, tile_nval, src, h1a, g_ffn, w_gu, w_d, n_tok):
    n_tiles = src.shape[0]
    smem_blk = functools.partial(pl.BlockSpec, (1, 1, TM), memory_space=pltpu.SMEM)
    grid_spec = pltpu.PrefetchScalarGridSpec(
        num_scalar_prefetch=3,
        grid=(n_tiles,),
        in_specs=[
            smem_blk(lambda i, *_: (0, 0, 0)),
            smem_blk(lambda i, *_: (jnp.minimum(i + 1, n_tiles - 1), 0, 0)),
            smem_blk(lambda i, *_: (i, 0, 0)),
            pl.BlockSpec(memory_space=pl.ANY),
            pl.BlockSpec((1, D_MODEL), lambda i, *_: (0, 0)),
            pl.BlockSpec((1, D_MODEL, 2 * D_EXPERT), lambda i, elo, ehi, nv: (elo[i], 0, 0)),
            pl.BlockSpec((1, D_MODEL, 2 * D_EXPERT), lambda i, elo, ehi, nv: (ehi[i], 0, 0)),
            pl.BlockSpec((1, D_EXPERT, D_MODEL), lambda i, elo, ehi, nv: (elo[i], 0, 0)),
            pl.BlockSpec((1, D_EXPERT, D_MODEL), lambda i, elo, ehi, nv: (ehi[i], 0, 0)),
        ],
        out_specs=pl.BlockSpec(memory_space=pl.ANY),
        scratch_shapes=[pltpu.VMEM((2, TM * ROW_PITCH, LANES), F32),
                        pltpu.VMEM((2, TM * D_ROWS, LANES), F32),
                        pltpu.SemaphoreType.DMA((2,)),
                        pltpu.SemaphoreType.DMA((2,))],
    )
    return pl.pallas_call(
        _moe_body,
        grid_spec=grid_spec,
        out_shape=jax.ShapeDtypeStruct((n_tok * D_ROWS, LANES), F32),
        compiler_params=pltpu.CompilerParams(
            dimension_semantics=("arbitrary",), vmem_limit_bytes=VMEM_LIMIT),
        name="moe",
    )(tile_elo, tile_ehi, tile_nval, src, src, src, h1a, g_ffn, w_gu, w_gu, w_d, w_d)


def _ple(h2, tile0, p, g_ple, w_gate, w_proj, g_fin, name):
    n_tiles = p.shape[0] // TK
    return pl.pallas_call(
        _ple_body,
        grid=(n_tiles,),
        in_specs=[
            pl.BlockSpec((TK * D_ROWS, LANES), lambda i: (tile0 + i, 0)),
            pl.BlockSpec((TK, PLE_DIM), lambda i: (i, 0)),
            _full(g_ple.shape), _full(w_gate.shape), _full(w_proj.shape), _full(g_fin.shape),
        ],
        out_specs=pl.BlockSpec((TK, D_MODEL), lambda i: (i, 0)),
        out_shape=jax.ShapeDtypeStruct((n_tiles * TK, D_MODEL), F32),
        compiler_params=pltpu.CompilerParams(
            dimension_semantics=("arbitrary",), vmem_limit_bytes=VMEM_LIMIT),
        name=name,
    )(h2, p, g_ple, w_gate, w_proj, g_fin)


def kernel(x_prompt, x_sample, p_prompt, p_sample, state_conv, norm_mix_g, w_in, conv_w, conv_b, conv_ln_g, conv_ln_b, gmlp_ln_g, gmlp_ln_b, w_s, b_s, w_out, norm_ffn_g, w_router_group, b_router_group, w_router_expert, b_router_expert, w_exp_gate, w_exp_up, w_exp_down, norm_ple_g, w_ple_gate, w_ple_proj, norm_final_g):
    depth = w_in.shape[0]
    assert depth == 1, "single-layer pipeline"
    batch, seq, _ = x_prompt.shape
    dec_batch, dec_seq, _ = x_sample.shape
    assert seq % TP == 0 and TP % CHUNK == 0 and TP % dec_seq == 0 and dec_batch % (TP // dec_seq) == 0
    assert TP == TK
    sblk = TP // dec_seq
    n_prompt = batch * seq
    n_sample = dec_batch * dec_seq
    n_tok = n_prompt + n_sample

    row = lambda a: a.reshape(1, -1)
    w_in_b = w_in[0].astype(BF16)
    w_out_b = w_out[0].astype(BF16)
    cw = jnp.concatenate([conv_w[0], jnp.zeros((1, C_CONV), F32)], axis=0)
    tril = jnp.tril(jnp.ones((CHUNK, CHUNK), bool))
    ws_m = jnp.where(tril[None], w_s[0], 0.0)
    ws_cat = jnp.concatenate([ws_m[0::2], ws_m[1::2]], axis=2).astype(BF16)
    bs_lane = jnp.repeat(jnp.transpose(b_s[0]), HEAD_DIM, axis=1)
    w_rt = jnp.zeros((CLS_ROWS, D_MODEL), F32)
    w_rt = w_rt.at[0:N_GROUPS].set(jnp.transpose(w_router_group[0]))
    w_rt = w_rt.at[N_GROUPS:N_GROUPS + N_EXPERTS].set(jnp.transpose(w_router_expert[0])).astype(BF16)
    b_r = jnp.zeros((CLS_ROWS, 1), F32)
    b_r = b_r.at[0:N_GROUPS, 0].set(b_router_group[0]).at[N_GROUPS:N_GROUPS + N_EXPERTS, 0].set(b_router_expert[0])
    tri = jnp.triu(jnp.ones((TP, TP), F32)).astype(BF16)
    wsl = jnp.where(jnp.tril(jnp.ones((dec_seq, dec_seq), bool))[None], w_s[0][:, :dec_seq, :dec_seq], 0.0)
    wsl = jnp.repeat(jnp.transpose(wsl, (1, 2, 0)).reshape(dec_seq * dec_seq, N_HEADS), HEAD_DIM, axis=1)
    bsl = jnp.repeat(jnp.transpose(b_s[0][:, :dec_seq]), HEAD_DIM, axis=1)
    w_gu = jnp.concatenate([w_exp_gate[0], w_exp_up[0]], axis=2).astype(BF16)
    w_d = w_exp_down[0].astype(BF16)

    weights = (row(norm_mix_g[0]), w_in_b, cw, row(conv_b[0]), row(conv_ln_g[0]), row(conv_ln_b[0]),
               row(gmlp_ln_g[0]), row(gmlp_ln_b[0]), ws_cat, bs_lane, wsl, bsl,
               w_out_b, row(norm_ffn_g[0]), w_rt, b_r)

    x_tm = jnp.transpose(x_sample, (1, 0, 2))
    hist_tm = jnp.transpose(state_conv[0], (1, 0, 2))
    h1a, meta, cnt, cst_p, cst_s, v_s = _mixer(x_prompt, x_tm, hist_tm, weights, tri)

    cls = meta[:, 0, :].reshape(-1).astype(jnp.int32)
    rank = meta[:, 1, :].reshape(-1).astype(jnp.int32)
    cnt = cnt[:N_CLASSES, 0].astype(jnp.int32)
    padded = ((cnt + TM - 1) // TM) * TM
    ends = jnp.cumsum(padded)
    offs = ends - padded
    cls_ids = jnp.arange(N_CLASSES, dtype=jnp.int32)
    dest = jnp.sum(jnp.where(cls[:, None] == cls_ids[None, :], offs[None, :], 0), axis=1) + rank
    n_tiles = (n_tok + N_CLASSES * (TM - 1)) // TM + 1
    src = jnp.zeros((n_tiles * TM,), jnp.int32).at[dest].set(
        jnp.arange(n_tok, dtype=jnp.int32), unique_indices=True, mode="promise_in_bounds")
    tile_start = jnp.arange(n_tiles, dtype=jnp.int32) * TM
    n_used = ends[-1] // TM
    tile_cls = jnp.sum((tile_start[:, None] >= ends[None, :]).astype(jnp.int32), axis=1)
    tile_cls_c = jnp.minimum(tile_cls, N_CLASSES - 1)
    tile_nval = jnp.clip(jnp.take(offs + cnt, tile_cls_c) - tile_start, 0, TM)
    tile_nval = jnp.where(tile_cls < N_CLASSES, tile_nval, 0).astype(jnp.int32)
    last_cls = jnp.take(tile_cls, n_used - 1)
    tile_cls = jnp.where(tile_start < ends[-1], tile_cls, last_cls)
    pair_lo = jnp.asarray(np.array([p[0] for p in _PAIRS], np.int32))
    pair_hi = jnp.asarray(np.array([p[1] for p in _PAIRS], np.int32))
    tile_elo = (tile_cls // 6) * EPG + jnp.take(pair_lo, tile_cls % 6)
    tile_ehi = (tile_cls // 6) * EPG + jnp.take(pair_hi, tile_cls % 6)

    h2 = _moe(tile_elo, tile_ehi, tile_nval, src.reshape(n_tiles, 1, TM), h1a,
              row(norm_ffn_g[0]), w_gu, w_d, n_tok)

    ple_w = (row(norm_ple_g[0]), w_ple_gate[0].astype(BF16), w_ple_proj[0].astype(BF16), row(norm_final_g))
    y_p = _ple(h2, 0, p_prompt[0].reshape(n_prompt, PLE_DIM), *ple_w, name="ple_prompt")
    p_s_tm = jnp.transpose(p_sample[0].reshape(dec_batch // sblk, sblk, dec_seq, PLE_DIM), (0, 2, 1, 3))
    y_s = _ple(h2, n_prompt // TK, p_s_tm.reshape(n_sample, PLE_DIM), *ple_w, name="ple_sample")

    y_prompt = y_p.reshape(batch, seq, D_MODEL)
    y_sample = jnp.transpose(y_s.reshape(dec_batch // sblk, dec_seq, sblk, D_MODEL), (0, 2, 1, 3))
    y_sample = y_sample.reshape(dec_batch, dec_seq, D_MODEL)
    state_conv_prompt = cst_p[:, HIST_OFF:, :][None]
    state_conv_sample = jnp.transpose(cst_s, (1, 0, 2))[None]
    state_gmlp_v_sample = jnp.transpose(v_s, (1, 0, 2))[None]
    return (y_prompt, y_sample, state_conv_prompt, state_conv_sample, state_gmlp_v_sample)
```

```python
import functools

import numpy as np
import jax
import jax.numpy as jnp
from jax import lax
from jax.experimental import pallas as pl
from jax.experimental.pallas import tpu as pltpu

F32 = jnp.float32
BF16 = jnp.bfloat16

D_MODEL = 1024
C_CONV = 512
C_GMLP = 512
N_HEADS = 8
HEAD_DIM = 64
CONV_WIDTH = 31
CHUNK = 128
PLE_DIM = 256
N_GROUPS = 4
EPG = 4
N_EXPERTS = 16
D_EXPERT = 512
EPS = 1e-6
LANES = 128

N_CLASSES = N_GROUPS * 6
CLS_ROWS = 32
HIST = 32
HIST_OFF = HIST - (CONV_WIDTH - 1)
D_ROWS = D_MODEL // LANES
ROW_PITCH = D_ROWS + 2

TP = 512
TM = 256
TK = 512
CONV_ROWS = 64

VMEM_LIMIT = 56 * 1024 * 1024

_PAIRS = ((0, 1), (0, 2), (0, 3), (1, 2), (1, 3), (2, 3))


def _rms(x, g):
    ms = jnp.mean(x * x, axis=-1, keepdims=True)
    return x * lax.rsqrt(ms + EPS) * g


def _ln(x, g, b):
    mu = jnp.mean(x, axis=-1, keepdims=True)
    xc = x - mu
    var = jnp.mean(xc * xc, axis=-1, keepdims=True)
    return xc * lax.rsqrt(var + EPS) * g + b


def _max4(v):
    return jnp.maximum(jnp.maximum(v[0], v[1]), jnp.maximum(v[2], v[3]))


def _first4(v, m):
    return jnp.where(v[0] == m, 0, jnp.where(v[1] == m, 1, jnp.where(v[2] == m, 2, 3))).astype(jnp.int32)


def _route(lt):
    gl = [lt[i:i + 1, :] for i in range(N_GROUPS)]
    m = _max4(gl)
    g = _first4(gl, m)
    den = jnp.exp(gl[0] - m) + jnp.exp(gl[1] - m) + jnp.exp(gl[2] - m) + jnp.exp(gl[3] - m)
    g_w = 1.0 / den
    a = []
    for j in range(EPG):
        rows = [lt[N_GROUPS + EPG * q + j:N_GROUPS + EPG * q + j + 1, :] for q in range(N_GROUPS)]
        a.append(jnp.where(g == 0, rows[0], jnp.where(g == 1, rows[1], jnp.where(g == 2, rows[2], rows[3]))))
    v1 = _max4(a)
    i1 = _first4(a, v1)
    a2 = [jnp.where(i1 == j, -jnp.inf, a[j]) for j in range(EPG)]
    v2 = _max4(a2)
    i2 = _first4(a2, v2)
    e = jnp.exp(v2 - v1)
    s = 1.0 + e
    w1 = g_w / s
    w2 = g_w * e / s
    first_lo = i1 < i2
    lo = jnp.minimum(i1, i2)
    hi = jnp.maximum(i1, i2)
    w_lo = jnp.where(first_lo, w1, w2)
    w_hi = jnp.where(first_lo, w2, w1)
    pair = jnp.where(lo == 0, hi - 1, jnp.where(lo == 1, hi + 1, 5))
    return g * 6 + pair, w_lo, w_hi


def _load_token_rows(ref, n_tok, pitch):
    return jnp.concatenate([ref[pl.ds(s, n_tok, stride=pitch), :] for s in range(D_ROWS)], axis=1)


def _store_token_rows(ref, x, n_tok, pitch):
    for s in range(D_ROWS):
        ref[pl.ds(s, n_tok, stride=pitch), :] = x[:, s * LANES:(s + 1) * LANES]


def _router_tail(h, g_ffn_ref, w_rt_ref, b_r_ref, tri_ref, carry_ref, h1a_ref, meta_ref, cnt_ref, t):
    c = _rms(h, g_ffn_ref[...]).astype(BF16)
    lt = lax.dot_general(w_rt_ref[...], c, (((1,), (1,)), ((), ())), preferred_element_type=F32)
    lt = lt + b_r_ref[...]
    cls, w_lo, w_hi = _route(lt)
    rows = lax.broadcasted_iota(jnp.int32, (CLS_ROWS, t), 0)
    ohf = jnp.where(rows == cls, 1.0, 0.0).astype(F32)
    pre = jnp.dot(ohf.astype(BF16), tri_ref[...], preferred_element_type=F32)
    carry = carry_ref[:, 0:1]
    rank = jnp.sum(ohf * (pre - 1.0 + carry), axis=0, keepdims=True)
    new_cnt = jnp.broadcast_to(carry + jnp.sum(ohf, axis=1, keepdims=True), (CLS_ROWS, LANES))
    carry_ref[...] = new_cnt
    cnt_ref[...] = new_cnt
    meta_ref[0] = jnp.concatenate([cls.astype(F32), rank, jnp.zeros((6, t), F32)], axis=0)
    _store_token_rows(h1a_ref, h, t, ROW_PITCH)
    h1a_ref[pl.ds(D_ROWS, t, stride=ROW_PITCH), :] = jnp.transpose(jnp.broadcast_to(w_lo, (LANES, t)))
    h1a_ref[pl.ds(D_ROWS + 1, t, stride=ROW_PITCH), :] = jnp.transpose(jnp.broadcast_to(w_hi, (LANES, t)))


def _mixer_front(x, g_mix_ref, w_in_ref):
    a = _rms(x, g_mix_ref[...]).astype(BF16)
    z = jnp.dot(a, w_in_ref[...], preferred_element_type=F32)
    glu = z[:, 0:C_CONV] * jax.nn.sigmoid(z[:, C_CONV:2 * C_CONV])
    u = jax.nn.gelu(z[:, 2 * C_CONV:2 * C_CONV + C_GMLP])
    gv = jax.nn.gelu(z[:, 2 * C_CONV + C_GMLP:])
    return glu, u, gv


def _mixer_body(xp_ref, xs_ref, hist_ref, g_mix_ref, w_in_ref, cw_ref, cb_ref, clg_ref, clb_ref, glg_ref, glb_ref,
                ws_ref, bs_ref, wsl_ref, bsl_ref, w_out_ref, g_ffn_ref, w_rt_ref, b_r_ref, tri_ref,
                h1a_ref, meta_ref, cnt_ref, cstp_ref, csts_ref, vs_ref,
                glu_scr, xs_scr, cat_scr, carry_ref, *, n_prompt_tiles, nj, dec_seq, sblk):
    i = pl.program_id(0)

    @pl.when(i == 0)
    def _():
        carry_ref[...] = jnp.zeros_like(carry_ref)

    def tail(x):
        h = x + jnp.dot(cat_scr[...], w_out_ref[...], preferred_element_type=F32)
        _router_tail(h, g_ffn_ref, w_rt_ref, b_r_ref, tri_ref, carry_ref, h1a_ref, meta_ref, cnt_ref, TP)

    @pl.when(i < n_prompt_tiles)
    def _prompt():
        j = lax.rem(i, nj)
        x = xp_ref[0]
        glu, u, gv = _mixer_front(x, g_mix_ref, w_in_ref)

        @pl.when(j == 0)
        def _():
            glu_scr[0:HIST, :] = jnp.zeros((HIST, C_CONV), F32)

        @pl.when(j > 0)
        def _():
            glu_scr[0:HIST, :] = glu_scr[TP:TP + HIST, :]

        glu_scr[HIST:HIST + TP, :] = glu
        cstp_ref[0] = glu_scr[TP:TP + HIST, :]

        cb = cb_ref[...]
        for r0 in range(0, TP, CONV_ROWS):
            acc = jnp.zeros((CONV_ROWS, C_CONV), F32)
            for k in range(CONV_WIDTH):
                s0 = r0 + HIST_OFF + k
                acc = acc + cw_ref[k:k + 1, :] * glu_scr[s0:s0 + CONV_ROWS, :]
            ya = jax.nn.silu(_ln(acc + cb, clg_ref[...], clb_ref[...]))
            cat_scr[r0:r0 + CONV_ROWS, 0:C_CONV] = ya.astype(BF16)

        v = _ln(gv, glg_ref[...], glb_ref[...])
        vb = v.astype(BF16)
        lane = lax.broadcasted_iota(jnp.int32, (CHUNK, LANES), 1)
        is_lo = lane < HEAD_DIM
        zero = jnp.zeros((CHUNK, LANES), BF16)
        for c in range(TP // CHUNK):
            for p in range(N_HEADS // 2):
                blk = vb[c * CHUNK:(c + 1) * CHUNK, p * LANES:(p + 1) * LANES]
                rhs = jnp.concatenate([jnp.where(is_lo, blk, zero), jnp.where(is_lo, zero, blk)], axis=0)
                mixed = jnp.dot(ws_ref[p], rhs, preferred_element_type=F32) + bs_ref[:, p * LANES:(p + 1) * LANES]
                yb = u[c * CHUNK:(c + 1) * CHUNK, p * LANES:(p + 1) * LANES] * mixed
                cat_scr[c * CHUNK:(c + 1) * CHUNK, C_CONV + p * LANES:C_CONV + (p + 1) * LANES] = yb.astype(BF16)
        tail(x)

    @pl.when(i >= n_prompt_tiles)
    def _sample():
        x = xs_ref[...].reshape(TP, D_MODEL)
        glu, u, gv = _mixer_front(x, g_mix_ref, w_in_ref)

        nh = CONV_WIDTH - 1
        xs_scr[0:nh] = hist_ref[...]
        xs_scr[nh:nh + dec_seq] = glu.reshape(dec_seq, sblk, C_CONV)
        csts_ref[...] = xs_scr[dec_seq:dec_seq + nh]

        cb = cb_ref[...]
        for t in range(dec_seq):
            acc = jnp.zeros((sblk, C_CONV), F32)
            for k in range(CONV_WIDTH):
                acc = acc + cw_ref[k:k + 1, :] * xs_scr[t + k]
            ya = jax.nn.silu(_ln(acc + cb, clg_ref[...], clb_ref[...]))
            cat_scr[t * sblk:(t + 1) * sblk, 0:C_CONV] = ya.astype(BF16)

        v = _ln(gv, glg_ref[...], glb_ref[...])
        vs_ref[...] = v.reshape(dec_seq, sblk, C_GMLP)
        for t in range(dec_seq):
            mixed = jnp.broadcast_to(bsl_ref[t:t + 1, :], (sblk, C_GMLP))
            for tp in range(t + 1):
                r = t * dec_seq + tp
                mixed = mixed + wsl_ref[r:r + 1, :] * v[tp * sblk:(tp + 1) * sblk, :]
            yb = u[t * sblk:(t + 1) * sblk, :] * mixed
            cat_scr[t * sblk:(t + 1) * sblk, C_CONV:] = yb.astype(BF16)
        tail(x)


def _gather_copy(idx_ref, h1a_hbm, buf, sem, slot, r):
    tok = 0 if idx_ref is None else idx_ref[0, 0, r]
    return pltpu.make_async_copy(h1a_hbm.at[pl.ds(tok * ROW_PITCH, ROW_PITCH), :],
                                 buf.at[slot, pl.ds(r * ROW_PITCH, ROW_PITCH), :], sem.at[slot])


def _scatter_copy(idx_ref, obuf, h2_hbm, sem, slot, r):
    tok = 0 if idx_ref is None else idx_ref[0, 0, r]
    return pltpu.make_async_copy(obuf.at[slot, pl.ds(r * D_ROWS, D_ROWS), :],
                                 h2_hbm.at[pl.ds(tok * D_ROWS, D_ROWS), :], sem.at[slot])


def _moe_body(elo_ref, ehi_ref, nval_ref,
              src0_ref, srcn_ref, dstp_ref,
              h1a_hbm, g_ffn_ref, wgu_lo_ref, wgu_hi_ref, wd_lo_ref, wd_hi_ref,
              h2_hbm, buf, obuf, c_scr, hl_scr, hh_scr, acc_scr, gsem, ssem):
    del elo_ref, ehi_ref
    i = pl.program_id(0)
    slot = lax.rem(i, 2)
    prev_valid = nval_ref[jnp.maximum(i - 1, 0)] > 0
    prev2_valid = nval_ref[jnp.maximum(i - 2, 0)] > 0

    def wait_gather():
        for _ in range(TM):
            _gather_copy(None, h1a_hbm, buf, gsem, slot, 0).wait()

    def issue_scatter_prev(r0, r1):
        for r in range(r0, r1):
            _scatter_copy(dstp_ref, obuf, h2_hbm, ssem, 1 - slot, r).start(priority=r % 2)

    @pl.when(i == 0)
    def _():
        def body(r, carry):
            _gather_copy(src0_ref, h1a_hbm, buf, gsem, 0, r).start()
            return carry
        lax.fori_loop(0, TM, body, 0, unroll=8)
        obuf[1] = jnp.zeros((TM * D_ROWS, LANES), F32)

    @pl.when(jnp.logical_or(i == 1, jnp.logical_and(i >= 2, prev2_valid)))
    def _():
        for _ in range(TM):
            _scatter_copy(None, obuf, h2_hbm, ssem, slot, 0).wait()

    valid = nval_ref[i] > 0
    tile = buf.at[slot]
    half = TM // 2

    def issue_gather_next(r0, r1):
        for r in range(r0, r1):
            _gather_copy(srcn_ref, h1a_hbm, buf, gsem, 1 - slot, r).start(priority=r % 2)

    def expert_hidden(w_ref):
        gu = jnp.dot(c_scr[...], w_ref[0], preferred_element_type=F32)
        return (jax.nn.silu(gu[:, 0:D_EXPERT]) * gu[:, D_EXPERT:]).astype(BF16)

    def lane_tiled(row):
        return jnp.concatenate([tile[pl.ds(row, TM, stride=ROW_PITCH), :]] * D_ROWS, axis=1)

    @pl.when(valid)
    def _():
        wait_gather()
        c_scr[...] = _rms(_load_token_rows(tile, TM, ROW_PITCH), g_ffn_ref[...]).astype(BF16)
        hl_scr[...] = expert_hidden(wgu_lo_ref)
        issue_gather_next(0, half)

    @pl.when(valid)
    def _():
        hh_scr[...] = expert_hidden(wgu_hi_ref)
        issue_gather_next(half, TM)

    @pl.when(valid)
    def _():
        acc_scr[...] = lane_tiled(D_ROWS) * jnp.dot(hl_scr[...], wd_lo_ref[0], preferred_element_type=F32)
        issue_scatter_prev(0, half)

    @pl.when(valid)
    def _():
        y = acc_scr[...] + lane_tiled(D_ROWS + 1) * jnp.dot(hh_scr[...], wd_hi_ref[0], preferred_element_type=F32)
        _store_token_rows(obuf.at[slot], _load_token_rows(tile, TM, ROW_PITCH) + y, TM, D_ROWS)
        issue_scatter_prev(half, TM)

    @pl.when(jnp.logical_and(nval_ref[i] == 0, jnp.logical_and(i >= 1, prev_valid)))
    def _():
        wait_gather()
        issue_scatter_prev(0, TM)


def _ple_body(h2_ref, p_ref, g_ple_ref, w_gate_ref, w_proj_ref, g_fin_ref, y_ref):
    h = _load_token_rows(h2_ref, TK, D_ROWS)
    c = _rms(h, g_ple_ref[...]).astype(BF16)
    gate = jax.nn.sigmoid(jnp.dot(c, w_gate_ref[...], preferred_element_type=F32))
    proj = jnp.dot(p_ref[...].astype(BF16), w_proj_ref[...], preferred_element_type=F32)
    h = h + proj * gate
    y_ref[...] = _rms(h, g_fin_ref[...])


def _full(shape, single=False):
    nd = len(shape)
    if single:
        return pl.BlockSpec(shape, lambda *_: (0,) * nd, pipeline_mode=pl.Buffered(1))
    return pl.BlockSpec(shape, lambda *_: (0,) * nd)


def _mixer(x_prompt, x_tm, hist_tm, weights, tri):
    batch, seq, _ = x_prompt.shape
    dec_seq, dec_batch, _ = x_tm.shape
    nj = seq // TP
    npt = batch * nj
    sblk = TP // dec_seq
    nst = dec_batch // sblk
    n_tiles = npt + nst
    nh = CONV_WIDTH - 1
    w_specs = [_full(w.shape, single=True) for w in weights]
    body = functools.partial(_mixer_body, n_prompt_tiles=npt, nj=nj, dec_seq=dec_seq, sblk=sblk)

    def p_idx(i):
        return jnp.minimum(i, npt - 1)

    def s_idx(i):
        return jnp.maximum(i - npt, 0)

    return pl.pallas_call(
        body,
        grid=(n_tiles,),
        in_specs=[pl.BlockSpec((1, TP, D_MODEL), lambda i: (p_idx(i) // nj, p_idx(i) % nj, 0)),
                  pl.BlockSpec((dec_seq, sblk, D_MODEL), lambda i: (0, s_idx(i), 0), pipeline_mode=pl.Buffered(1)),
                  pl.BlockSpec((nh, sblk, C_CONV), lambda i: (0, s_idx(i), 0), pipeline_mode=pl.Buffered(1))]
        + w_specs + [_full(tri.shape, single=True)],
        out_specs=[
            pl.BlockSpec((TP * ROW_PITCH, LANES), lambda i: (i, 0)),
            pl.BlockSpec((1, 8, TP), lambda i: (i, 0, 0)),
            pl.BlockSpec((CLS_ROWS, LANES), lambda i: (0, 0)),
            pl.BlockSpec((1, HIST, C_CONV), lambda i: (p_idx(i) // nj, 0, 0)),
            pl.BlockSpec((nh, sblk, C_CONV), lambda i: (0, s_idx(i), 0)),
            pl.BlockSpec((dec_seq, sblk, C_GMLP), lambda i: (0, s_idx(i), 0)),
        ],
        out_shape=[
            jax.ShapeDtypeStruct((n_tiles * TP * ROW_PITCH, LANES), F32),
            jax.ShapeDtypeStruct((n_tiles, 8, TP), F32),
            jax.ShapeDtypeStruct((CLS_ROWS, LANES), F32),
            jax.ShapeDtypeStruct((batch, HIST, C_CONV), F32),
            jax.ShapeDtypeStruct((nh, dec_batch, C_CONV), F32),
            jax.ShapeDtypeStruct((dec_seq, dec_batch, C_GMLP), F32),
        ],
        scratch_shapes=[
            pltpu.VMEM((HIST + TP, C_CONV), F32),
            pltpu.VMEM((nh + dec_seq, sblk, C_CONV), F32),
            pltpu.VMEM((TP, D_MODEL), BF16),
            pltpu.VMEM((CLS_ROWS, LANES), F32),
        ],
        compiler_params=pltpu.CompilerParams(
            dimension_semantics=("arbitrary",), vmem_limit_bytes=VMEM_LIMIT),
        name="mixer",
    )(x_prompt, x_tm, hist_tm, *weights, tri)


def _moe(tile_elo, tile_ehi, tile_nval, src, dst_prev, h1a, g_ffn, w_gu, w_d, n_tok):
    n_tiles = src.shape[0]
    smem_blk = functools.partial(pl.BlockSpec, (1, 1, TM), memory_space=pltpu.SMEM)
    grid_spec = pltpu.PrefetchScalarGridSpec(
        num_scalar_prefetch=3,
        grid=(n_tiles,),
        in_specs=[
            smem_blk(lambda i, *_: (0, 0, 0)),
            smem_blk(lambda i, *_: (jnp.minimum(i + 1, n_tiles - 1), 0, 0)),
            smem_blk(lambda i, *_: (i, 0, 0)),
            pl.BlockSpec(memory_space=pl.ANY),
            pl.BlockSpec((1, D_MODEL), lambda i, *_: (0, 0)),
            pl.BlockSpec((1, D_MODEL, 2 * D_EXPERT), lambda i, elo, ehi, nv: (elo[i], 0, 0)),
            pl.BlockSpec((1, D_MODEL, 2 * D_EXPERT), lambda i, elo, ehi, nv: (ehi[i], 0, 0)),
            pl.BlockSpec((1, D_EXPERT, D_MODEL), lambda i, elo, ehi, nv: (elo[i], 0, 0)),
            pl.BlockSpec((1, D_EXPERT, D_MODEL), lambda i, elo, ehi, nv: (ehi[i], 0, 0)),
        ],
        out_specs=pl.BlockSpec(memory_space=pl.ANY),
        scratch_shapes=[pltpu.VMEM((2, TM * ROW_PITCH, LANES), F32),
                        pltpu.VMEM((2, TM * D_ROWS, LANES), F32),
                        pltpu.VMEM((TM, D_MODEL), BF16),
                        pltpu.VMEM((TM, D_EXPERT), BF16),
                        pltpu.VMEM((TM, D_EXPERT), BF16),
                        pltpu.VMEM((TM, D_MODEL), F32),
                        pltpu.SemaphoreType.DMA((2,)),
                        pltpu.SemaphoreType.DMA((2,))],
    )
    return pl.pallas_call(
        _moe_body,
        grid_spec=grid_spec,
        out_shape=jax.ShapeDtypeStruct(((n_tok + TM) * D_ROWS, LANES), F32),
        compiler_params=pltpu.CompilerParams(
            dimension_semantics=("arbitrary",), vmem_limit_bytes=VMEM_LIMIT),
        name="moe",
    )(tile_elo, tile_ehi, tile_nval, src, src, dst_prev, h1a, g_ffn, w_gu, w_gu, w_d, w_d)


def _ple(h2, tile0, p, g_ple, w_gate, w_proj, g_fin, name):
    n_tiles = p.shape[0] // TK
    return pl.pallas_call(
        _ple_body,
        grid=(n_tiles,),
        in_specs=[
            pl.BlockSpec((TK * D_ROWS, LANES), lambda i: (tile0 + i, 0)),
            pl.BlockSpec((TK, PLE_DIM), lambda i: (i, 0)),
            _full(g_ple.shape), _full(w_gate.shape), _full(w_proj.shape), _full(g_fin.shape),
        ],
        out_specs=pl.BlockSpec((TK, D_MODEL), lambda i: (i, 0)),
        out_shape=jax.ShapeDtypeStruct((n_tiles * TK, D_MODEL), F32),
        compiler_params=pltpu.CompilerParams(
            dimension_semantics=("arbitrary",), vmem_limit_bytes=VMEM_LIMIT),
        name=name,
    )(h2, p, g_ple, w_gate, w_proj, g_fin)


def kernel(x_prompt, x_sample, p_prompt, p_sample, state_conv, norm_mix_g, w_in, conv_w, conv_b, conv_ln_g, conv_ln_b, gmlp_ln_g, gmlp_ln_b, w_s, b_s, w_out, norm_ffn_g, w_router_group, b_router_group, w_router_expert, b_router_expert, w_exp_gate, w_exp_up, w_exp_down, norm_ple_g, w_ple_gate, w_ple_proj, norm_final_g):
    depth = w_in.shape[0]
    assert depth == 1, "single-layer pipeline"
    batch, seq, _ = x_prompt.shape
    dec_batch, dec_seq, _ = x_sample.shape
    assert seq % TP == 0 and TP % CHUNK == 0 and TP % dec_seq == 0 and dec_batch % (TP // dec_seq) == 0
    assert TP == TK
    sblk = TP // dec_seq
    n_prompt = batch * seq
    n_sample = dec_batch * dec_seq
    n_tok = n_prompt + n_sample

    row = lambda a: a.reshape(1, -1)
    w_in_b = w_in[0].astype(BF16)
    w_out_b = w_out[0].astype(BF16)
    cw = jnp.concatenate([conv_w[0], jnp.zeros((1, C_CONV), F32)], axis=0)
    tril = jnp.tril(jnp.ones((CHUNK, CHUNK), bool))
    ws_m = jnp.where(tril[None], w_s[0], 0.0)
    ws_cat = jnp.concatenate([ws_m[0::2], ws_m[1::2]], axis=2).astype(BF16)
    bs_lane = jnp.repeat(jnp.transpose(b_s[0]), HEAD_DIM, axis=1)
    w_rt = jnp.zeros((CLS_ROWS, D_MODEL), F32)
    w_rt = w_rt.at[0:N_GROUPS].set(jnp.transpose(w_router_group[0]))
    w_rt = w_rt.at[N_GROUPS:N_GROUPS + N_EXPERTS].set(jnp.transpose(w_router_expert[0])).astype(BF16)
    b_r = jnp.zeros((CLS_ROWS, 1), F32)
    b_r = b_r.at[0:N_GROUPS, 0].set(b_router_group[0]).at[N_GROUPS:N_GROUPS + N_EXPERTS, 0].set(b_router_expert[0])
    tri = jnp.triu(jnp.ones((TP, TP), F32)).astype(BF16)
    wsl = jnp.where(jnp.tril(jnp.ones((dec_seq, dec_seq), bool))[None], w_s[0][:, :dec_seq, :dec_seq], 0.0)
    wsl = jnp.repeat(jnp.transpose(wsl, (1, 2, 0)).reshape(dec_seq * dec_seq, N_HEADS), HEAD_DIM, axis=1)
    bsl = jnp.repeat(jnp.transpose(b_s[0][:, :dec_seq]), HEAD_DIM, axis=1)
    w_gu = jnp.concatenate([w_exp_gate[0], w_exp_up[0]], axis=2).astype(BF16)
    w_d = w_exp_down[0].astype(BF16)

    weights = (row(norm_mix_g[0]), w_in_b, cw, row(conv_b[0]), row(conv_ln_g[0]), row(conv_ln_b[0]),
               row(gmlp_ln_g[0]), row(gmlp_ln_b[0]), ws_cat, bs_lane, wsl, bsl,
               w_out_b, row(norm_ffn_g[0]), w_rt, b_r)

    x_tm = jnp.transpose(x_sample, (1, 0, 2))
    hist_tm = jnp.transpose(state_conv[0], (1, 0, 2))
    h1a, meta, cnt, cst_p, cst_s, v_s = _mixer(x_prompt, x_tm, hist_tm, weights, tri)

    cls = meta[:, 0, :].reshape(-1).astype(jnp.int32)
    rank = meta[:, 1, :].reshape(-1).astype(jnp.int32)
    cnt = cnt[:N_CLASSES, 0].astype(jnp.int32)
    padded = ((cnt + TM - 1) // TM) * TM
    ends = jnp.cumsum(padded)
    offs = ends - padded
    cls_ids = jnp.arange(N_CLASSES, dtype=jnp.int32)
    dest = jnp.sum(jnp.where(cls[:, None] == cls_ids[None, :], offs[None, :], 0), axis=1) + rank
    n_tiles = (n_tok + N_CLASSES * (TM - 1)) // TM + 2
    src = jnp.full((n_tiles * TM,), -1, jnp.int32).at[dest].set(
        jnp.arange(n_tok, dtype=jnp.int32), unique_indices=True, mode="promise_in_bounds")
    spare = n_tok + jnp.arange(TM, dtype=jnp.int32)
    dst = jnp.where(src >= 0, src, jnp.tile(spare, n_tiles))
    dst_prev = jnp.concatenate([spare, dst]).reshape(n_tiles + 1, 1, TM)
    src = jnp.maximum(src, 0)
    tile_start = jnp.arange(n_tiles, dtype=jnp.int32) * TM
    n_used = ends[-1] // TM
    tile_cls = jnp.sum((tile_start[:, None] >= ends[None, :]).astype(jnp.int32), axis=1)
    tile_cls_c = jnp.minimum(tile_cls, N_CLASSES - 1)
    tile_nval = jnp.clip(jnp.take(offs + cnt, tile_cls_c) - tile_start, 0, TM)
    tile_nval = jnp.where(tile_cls < N_CLASSES, tile_nval, 0).astype(jnp.int32)
    last_cls = jnp.take(tile_cls, n_used - 1)
    tile_cls = jnp.where(tile_start < ends[-1], tile_cls, last_cls)
    pair_lo = jnp.asarray(np.array([p[0] for p in _PAIRS], np.int32))
    pair_hi = jnp.asarray(np.array([p[1] for p in _PAIRS], np.int32))
    tile_elo = (tile_cls // 6) * EPG + jnp.take(pair_lo, tile_cls % 6)
    tile_ehi = (tile_cls // 6) * EPG + jnp.take(pair_hi, tile_cls % 6)

    h2 = _moe(tile_elo, tile_ehi, tile_nval, src.reshape(n_tiles, 1, TM), dst_prev, h1a,
              row(norm_ffn_g[0]), w_gu, w_d, n_tok)

    ple_w = (row(norm_ple_g[0]), w_ple_gate[0].astype(BF16), w_ple_proj[0].astype(BF16), row(norm_final_g))
    y_p = _ple(h2, 0, p_prompt[0].reshape(n_prompt, PLE_DIM), *ple_w, name="ple_prompt")
    p_s_tm = jnp.transpose(p_sample[0].reshape(dec_batch // sblk, sblk, dec_seq, PLE_DIM), (0, 2, 1, 3))
    y_s = _ple(h2, n_prompt // TK, p_s_tm.reshape(n_sample, PLE_DIM), *ple_w, name="ple_sample")

    y_prompt = y_p.reshape(batch, seq, D_MODEL)
    y_sample = jnp.transpose(y_s.reshape(dec_batch // sblk, dec_seq, sblk, D_MODEL), (0, 2, 1, 3))
    y_sample = y_sample.reshape(dec_batch, dec_seq, D_MODEL)
    state_conv_prompt = cst_p[:, HIST_OFF:, :][None]
    state_conv_sample = jnp.transpose(cst_s, (1, 0, 2))[None]
    state_gmlp_v_sample = jnp.transpose(v_s, (1, 0, 2))[None]
    return (y_prompt, y_sample, state_conv_prompt, state_conv_sample, state_gmlp_v_sample)
```

```python
import functools

import numpy as np
import jax
import jax.numpy as jnp
from jax import lax
from jax.experimental import pallas as pl
from jax.experimental.pallas import tpu as pltpu

F32 = jnp.float32
BF16 = jnp.bfloat16
I32 = jnp.int32

D_MODEL = 1024
C_CONV = 512
C_GMLP = 512
N_HEADS = 8
HEAD_DIM = 64
CONV_WIDTH = 31
CHUNK = 128
PLE_DIM = 256
N_GROUPS = 4
EPG = 4
N_EXPERTS = 16
D_EXPERT = 512
EPS = 1e-6
LANES = 128
SUBLANES = 8

N_CLASSES = N_GROUPS * 6
CLS_ROWS = 32
HIST = 32
HIST_OFF = HIST - (CONV_WIDTH - 1)
D_ROWS = D_MODEL // LANES
assert D_ROWS == SUBLANES
GROUP_ROWS = D_ROWS * SUBLANES

TP = 512
TM = 256
CONV_ROWS = 64
PIECES = tuple(TM >> b for b in range(TM.bit_length()))

VMEM_LIMIT = 58 * 1024 * 1024

_PAIRS = ((0, 1), (0, 2), (0, 3), (1, 2), (1, 3), (2, 3))


def _rms(x, g):
    ms = jnp.mean(x * x, axis=-1, keepdims=True)
    return x * lax.rsqrt(ms + EPS) * g


def _ln(x, g, b):
    mu = jnp.mean(x, axis=-1, keepdims=True)
    xc = x - mu
    var = jnp.mean(xc * xc, axis=-1, keepdims=True)
    return xc * lax.rsqrt(var + EPS) * g + b


def _max4(v):
    return jnp.maximum(jnp.maximum(v[0], v[1]), jnp.maximum(v[2], v[3]))


def _first4(v, m):
    return jnp.where(v[0] == m, 0, jnp.where(v[1] == m, 1, jnp.where(v[2] == m, 2, 3))).astype(I32)


def _route(lt):
    gl = [lt[i:i + 1, :] for i in range(N_GROUPS)]
    m = _max4(gl)
    g = _first4(gl, m)
    den = jnp.exp(gl[0] - m) + jnp.exp(gl[1] - m) + jnp.exp(gl[2] - m) + jnp.exp(gl[3] - m)
    g_w = 1.0 / den
    a = []
    for j in range(EPG):
        rows = [lt[N_GROUPS + EPG * q + j:N_GROUPS + EPG * q + j + 1, :] for q in range(N_GROUPS)]
        a.append(jnp.where(g == 0, rows[0], jnp.where(g == 1, rows[1], jnp.where(g == 2, rows[2], rows[3]))))
    v1 = _max4(a)
    i1 = _first4(a, v1)
    a2 = [jnp.where(i1 == j, -jnp.inf, a[j]) for j in range(EPG)]
    v2 = _max4(a2)
    i2 = _first4(a2, v2)
    e = jnp.exp(v2 - v1)
    s = 1.0 + e
    w1 = g_w / s
    w2 = g_w * e / s
    first_lo = i1 < i2
    lo = jnp.minimum(i1, i2)
    hi = jnp.maximum(i1, i2)
    w_lo = jnp.where(first_lo, w1, w2)
    w_hi = jnp.where(first_lo, w2, w1)
    pair = jnp.where(lo == 0, hi - 1, jnp.where(lo == 1, hi + 1, 5))
    return g * 6 + pair, w_lo, w_hi


def _load_token_blocks(ref, n_tok):
    return jnp.concatenate([ref[pl.ds(s, n_tok, stride=D_ROWS), :] for s in range(D_ROWS)], axis=1)


def _store_token_blocks(ref, x, n_tok):
    for s in range(D_ROWS):
        ref[pl.ds(s, n_tok, stride=D_ROWS), :] = x[:, s * LANES:(s + 1) * LANES]


def _router_tail(h, g_ffn_ref, w_rt_ref, b_r_ref, tri_ref, carry_ref, h1_ref, w1_ref, lp_ref, tab_ref,
                 stage_h, stage_w, lp_vmem, lp_smem):
    t = TP
    c = _rms(h, g_ffn_ref[...]).astype(BF16)
    lt = lax.dot_general(w_rt_ref[...], c, (((1,), (1,)), ((), ())), preferred_element_type=F32)
    lt = lt + b_r_ref[...]
    cls, w_lo, w_hi = _route(lt)
    rows = lax.broadcasted_iota(I32, (CLS_ROWS, t), 0)
    ohf = jnp.where(rows == cls, 1.0, 0.0).astype(F32)
    pre = jnp.dot(ohf.astype(BF16), tri_ref[...], preferred_element_type=F32)
    tot = jnp.sum(ohf, axis=1, keepdims=True)
    lstart = jnp.sum(jnp.where(cls < rows, 1.0, 0.0).astype(F32), axis=1, keepdims=True)
    lpos = jnp.sum(ohf * (pre - 1.0 + lstart), axis=0, keepdims=True).astype(I32)
    carry = carry_ref[:, 0:1]
    lane = lax.broadcasted_iota(I32, (CLS_ROWS, LANES), 1)
    tab_ref[0] = jnp.where(lane == 0, carry, jnp.where(lane == 1, tot, jnp.where(lane == 2, lstart, 0.0)))
    carry_ref[...] = jnp.broadcast_to(carry + tot, (CLS_ROWS, LANES))
    lp_ref[0] = jnp.broadcast_to(lpos, (SUBLANES, t))
    lp_vmem[...] = jnp.broadcast_to(lpos, (SUBLANES, t))

    ng = t // SUBLANES
    for s in range(D_ROWS):
        stage_h[:, s * SUBLANES:(s + 1) * SUBLANES, :] = h[:, s * LANES:(s + 1) * LANES].reshape(ng, SUBLANES, LANES)
    stage_w[:, 0:SUBLANES, :] = jnp.transpose(jnp.broadcast_to(w_lo, (LANES, t))).reshape(ng, SUBLANES, LANES)
    stage_w[:, SUBLANES:2 * SUBLANES, :] = jnp.transpose(jnp.broadcast_to(w_hi, (LANES, t))).reshape(
        ng, SUBLANES, LANES)
    pltpu.sync_copy(lp_vmem.at[pl.ds(0, 1)], lp_smem)

    def group(g, carry_):
        for k in range(SUBLANES):
            dst = pl.multiple_of(lp_smem[0, g * SUBLANES + k] * D_ROWS, D_ROWS)
            h1_ref[pl.ds(dst, D_ROWS), :] = stage_h[g, pl.ds(k, D_ROWS, stride=SUBLANES), :]
            w1_ref[pl.ds(dst, D_ROWS), :] = stage_w[g, pl.ds(k, D_ROWS, stride=SUBLANES), :]
        return carry_

    lax.fori_loop(0, ng, group, 0)


def _mixer_front(x, g_mix_ref, w_in_ref):
    a = _rms(x, g_mix_ref[...]).astype(BF16)
    z = jnp.dot(a, w_in_ref[...], preferred_element_type=F32)
    glu = z[:, 0:C_CONV] * jax.nn.sigmoid(z[:, C_CONV:2 * C_CONV])
    u = jax.nn.gelu(z[:, 2 * C_CONV:2 * C_CONV + C_GMLP])
    gv = jax.nn.gelu(z[:, 2 * C_CONV + C_GMLP:])
    return glu, u, gv


def _mixer_body(xp_ref, xs_ref, hist_ref, g_mix_ref, w_in_ref, cw_ref, cb_ref, clg_ref, clb_ref, glg_ref, glb_ref,
                ws_ref, bs_ref, wsl_ref, bsl_ref, w_out_ref, g_ffn_ref, w_rt_ref, b_r_ref, tri_ref,
                h1_ref, w1_ref, lp_ref, tab_ref, cstp_ref, csts_ref, vs_ref,
                glu_scr, xs_scr, cat_scr, carry_ref, stage_h, stage_w, lp_vmem, lp_smem,
                *, n_prompt_tiles, nj, dec_seq, sblk):
    i = pl.program_id(0)

    @pl.when(i == 0)
    def _():
        carry_ref[...] = jnp.zeros_like(carry_ref)
        stage_w[...] = jnp.zeros_like(stage_w)

    def tail(x):
        h = x + jnp.dot(cat_scr[...], w_out_ref[...], preferred_element_type=F32)
        _router_tail(h, g_ffn_ref, w_rt_ref, b_r_ref, tri_ref, carry_ref, h1_ref, w1_ref, lp_ref, tab_ref,
                     stage_h, stage_w, lp_vmem, lp_smem)

    @pl.when(i < n_prompt_tiles)
    def _prompt():
        j = lax.rem(i, nj)
        x = xp_ref[0]
        glu, u, gv = _mixer_front(x, g_mix_ref, w_in_ref)

        @pl.when(j == 0)
        def _():
            glu_scr[0:HIST, :] = jnp.zeros((HIST, C_CONV), F32)

        @pl.when(j > 0)
        def _():
            glu_scr[0:HIST, :] = glu_scr[TP:TP + HIST, :]

        glu_scr[HIST:HIST + TP, :] = glu
        cstp_ref[0] = glu_scr[TP:TP + HIST, :]

        cb = cb_ref[...]
        for r0 in range(0, TP, CONV_ROWS):
            acc = jnp.zeros((CONV_ROWS, C_CONV), F32)
            for k in range(CONV_WIDTH):
                s0 = r0 + HIST_OFF + k
                acc = acc + cw_ref[k:k + 1, :] * glu_scr[s0:s0 + CONV_ROWS, :]
            ya = jax.nn.silu(_ln(acc + cb, clg_ref[...], clb_ref[...]))
            cat_scr[r0:r0 + CONV_ROWS, 0:C_CONV] = ya.astype(BF16)

        v = _ln(gv, glg_ref[...], glb_ref[...])
        vb = v.astype(BF16)
        lane = lax.broadcasted_iota(I32, (CHUNK, LANES), 1)
        is_lo = lane < HEAD_DIM
        zero = jnp.zeros((CHUNK, LANES), BF16)
        for c in range(TP // CHUNK):
            for p in range(N_HEADS // 2):
                blk = vb[c * CHUNK:(c + 1) * CHUNK, p * LANES:(p + 1) * LANES]
                rhs = jnp.concatenate([jnp.where(is_lo, blk, zero), jnp.where(is_lo, zero, blk)], axis=0)
                mixed = jnp.dot(ws_ref[p], rhs, preferred_element_type=F32) + bs_ref[:, p * LANES:(p + 1) * LANES]
                yb = u[c * CHUNK:(c + 1) * CHUNK, p * LANES:(p + 1) * LANES] * mixed
                cat_scr[c * CHUNK:(c + 1) * CHUNK, C_CONV + p * LANES:C_CONV + (p + 1) * LANES] = yb.astype(BF16)
        tail(x)

    @pl.when(i >= n_prompt_tiles)
    def _sample():
        x = xs_ref[...].reshape(TP, D_MODEL)
        glu, u, gv = _mixer_front(x, g_mix_ref, w_in_ref)

        nh = CONV_WIDTH - 1
        xs_scr[0:nh] = hist_ref[...]
        xs_scr[nh:nh + dec_seq] = glu.reshape(dec_seq, sblk, C_CONV)
        csts_ref[...] = xs_scr[dec_seq:dec_seq + nh]

        cb = cb_ref[...]
        for t in range(dec_seq):
            acc = jnp.zeros((sblk, C_CONV), F32)
            for k in range(CONV_WIDTH):
                acc = acc + cw_ref[k:k + 1, :] * xs_scr[t + k]
            ya = jax.nn.silu(_ln(acc + cb, clg_ref[...], clb_ref[...]))
            cat_scr[t * sblk:(t + 1) * sblk, 0:C_CONV] = ya.astype(BF16)

        v = _ln(gv, glg_ref[...], glb_ref[...])
        vs_ref[...] = v.reshape(dec_seq, sblk, C_GMLP)
        for t in range(dec_seq):
            mixed = jnp.broadcast_to(bsl_ref[t:t + 1, :], (sblk, C_GMLP))
            for tp in range(t + 1):
                r = t * dec_seq + tp
                mixed = mixed + wsl_ref[r:r + 1, :] * v[tp * sblk:(tp + 1) * sblk, :]
            yb = u[t * sblk:(t + 1) * sblk, :] * mixed
            cat_scr[t * sblk:(t + 1) * sblk, C_CONV:] = yb.astype(BF16)
        tail(x)


def _for_pieces(ln, fn):
    for b, size in enumerate(PIECES):
        shift = size.bit_length()
        off = lax.shift_left(lax.shift_right_logical(ln, shift), shift)

        @pl.when((ln & size) != 0)
        def _(off=off, size=size):
            fn(off, size)


def _for_runs(k, tcls_ref, r0_ref, nval_ref, rank0_ref, cnt_ref, lsrc_ref, n_src_tiles, fn):
    c = tcls_ref[k]
    ra = r0_ref[k]
    rb = ra + nval_ref[k]

    def body(j, carry):
        idx = j * N_CLASSES + c
        s = rank0_ref[idx]
        lo = jnp.maximum(s, ra)
        hi = jnp.minimum(s + cnt_ref[idx], rb)
        ln = hi - lo

        @pl.when(ln > 0)
        def _():
            src = lsrc_ref[idx] + (lo - s)
            dst = lo - ra
            _for_pieces(ln, lambda off, size: fn(src + off, dst + off, size))

        return carry

    lax.fori_loop(0, n_src_tiles, body, 0)


def _blocks(ref, first, n):
    start = first * D_ROWS if isinstance(first, int) else pl.multiple_of(first * D_ROWS, D_ROWS)
    return ref.at[pl.ds(start, n * D_ROWS), :]


def _moe_body(elo_ref, ehi_ref, nval_ref, tcls_ref, r0_ref, rank0_ref, cnt_ref, lsrc_ref,
              h1_hbm, w1_hbm, g_ffn_ref, wgu_lo_ref, wgu_hi_ref, wd_lo_ref, wd_hi_ref,
              h2_hbm, hbuf, wbuf, obuf, gsem, ssem, *, n_src_tiles):
    del elo_ref, ehi_ref
    k = pl.program_id(0)
    n = pl.num_programs(0)
    slot = lax.rem(k, 2)
    tables = (tcls_ref, r0_ref, nval_ref, rank0_ref, cnt_ref, lsrc_ref, n_src_tiles)

    def gather_tile(tile, to_slot):
        def copy(src, dst, size):
            pltpu.make_async_copy(_blocks(h1_hbm, src, size), _blocks(hbuf.at[to_slot], dst, size),
                                  gsem.at[to_slot]).start()
            pltpu.make_async_copy(_blocks(w1_hbm, src, size), _blocks(wbuf.at[to_slot], dst, size),
                                  gsem.at[to_slot]).start(priority=1)
        _for_runs(tile, *tables, copy)

    def wait_rows(n_rows, make_copy):
        _for_pieces(n_rows, lambda off, size: make_copy(size).wait())

    @pl.when(k == 0)
    def _():
        hbuf[...] = jnp.zeros_like(hbuf)
        wbuf[...] = jnp.zeros_like(wbuf)
        gather_tile(0, 0)

    @pl.when(jnp.logical_and(k >= 2, nval_ref[jnp.maximum(k - 2, 0)] > 0))
    def _():
        wait_rows(nval_ref[jnp.maximum(k - 2, 0)],
                  lambda size: pltpu.make_async_copy(_blocks(obuf.at[slot], 0, size), _blocks(h2_hbm, 0, size),
                                                     ssem.at[slot]))

    nxt = jnp.minimum(k + 1, n - 1)

    @pl.when(jnp.logical_and(k + 1 < n, nval_ref[nxt] > 0))
    def _():
        gather_tile(nxt, 1 - slot)

    @pl.when(nval_ref[k] > 0)
    def _():
        for src_hbm, dst_buf in ((h1_hbm, hbuf), (w1_hbm, wbuf)):
            wait_rows(nval_ref[k],
                      lambda size: pltpu.make_async_copy(_blocks(src_hbm, 0, size), _blocks(dst_buf.at[slot], 0, size),
                                                         gsem.at[slot]))
        h = _load_token_blocks(hbuf.at[slot], TM)
        wl = wbuf[slot, pl.ds(0, TM, stride=D_ROWS), :]
        wh = wbuf[slot, pl.ds(1, TM, stride=D_ROWS), :]
        c = _rms(h, g_ffn_ref[...]).astype(BF16)
        gu = jnp.dot(c, wgu_lo_ref[0], preferred_element_type=F32)
        hl = (jax.nn.silu(gu[:, 0:D_EXPERT]) * gu[:, D_EXPERT:]).astype(BF16)
        gu = jnp.dot(c, wgu_hi_ref[0], preferred_element_type=F32)
        hh = (jax.nn.silu(gu[:, 0:D_EXPERT]) * gu[:, D_EXPERT:]).astype(BF16)
        yl = jnp.dot(hl, wd_lo_ref[0], preferred_element_type=F32)
        yh = jnp.dot(hh, wd_hi_ref[0], preferred_element_type=F32)
        wl8 = jnp.concatenate([wl] * D_ROWS, axis=1)
        wh8 = jnp.concatenate([wh] * D_ROWS, axis=1)
        _store_token_blocks(obuf.at[slot], h + (wl8 * yl + wh8 * yh), TM)

        def copy_back(src, dst, size):
            pltpu.make_async_copy(_blocks(obuf.at[slot], dst, size), _blocks(h2_hbm, src, size),
                                  ssem.at[slot]).start()
        _for_runs(k, *tables, copy_back)


def _ple_body(lp_ref, h2_ref, p_ref, g_ple_ref, w_gate_ref, w_proj_ref, g_fin_ref, y_ref, stage_h):
    ng = TP // SUBLANES

    def group(g, carry):
        for k in range(SUBLANES):
            src = pl.multiple_of(lp_ref[0, 0, g * SUBLANES + k] * D_ROWS, D_ROWS)
            stage_h[g, pl.ds(k, D_ROWS, stride=SUBLANES), :] = h2_ref[pl.ds(src, D_ROWS), :]
        return carry

    lax.fori_loop(0, ng, group, 0)
    h = jnp.concatenate([stage_h[:, s * SUBLANES:(s + 1) * SUBLANES, :].reshape(TP, LANES) for s in range(D_ROWS)],
                        axis=1)
    c = _rms(h, g_ple_ref[...]).astype(BF16)
    gate = jax.nn.sigmoid(jnp.dot(c, w_gate_ref[...], preferred_element_type=F32))
    proj = jnp.dot(p_ref[...].astype(BF16), w_proj_ref[...], preferred_element_type=F32)
    h = h + proj * gate
    y_ref[...] = _rms(h, g_fin_ref[...])


def _full(shape, single=False):
    nd = len(shape)
    if single:
        return pl.BlockSpec(shape, lambda *_: (0,) * nd, pipeline_mode=pl.Buffered(1))
    return pl.BlockSpec(shape, lambda *_: (0,) * nd)


def _mixer(x_prompt, x_tm, hist_tm, weights, tri):
    batch, seq, _ = x_prompt.shape
    dec_seq, dec_batch, _ = x_tm.shape
    nj = seq // TP
    npt = batch * nj
    sblk = TP // dec_seq
    nst = dec_batch // sblk
    n_tiles = npt + nst
    nh = CONV_WIDTH - 1
    w_specs = [_full(w.shape, single=True) for w in weights]
    body = functools.partial(_mixer_body, n_prompt_tiles=npt, nj=nj, dec_seq=dec_seq, sblk=sblk)

    def p_idx(i):
        return jnp.minimum(i, npt - 1)

    def s_idx(i):
        return jnp.maximum(i - npt, 0)

    return pl.pallas_call(
        body,
        grid=(n_tiles,),
        in_specs=[pl.BlockSpec((1, TP, D_MODEL), lambda i: (p_idx(i) // nj, p_idx(i) % nj, 0)),
                  pl.BlockSpec((dec_seq, sblk, D_MODEL), lambda i: (0, s_idx(i), 0), pipeline_mode=pl.Buffered(1)),
                  pl.BlockSpec((nh, sblk, C_CONV), lambda i: (0, s_idx(i), 0), pipeline_mode=pl.Buffered(1))]
        + w_specs + [_full(tri.shape, single=True)],
        out_specs=[
            pl.BlockSpec((TP * D_ROWS, LANES), lambda i: (i, 0)),
            pl.BlockSpec((TP * D_ROWS, LANES), lambda i: (i, 0)),
            pl.BlockSpec((1, SUBLANES, TP), lambda i: (i, 0, 0)),
            pl.BlockSpec((1, CLS_ROWS, LANES), lambda i: (i, 0, 0)),
            pl.BlockSpec((1, HIST, C_CONV), lambda i: (p_idx(i) // nj, 0, 0)),
            pl.BlockSpec((nh, sblk, C_CONV), lambda i: (0, s_idx(i), 0)),
            pl.BlockSpec((dec_seq, sblk, C_GMLP), lambda i: (0, s_idx(i), 0)),
        ],
        out_shape=[
            jax.ShapeDtypeStruct((n_tiles * TP * D_ROWS, LANES), F32),
            jax.ShapeDtypeStruct((n_tiles * TP * D_ROWS, LANES), F32),
            jax.ShapeDtypeStruct((n_tiles, SUBLANES, TP), I32),
            jax.ShapeDtypeStruct((n_tiles, CLS_ROWS, LANES), F32),
            jax.ShapeDtypeStruct((batch, HIST, C_CONV), F32),
            jax.ShapeDtypeStruct((nh, dec_batch, C_CONV), F32),
            jax.ShapeDtypeStruct((dec_seq, dec_batch, C_GMLP), F32),
        ],
        scratch_shapes=[
            pltpu.VMEM((HIST + TP, C_CONV), F32),
            pltpu.VMEM((nh + dec_seq, sblk, C_CONV), F32),
            pltpu.VMEM((TP, D_MODEL), BF16),
            pltpu.VMEM((CLS_ROWS, LANES), F32),
            pltpu.VMEM((TP // SUBLANES, GROUP_ROWS, LANES), F32),
            pltpu.VMEM((TP // SUBLANES, GROUP_ROWS, LANES), F32),
            pltpu.VMEM((SUBLANES, TP), I32),
            pltpu.SMEM((1, TP), I32),
        ],
        compiler_params=pltpu.CompilerParams(
            dimension_semantics=("arbitrary",), vmem_limit_bytes=VMEM_LIMIT),
        name="mixer",
    )(x_prompt, x_tm, hist_tm, *weights, tri)


def _moe(tile_tabs, run_tabs, h1, w1, g_ffn, w_gu, w_d, n_src_tiles):
    n_tiles = tile_tabs[0].shape[0]
    n_prefetch = len(tile_tabs) + len(run_tabs)
    grid_spec = pltpu.PrefetchScalarGridSpec(
        num_scalar_prefetch=n_prefetch,
        grid=(n_tiles,),
        in_specs=[
            pl.BlockSpec(memory_space=pl.ANY),
            pl.BlockSpec(memory_space=pl.ANY),
            pl.BlockSpec((1, D_MODEL), lambda i, *_: (0, 0)),
            pl.BlockSpec((1, D_MODEL, 2 * D_EXPERT), lambda i, elo, ehi, *_: (elo[i], 0, 0)),
            pl.BlockSpec((1, D_MODEL, 2 * D_EXPERT), lambda i, elo, ehi, *_: (ehi[i], 0, 0)),
            pl.BlockSpec((1, D_EXPERT, D_MODEL), lambda i, elo, ehi, *_: (elo[i], 0, 0)),
            pl.BlockSpec((1, D_EXPERT, D_MODEL), lambda i, elo, ehi, *_: (ehi[i], 0, 0)),
        ],
        out_specs=pl.BlockSpec(memory_space=pl.ANY),
        scratch_shapes=[pltpu.VMEM((2, TM * D_ROWS, LANES), F32),
                        pltpu.VMEM((2, TM * D_ROWS, LANES), F32),
                        pltpu.VMEM((2, TM * D_ROWS, LANES), F32),
                        pltpu.SemaphoreType.DMA((2,)),
                        pltpu.SemaphoreType.DMA((2,))],
    )
    return pl.pallas_call(
        functools.partial(_moe_body, n_src_tiles=n_src_tiles),
        grid_spec=grid_spec,
        out_shape=jax.ShapeDtypeStruct(h1.shape, F32),
        compiler_params=pltpu.CompilerParams(
            dimension_semantics=("arbitrary",), vmem_limit_bytes=VMEM_LIMIT),
        name="moe",
    )(*tile_tabs, *run_tabs, h1, w1, g_ffn, w_gu, w_gu, w_d, w_d)


def _ple(lp, h2, tile0, p, g_ple, w_gate, w_proj, g_fin, name):
    n_tiles = p.shape[0] // TP
    return pl.pallas_call(
        _ple_body,
        grid=(n_tiles,),
        in_specs=[
            pl.BlockSpec((1, 1, TP), lambda i: (tile0 + i, 0, 0), memory_space=pltpu.SMEM),
            pl.BlockSpec((TP * D_ROWS, LANES), lambda i: (tile0 + i, 0)),
            pl.BlockSpec((TP, PLE_DIM), lambda i: (i, 0)),
            _full(g_ple.shape), _full(w_gate.shape), _full(w_proj.shape), _full(g_fin.shape),
        ],
        out_specs=pl.BlockSpec((TP, D_MODEL), lambda i: (i, 0)),
        out_shape=jax.ShapeDtypeStruct((n_tiles * TP, D_MODEL), F32),
        scratch_shapes=[pltpu.VMEM((TP // SUBLANES, GROUP_ROWS, LANES), F32)],
        compiler_params=pltpu.CompilerParams(
            dimension_semantics=("arbitrary",), vmem_limit_bytes=VMEM_LIMIT),
        name=name,
    )(lp, h2, p, g_ple, w_gate, w_proj, g_fin)


def kernel(x_prompt, x_sample, p_prompt, p_sample, state_conv, norm_mix_g, w_in, conv_w, conv_b, conv_ln_g, conv_ln_b, gmlp_ln_g, gmlp_ln_b, w_s, b_s, w_out, norm_ffn_g, w_router_group, b_router_group, w_router_expert, b_router_expert, w_exp_gate, w_exp_up, w_exp_down, norm_ple_g, w_ple_gate, w_ple_proj, norm_final_g):
    depth = w_in.shape[0]
    assert depth == 1, "single-layer pipeline"
    batch, seq, _ = x_prompt.shape
    dec_batch, dec_seq, _ = x_sample.shape
    assert seq % TP == 0 and TP % CHUNK == 0 and TP % dec_seq == 0 and dec_batch % (TP // dec_seq) == 0
    sblk = TP // dec_seq
    n_prompt = batch * seq
    n_sample = dec_batch * dec_seq
    n_tok = n_prompt + n_sample
    n_src_tiles = n_tok // TP

    row = lambda a: a.reshape(1, -1)
    w_in_b = w_in[0].astype(BF16)
    w_out_b = w_out[0].astype(BF16)
    cw = jnp.concatenate([conv_w[0], jnp.zeros((1, C_CONV), F32)], axis=0)
    tril = jnp.tril(jnp.ones((CHUNK, CHUNK), bool))
    ws_m = jnp.where(tril[None], w_s[0], 0.0)
    ws_cat = jnp.concatenate([ws_m[0::2], ws_m[1::2]], axis=2).astype(BF16)
    bs_lane = jnp.repeat(jnp.transpose(b_s[0]), HEAD_DIM, axis=1)
    w_rt = jnp.zeros((CLS_ROWS, D_MODEL), F32)
    w_rt = w_rt.at[0:N_GROUPS].set(jnp.transpose(w_router_group[0]))
    w_rt = w_rt.at[N_GROUPS:N_GROUPS + N_EXPERTS].set(jnp.transpose(w_router_expert[0])).astype(BF16)
    b_r = jnp.zeros((CLS_ROWS, 1), F32)
    b_r = b_r.at[0:N_GROUPS, 0].set(b_router_group[0]).at[N_GROUPS:N_GROUPS + N_EXPERTS, 0].set(b_router_expert[0])
    tri = jnp.triu(jnp.ones((TP, TP), F32)).astype(BF16)
    wsl = jnp.where(jnp.tril(jnp.ones((dec_seq, dec_seq), bool))[None], w_s[0][:, :dec_seq, :dec_seq], 0.0)
    wsl = jnp.repeat(jnp.transpose(wsl, (1, 2, 0)).reshape(dec_seq * dec_seq, N_HEADS), HEAD_DIM, axis=1)
    bsl = jnp.repeat(jnp.transpose(b_s[0][:, :dec_seq]), HEAD_DIM, axis=1)
    w_gu = jnp.concatenate([w_exp_gate[0], w_exp_up[0]], axis=2).astype(BF16)
    w_d = w_exp_down[0].astype(BF16)

    weights = (row(norm_mix_g[0]), w_in_b, cw, row(conv_b[0]), row(conv_ln_g[0]), row(conv_ln_b[0]),
               row(gmlp_ln_g[0]), row(gmlp_ln_b[0]), ws_cat, bs_lane, wsl, bsl,
               w_out_b, row(norm_ffn_g[0]), w_rt, b_r)

    x_tm = jnp.transpose(x_sample, (1, 0, 2))
    hist_tm = jnp.transpose(state_conv[0], (1, 0, 2))
    h1, w1, lp, tab, cst_p, cst_s, v_s = _mixer(x_prompt, x_tm, hist_tm, weights, tri)

    rank0 = tab[:, :N_CLASSES, 0].astype(I32)
    cnt = tab[:, :N_CLASSES, 1].astype(I32)
    lsrc = tab[:, :N_CLASSES, 2].astype(I32) + (jnp.arange(n_src_tiles, dtype=I32) * TP)[:, None]
    total = rank0[-1] + cnt[-1]
    padded = ((total + TM - 1) // TM) * TM
    ends = jnp.cumsum(padded)
    offs = ends - padded
    n_tiles = (n_tok + N_CLASSES * (TM - 1)) // TM + 2
    tile_start = jnp.arange(n_tiles, dtype=I32) * TM
    n_used = ends[-1] // TM
    tile_cls = jnp.sum((tile_start[:, None] >= ends[None, :]).astype(I32), axis=1)
    tile_cls = jnp.where(tile_start < ends[-1], tile_cls, jnp.take(tile_cls, n_used - 1))
    tile_r0 = tile_start - jnp.take(offs, tile_cls)
    tile_nval = jnp.where(tile_start < ends[-1], jnp.clip(jnp.take(total, tile_cls) - tile_r0, 0, TM), 0).astype(I32)
    pair_lo = jnp.asarray(np.array([p[0] for p in _PAIRS], np.int32))
    pair_hi = jnp.asarray(np.array([p[1] for p in _PAIRS], np.int32))
    tile_elo = (tile_cls // 6) * EPG + jnp.take(pair_lo, tile_cls % 6)
    tile_ehi = (tile_cls // 6) * EPG + jnp.take(pair_hi, tile_cls % 6)

    h2 = _moe((tile_elo, tile_ehi, tile_nval, tile_cls, tile_r0),
              (rank0.reshape(-1), cnt.reshape(-1), lsrc.reshape(-1)),
              h1, w1, row(norm_ffn_g[0]), w_gu, w_d, n_src_tiles)

    lp = lp[:, 0:1, :]
    ple_w = (row(norm_ple_g[0]), w_ple_gate[0].astype(BF16), w_ple_proj[0].astype(BF16), row(norm_final_g))
    y_p = _ple(lp, h2, 0, p_prompt[0].reshape(n_prompt, PLE_DIM), *ple_w, name="ple_prompt")
    p_s_tm = jnp.transpose(p_sample[0].reshape(dec_batch // sblk, sblk, dec_seq, PLE_DIM), (0, 2, 1, 3))
    y_s = _ple(lp, h2, n_prompt // TP, p_s_tm.reshape(n_sample, PLE_DIM), *ple_w, name="ple_sample")

    y_prompt = y_p.reshape(batch, seq, D_MODEL)
    y_sample = jnp.transpose(y_s.reshape(dec_batch // sblk, dec_seq, sblk, D_MODEL), (0, 2, 1, 3))
    y_sample = y_sample.reshape(dec_batch, dec_seq, D_MODEL)
    state_conv_prompt = cst_p[:, HIST_OFF:, :][None]
    state_conv_sample = jnp.transpose(cst_s, (1, 0, 2))[None]
    state_gmlp_v_sample = jnp.transpose(v_s, (1, 0, 2))[None]
    return (y_prompt, y_sample, state_conv_prompt, state_conv_sample, state_gmlp_v_sample)
```

```python
import functools

import numpy as np
import jax
import jax.numpy as jnp
from jax import lax
from jax.experimental import pallas as pl
from jax.experimental.pallas import tpu as pltpu

F32 = jnp.float32
BF16 = jnp.bfloat16
I32 = jnp.int32

D_MODEL = 1024
C_CONV = 512
C_GMLP = 512
N_HEADS = 8
HEAD_DIM = 64
CONV_WIDTH = 31
CHUNK = 128
PLE_DIM = 256
N_GROUPS = 4
EPG = 4
N_EXPERTS = 16
D_EXPERT = 512
EPS = 1e-6
LANES = 128
SUBLANES = 8

N_CLASSES = N_GROUPS * 6
CLS_ROWS = 32
HIST = 32
HIST_OFF = HIST - (CONV_WIDTH - 1)
D_ROWS = D_MODEL // LANES
assert D_ROWS == SUBLANES
GROUP_ROWS = D_ROWS * SUBLANES

TP = 512
TM = 256
CONV_ROWS = 64
CONV_PITCH = 2
PIECES = tuple(TM >> b for b in range(TM.bit_length()))

VMEM_LIMIT = 58 * 1024 * 1024

_PAIRS = ((0, 1), (0, 2), (0, 3), (1, 2), (1, 3), (2, 3))


def _rms(x, g):
    ms = jnp.mean(x * x, axis=-1, keepdims=True)
    return x * lax.rsqrt(ms + EPS) * g


def _ln(x, g, b):
    mu = jnp.mean(x, axis=-1, keepdims=True)
    xc = x - mu
    var = jnp.mean(xc * xc, axis=-1, keepdims=True)
    return xc * lax.rsqrt(var + EPS) * g + b


def _max4(v):
    return jnp.maximum(jnp.maximum(v[0], v[1]), jnp.maximum(v[2], v[3]))


def _first4(v, m):
    return jnp.where(v[0] == m, 0, jnp.where(v[1] == m, 1, jnp.where(v[2] == m, 2, 3))).astype(I32)


def _route(lt):
    gl = [lt[i:i + 1, :] for i in range(N_GROUPS)]
    m = _max4(gl)
    g = _first4(gl, m)
    den = jnp.exp(gl[0] - m) + jnp.exp(gl[1] - m) + jnp.exp(gl[2] - m) + jnp.exp(gl[3] - m)
    g_w = 1.0 / den
    a = []
    for j in range(EPG):
        rows = [lt[N_GROUPS + EPG * q + j:N_GROUPS + EPG * q + j + 1, :] for q in range(N_GROUPS)]
        a.append(jnp.where(g == 0, rows[0], jnp.where(g == 1, rows[1], jnp.where(g == 2, rows[2], rows[3]))))
    v1 = _max4(a)
    i1 = _first4(a, v1)
    a2 = [jnp.where(i1 == j, -jnp.inf, a[j]) for j in range(EPG)]
    v2 = _max4(a2)
    i2 = _first4(a2, v2)
    e = jnp.exp(v2 - v1)
    s = 1.0 + e
    w1 = g_w / s
    w2 = g_w * e / s
    first_lo = i1 < i2
    lo = jnp.minimum(i1, i2)
    hi = jnp.maximum(i1, i2)
    w_lo = jnp.where(first_lo, w1, w2)
    w_hi = jnp.where(first_lo, w2, w1)
    pair = jnp.where(lo == 0, hi - 1, jnp.where(lo == 1, hi + 1, 5))
    return g * 6 + pair, w_lo, w_hi


def _load_token_blocks(ref, n_tok):
    return jnp.concatenate([ref[pl.ds(s, n_tok, stride=D_ROWS), :] for s in range(D_ROWS)], axis=1)


def _store_token_blocks(ref, x, n_tok):
    for s in range(D_ROWS):
        ref[pl.ds(s, n_tok, stride=D_ROWS), :] = x[:, s * LANES:(s + 1) * LANES]


def _router_tail(h, g_ffn_ref, w_rt_ref, b_r_ref, tri_ref, carry_ref, h1_ref, w1_ref, lp_ref, tab_ref,
                 stage_h, stage_w, lp_vmem, lp_smem):
    t = TP
    c = _rms(h, g_ffn_ref[...]).astype(BF16)
    lt = lax.dot_general(w_rt_ref[...], c, (((1,), (1,)), ((), ())), preferred_element_type=F32)
    lt = lt + b_r_ref[...]
    cls, w_lo, w_hi = _route(lt)
    rows = lax.broadcasted_iota(I32, (CLS_ROWS, t), 0)
    ohf = jnp.where(rows == cls, 1.0, 0.0).astype(F32)
    pre = jnp.dot(ohf.astype(BF16), tri_ref[...], preferred_element_type=F32)
    tot = jnp.sum(ohf, axis=1, keepdims=True)
    lstart = jnp.sum(jnp.where(cls < rows, 1.0, 0.0).astype(F32), axis=1, keepdims=True)
    lpos = jnp.sum(ohf * (pre - 1.0 + lstart), axis=0, keepdims=True).astype(I32)
    carry = carry_ref[:, 0:1]
    lane = lax.broadcasted_iota(I32, (CLS_ROWS, LANES), 1)
    tab_ref[0] = jnp.where(lane == 0, carry, jnp.where(lane == 1, tot, jnp.where(lane == 2, lstart, 0.0)))
    carry_ref[...] = jnp.broadcast_to(carry + tot, (CLS_ROWS, LANES))
    lp_ref[0] = jnp.broadcast_to(lpos, (SUBLANES, t))
    lp_vmem[...] = jnp.broadcast_to(lpos, (SUBLANES, t))

    ng = t // SUBLANES
    for s in range(D_ROWS):
        stage_h[:, s * SUBLANES:(s + 1) * SUBLANES, :] = h[:, s * LANES:(s + 1) * LANES].reshape(ng, SUBLANES, LANES)
    stage_w[:, 0:SUBLANES, :] = jnp.transpose(jnp.broadcast_to(w_lo, (LANES, t))).reshape(ng, SUBLANES, LANES)
    stage_w[:, SUBLANES:2 * SUBLANES, :] = jnp.transpose(jnp.broadcast_to(w_hi, (LANES, t))).reshape(
        ng, SUBLANES, LANES)
    pltpu.sync_copy(lp_vmem.at[pl.ds(0, 1)], lp_smem)

    def group(g, carry_):
        for k in range(SUBLANES):
            dst = pl.multiple_of(lp_smem[0, g * SUBLANES + k] * D_ROWS, D_ROWS)
            h1_ref[pl.ds(dst, D_ROWS), :] = stage_h[g, pl.ds(k, D_ROWS, stride=SUBLANES), :]
            w1_ref[pl.ds(dst, D_ROWS), :] = stage_w[g, pl.ds(k, D_ROWS, stride=SUBLANES), :]
        return carry_

    lax.fori_loop(0, ng, group, 0)


def _mixer_front(x, g_mix_ref, w_in_ref):
    a = _rms(x, g_mix_ref[...]).astype(BF16)
    z = jnp.dot(a, w_in_ref[...], preferred_element_type=F32)
    glu = z[:, 0:C_CONV] * jax.nn.sigmoid(z[:, C_CONV:2 * C_CONV])
    u = jax.nn.gelu(z[:, 2 * C_CONV:2 * C_CONV + C_GMLP])
    gv = jax.nn.gelu(z[:, 2 * C_CONV + C_GMLP:])
    return glu, u, gv


def _mixer_body(xp_ref, xs_ref, hist_ref, g_mix_ref, w_in_ref, cw_ref, cb_ref, clg_ref, clb_ref, glg_ref, glb_ref,
                ws_ref, bs_ref, wsl_ref, bsl_ref, w_out_ref, g_ffn_ref, w_rt_ref, b_r_ref, tri_ref,
                h1_ref, w1_ref, lp_ref, tab_ref, cstp_ref, csts_ref, vs_ref,
                glu_scr, xs_scr, cat_scr, carry_ref, stage_h, stage_w, lp_vmem, lp_smem,
                *, n_prompt_tiles, nj, dec_seq, sblk):
    i = pl.program_id(0)

    @pl.when(i == 0)
    def _():
        carry_ref[...] = jnp.zeros_like(carry_ref)
        stage_w[...] = jnp.zeros_like(stage_w)

    def tail(x):
        h = x + jnp.dot(cat_scr[...], w_out_ref[...], preferred_element_type=F32)
        _router_tail(h, g_ffn_ref, w_rt_ref, b_r_ref, tri_ref, carry_ref, h1_ref, w1_ref, lp_ref, tab_ref,
                     stage_h, stage_w, lp_vmem, lp_smem)

    @pl.when(i < n_prompt_tiles)
    def _prompt():
        j = lax.rem(i, nj)
        x = xp_ref[0]
        glu, u, gv = _mixer_front(x, g_mix_ref, w_in_ref)

        def time_rows(t0, n):
            return pl.ds(CONV_PITCH * t0, n, stride=CONV_PITCH)

        @pl.when(j == 0)
        def _():
            for s in range(C_CONV // LANES):
                glu_scr[s, time_rows(0, HIST), :] = jnp.zeros((HIST, LANES), F32)

        @pl.when(j > 0)
        def _():
            for s in range(C_CONV // LANES):
                glu_scr[s, time_rows(0, HIST), :] = glu_scr[s, time_rows(TP, HIST), :]

        for s in range(C_CONV // LANES):
            glu_scr[s, time_rows(HIST, TP), :] = glu[:, s * LANES:(s + 1) * LANES]
        cstp_ref[0] = jnp.concatenate([glu_scr[s, time_rows(TP, HIST), :] for s in range(C_CONV // LANES)], axis=1)

        cb = cb_ref[...]
        for r0 in range(0, TP, CONV_ROWS):
            slabs = []
            for s in range(C_CONV // LANES):
                acc = None
                for k in range(CONV_WIDTH):
                    term = (cw_ref[k:k + 1, s * LANES:(s + 1) * LANES]
                            * glu_scr[s, time_rows(r0 + HIST_OFF + k, CONV_ROWS), :])
                    acc = term if acc is None else acc + term
                slabs.append(acc)
            ya = jax.nn.silu(_ln(jnp.concatenate(slabs, axis=1) + cb, clg_ref[...], clb_ref[...]))
            cat_scr[r0:r0 + CONV_ROWS, 0:C_CONV] = ya.astype(BF16)

        v = _ln(gv, glg_ref[...], glb_ref[...])
        vb = v.astype(BF16)
        lane = lax.broadcasted_iota(I32, (CHUNK, LANES), 1)
        is_lo = lane < HEAD_DIM
        zero = jnp.zeros((CHUNK, LANES), BF16)
        for c in range(TP // CHUNK):
            for p in range(N_HEADS // 2):
                blk = vb[c * CHUNK:(c + 1) * CHUNK, p * LANES:(p + 1) * LANES]
                rhs = jnp.concatenate([jnp.where(is_lo, blk, zero), jnp.where(is_lo, zero, blk)], axis=0)
                mixed = jnp.dot(ws_ref[p], rhs, preferred_element_type=F32) + bs_ref[:, p * LANES:(p + 1) * LANES]
                yb = u[c * CHUNK:(c + 1) * CHUNK, p * LANES:(p + 1) * LANES] * mixed
                cat_scr[c * CHUNK:(c + 1) * CHUNK, C_CONV + p * LANES:C_CONV + (p + 1) * LANES] = yb.astype(BF16)
        tail(x)

    @pl.when(i >= n_prompt_tiles)
    def _sample():
        x = xs_ref[...].reshape(TP, D_MODEL)
        glu, u, gv = _mixer_front(x, g_mix_ref, w_in_ref)

        nh = CONV_WIDTH - 1
        xs_scr[0:nh] = hist_ref[...]
        xs_scr[nh:nh + dec_seq] = glu.reshape(dec_seq, sblk, C_CONV)
        csts_ref[...] = xs_scr[dec_seq:dec_seq + nh]

        cb = cb_ref[...]
        for t in range(dec_seq):
            acc = jnp.zeros((sblk, C_CONV), F32)
            for k in range(CONV_WIDTH):
                acc = acc + cw_ref[k:k + 1, :] * xs_scr[t + k]
            ya = jax.nn.silu(_ln(acc + cb, clg_ref[...], clb_ref[...]))
            cat_scr[t * sblk:(t + 1) * sblk, 0:C_CONV] = ya.astype(BF16)

        v = _ln(gv, glg_ref[...], glb_ref[...])
        vs_ref[...] = v.reshape(dec_seq, sblk, C_GMLP)
        for t in range(dec_seq):
            mixed = jnp.broadcast_to(bsl_ref[t:t + 1, :], (sblk, C_GMLP))
            for tp in range(t + 1):
                r = t * dec_seq + tp
                mixed = mixed + wsl_ref[r:r + 1, :] * v[tp * sblk:(tp + 1) * sblk, :]
            yb = u[t * sblk:(t + 1) * sblk, :] * mixed
            cat_scr[t * sblk:(t + 1) * sblk, C_CONV:] = yb.astype(BF16)
        tail(x)


def _for_pieces(ln, fn):
    for b, size in enumerate(PIECES):
        shift = size.bit_length()
        off = lax.shift_left(lax.shift_right_logical(ln, shift), shift)

        @pl.when((ln & size) != 0)
        def _(off=off, size=size):
            fn(off, size)


def _for_runs(k, tcls_ref, r0_ref, nval_ref, jlo_ref, jhi_ref, rank0_ref, cnt_ref, lsrc_ref, fn):
    c = tcls_ref[k]
    ra = r0_ref[k]
    rb = ra + nval_ref[k]

    def body(j, carry):
        idx = j * N_CLASSES + c
        s = rank0_ref[idx]
        lo = jnp.maximum(s, ra)
        hi = jnp.minimum(s + cnt_ref[idx], rb)
        ln = hi - lo

        @pl.when(ln > 0)
        def _():
            src = lsrc_ref[idx] + (lo - s)
            dst = lo - ra
            _for_pieces(ln, lambda off, size: fn(src + off, dst + off, size))

        return carry

    lax.fori_loop(jlo_ref[k], jhi_ref[k], body, 0)


def _blocks(ref, first, n):
    start = first * D_ROWS if isinstance(first, int) else pl.multiple_of(first * D_ROWS, D_ROWS)
    return ref.at[pl.ds(start, n * D_ROWS), :]


def _moe_body(grp_ref, lo_ref, hi_ref, nval_ref, tcls_ref, r0_ref, jlo_ref, jhi_ref, rank0_ref, cnt_ref, lsrc_ref,
              h1_hbm, w1_hbm, g_ffn_ref, wgu_ref, wd_ref,
              h2_hbm, hbuf, wbuf, obuf, gsem, ssem):
    del grp_ref
    k = pl.program_id(0)
    n = pl.num_programs(0)
    slot = lax.rem(k, 2)
    tables = (tcls_ref, r0_ref, nval_ref, jlo_ref, jhi_ref, rank0_ref, cnt_ref, lsrc_ref)

    def gather_tile(tile, to_slot):
        def copy(src, dst, size):
            pltpu.make_async_copy(_blocks(h1_hbm, src, size), _blocks(hbuf.at[to_slot], dst, size),
                                  gsem.at[to_slot]).start()
            pltpu.make_async_copy(_blocks(w1_hbm, src, size), _blocks(wbuf.at[to_slot], dst, size),
                                  gsem.at[to_slot]).start(priority=1)
        _for_runs(tile, *tables, copy)

    def wait_rows(n_rows, make_copy):
        _for_pieces(n_rows, lambda off, size: make_copy(size).wait())

    @pl.when(k == 0)
    def _():
        hbuf[...] = jnp.zeros_like(hbuf)
        wbuf[...] = jnp.zeros_like(wbuf)
        gather_tile(0, 0)

    @pl.when(jnp.logical_and(k >= 2, nval_ref[jnp.maximum(k - 2, 0)] > 0))
    def _():
        wait_rows(nval_ref[jnp.maximum(k - 2, 0)],
                  lambda size: pltpu.make_async_copy(_blocks(obuf.at[slot], 0, size), _blocks(h2_hbm, 0, size),
                                                     ssem.at[slot]))

    nxt = jnp.minimum(k + 1, n - 1)

    @pl.when(jnp.logical_and(k + 1 < n, nval_ref[nxt] > 0))
    def _():
        gather_tile(nxt, 1 - slot)

    @pl.when(nval_ref[k] > 0)
    def _():
        for src_hbm, dst_buf in ((h1_hbm, hbuf), (w1_hbm, wbuf)):
            wait_rows(nval_ref[k],
                      lambda size: pltpu.make_async_copy(_blocks(src_hbm, 0, size), _blocks(dst_buf.at[slot], 0, size),
                                                         gsem.at[slot]))
        h = _load_token_blocks(hbuf.at[slot], TM)
        wl = wbuf[slot, pl.ds(0, TM, stride=D_ROWS), :]
        wh = wbuf[slot, pl.ds(1, TM, stride=D_ROWS), :]
        c = _rms(h, g_ffn_ref[...]).astype(BF16)
        e_lo = lo_ref[k]
        e_hi = hi_ref[k]
        gu = jnp.dot(c, wgu_ref[e_lo], preferred_element_type=F32)
        hl = (jax.nn.silu(gu[:, 0:D_EXPERT]) * gu[:, D_EXPERT:]).astype(BF16)
        gu = jnp.dot(c, wgu_ref[e_hi], preferred_element_type=F32)
        hh = (jax.nn.silu(gu[:, 0:D_EXPERT]) * gu[:, D_EXPERT:]).astype(BF16)
        yl = jnp.dot(hl, wd_ref[e_lo], preferred_element_type=F32)
        yh = jnp.dot(hh, wd_ref[e_hi], preferred_element_type=F32)
        wl8 = jnp.concatenate([wl] * D_ROWS, axis=1)
        wh8 = jnp.concatenate([wh] * D_ROWS, axis=1)
        _store_token_blocks(obuf.at[slot], h + (wl8 * yl + wh8 * yh), TM)

        def copy_back(src, dst, size):
            pltpu.make_async_copy(_blocks(obuf.at[slot], dst, size), _blocks(h2_hbm, src, size),
                                  ssem.at[slot]).start()
        _for_runs(k, *tables, copy_back)


def _ple_body(lp_ref, h2_ref, p_ref, g_ple_ref, w_gate_ref, w_proj_ref, g_fin_ref, y_ref, stage_h):
    ng = TP // SUBLANES

    def group(g, carry):
        for k in range(SUBLANES):
            src = pl.multiple_of(lp_ref[0, 0, g * SUBLANES + k] * D_ROWS, D_ROWS)
            stage_h[g, pl.ds(k, D_ROWS, stride=SUBLANES), :] = h2_ref[pl.ds(src, D_ROWS), :]
        return carry

    lax.fori_loop(0, ng, group, 0)
    h = jnp.concatenate([stage_h[:, s * SUBLANES:(s + 1) * SUBLANES, :].reshape(TP, LANES) for s in range(D_ROWS)],
                        axis=1)
    c = _rms(h, g_ple_ref[...]).astype(BF16)
    gate = jax.nn.sigmoid(jnp.dot(c, w_gate_ref[...], preferred_element_type=F32))
    proj = jnp.dot(p_ref[...].astype(BF16), w_proj_ref[...], preferred_element_type=F32)
    h = h + proj * gate
    y_ref[...] = _rms(h, g_fin_ref[...])


def _full(shape, single=False):
    nd = len(shape)
    if single:
        return pl.BlockSpec(shape, lambda *_: (0,) * nd, pipeline_mode=pl.Buffered(1))
    return pl.BlockSpec(shape, lambda *_: (0,) * nd)


def _mixer(x_prompt, x_tm, hist_tm, weights, tri):
    batch, seq, _ = x_prompt.shape
    dec_seq, dec_batch, _ = x_tm.shape
    nj = seq // TP
    npt = batch * nj
    sblk = TP // dec_seq
    nst = dec_batch // sblk
    n_tiles = npt + nst
    nh = CONV_WIDTH - 1
    w_specs = [_full(w.shape, single=True) for w in weights]
    body = functools.partial(_mixer_body, n_prompt_tiles=npt, nj=nj, dec_seq=dec_seq, sblk=sblk)

    def p_idx(i):
        return jnp.minimum(i, npt - 1)

    def s_idx(i):
        return jnp.maximum(i - npt, 0)

    return pl.pallas_call(
        body,
        grid=(n_tiles,),
        in_specs=[pl.BlockSpec((1, TP, D_MODEL), lambda i: (p_idx(i) // nj, p_idx(i) % nj, 0)),
                  pl.BlockSpec((dec_seq, sblk, D_MODEL), lambda i: (0, s_idx(i), 0), pipeline_mode=pl.Buffered(1)),
                  pl.BlockSpec((nh, sblk, C_CONV), lambda i: (0, s_idx(i), 0), pipeline_mode=pl.Buffered(1))]
        + w_specs + [_full(tri.shape, single=True)],
        out_specs=[
            pl.BlockSpec((TP * D_ROWS, LANES), lambda i: (i, 0)),
            pl.BlockSpec((TP * D_ROWS, LANES), lambda i: (i, 0)),
            pl.BlockSpec((1, SUBLANES, TP), lambda i: (i, 0, 0)),
            pl.BlockSpec((1, CLS_ROWS, LANES), lambda i: (i, 0, 0)),
            pl.BlockSpec((1, HIST, C_CONV), lambda i: (p_idx(i) // nj, 0, 0)),
            pl.BlockSpec((nh, sblk, C_CONV), lambda i: (0, s_idx(i), 0)),
            pl.BlockSpec((dec_seq, sblk, C_GMLP), lambda i: (0, s_idx(i), 0)),
        ],
        out_shape=[
            jax.ShapeDtypeStruct((n_tiles * TP * D_ROWS, LANES), F32),
            jax.ShapeDtypeStruct((n_tiles * TP * D_ROWS, LANES), F32),
            jax.ShapeDtypeStruct((n_tiles, SUBLANES, TP), I32),
            jax.ShapeDtypeStruct((n_tiles, CLS_ROWS, LANES), F32),
            jax.ShapeDtypeStruct((batch, HIST, C_CONV), F32),
            jax.ShapeDtypeStruct((nh, dec_batch, C_CONV), F32),
            jax.ShapeDtypeStruct((dec_seq, dec_batch, C_GMLP), F32),
        ],
        scratch_shapes=[
            pltpu.VMEM((C_CONV // LANES, CONV_PITCH * (HIST + TP), LANES), F32),
            pltpu.VMEM((nh + dec_seq, sblk, C_CONV), F32),
            pltpu.VMEM((TP, D_MODEL), BF16),
            pltpu.VMEM((CLS_ROWS, LANES), F32),
            pltpu.VMEM((TP // SUBLANES, GROUP_ROWS, LANES), F32),
            pltpu.VMEM((TP // SUBLANES, GROUP_ROWS, LANES), F32),
            pltpu.VMEM((SUBLANES, TP), I32),
            pltpu.SMEM((1, TP), I32),
        ],
        compiler_params=pltpu.CompilerParams(
            dimension_semantics=("arbitrary",), vmem_limit_bytes=VMEM_LIMIT),
        name="mixer",
    )(x_prompt, x_tm, hist_tm, *weights, tri)


def _moe(tile_tabs, run_tabs, h1, w1, g_ffn, w_gu, w_d):
    n_tiles = tile_tabs[0].shape[0]
    n_prefetch = len(tile_tabs) + len(run_tabs)
    grid_spec = pltpu.PrefetchScalarGridSpec(
        num_scalar_prefetch=n_prefetch,
        grid=(n_tiles,),
        in_specs=[
            pl.BlockSpec(memory_space=pl.ANY),
            pl.BlockSpec(memory_space=pl.ANY),
            pl.BlockSpec((1, D_MODEL), lambda i, *_: (0, 0)),
            pl.BlockSpec((EPG, D_MODEL, 2 * D_EXPERT), lambda i, grp, *_: (grp[i], 0, 0)),
            pl.BlockSpec((EPG, D_EXPERT, D_MODEL), lambda i, grp, *_: (grp[i], 0, 0)),
        ],
        out_specs=pl.BlockSpec(memory_space=pl.ANY),
        scratch_shapes=[pltpu.VMEM((2, TM * D_ROWS, LANES), F32),
                        pltpu.VMEM((2, TM * D_ROWS, LANES), F32),
                        pltpu.VMEM((2, TM * D_ROWS, LANES), F32),
                        pltpu.SemaphoreType.DMA((2,)),
                        pltpu.SemaphoreType.DMA((2,))],
    )
    return pl.pallas_call(
        _moe_body,
        grid_spec=grid_spec,
        out_shape=jax.ShapeDtypeStruct(h1.shape, F32),
        compiler_params=pltpu.CompilerParams(
            dimension_semantics=("arbitrary",), vmem_limit_bytes=VMEM_LIMIT),
        name="moe",
    )(*tile_tabs, *run_tabs, h1, w1, g_ffn, w_gu, w_d)


def _ple(lp, h2, tile0, p, g_ple, w_gate, w_proj, g_fin, name):
    n_tiles = p.shape[0] // TP
    return pl.pallas_call(
        _ple_body,
        grid=(n_tiles,),
        in_specs=[
            pl.BlockSpec((1, 1, TP), lambda i: (tile0 + i, 0, 0), memory_space=pltpu.SMEM),
            pl.BlockSpec((TP * D_ROWS, LANES), lambda i: (tile0 + i, 0)),
            pl.BlockSpec((TP, PLE_DIM), lambda i: (i, 0)),
            _full(g_ple.shape), _full(w_gate.shape), _full(w_proj.shape), _full(g_fin.shape),
        ],
        out_specs=pl.BlockSpec((TP, D_MODEL), lambda i: (i, 0)),
        out_shape=jax.ShapeDtypeStruct((n_tiles * TP, D_MODEL), F32),
        scratch_shapes=[pltpu.VMEM((TP // SUBLANES, GROUP_ROWS, LANES), F32)],
        compiler_params=pltpu.CompilerParams(
            dimension_semantics=("arbitrary",), vmem_limit_bytes=VMEM_LIMIT),
        name=name,
    )(lp, h2, p, g_ple, w_gate, w_proj, g_fin)


def kernel(x_prompt, x_sample, p_prompt, p_sample, state_conv, norm_mix_g, w_in, conv_w, conv_b, conv_ln_g, conv_ln_b, gmlp_ln_g, gmlp_ln_b, w_s, b_s, w_out, norm_ffn_g, w_router_group, b_router_group, w_router_expert, b_router_expert, w_exp_gate, w_exp_up, w_exp_down, norm_ple_g, w_ple_gate, w_ple_proj, norm_final_g):
    depth = w_in.shape[0]
    assert depth == 1, "single-layer pipeline"
    batch, seq, _ = x_prompt.shape
    dec_batch, dec_seq, _ = x_sample.shape
    assert seq % TP == 0 and TP % CHUNK == 0 and TP % dec_seq == 0 and dec_batch % (TP // dec_seq) == 0
    sblk = TP // dec_seq
    n_prompt = batch * seq
    n_sample = dec_batch * dec_seq
    n_tok = n_prompt + n_sample
    n_src_tiles = n_tok // TP

    row = lambda a: a.reshape(1, -1)
    w_in_b = w_in[0].astype(BF16)
    w_out_b = w_out[0].astype(BF16)
    cw = jnp.concatenate([conv_w[0], jnp.zeros((1, C_CONV), F32)], axis=0)
    tril = jnp.tril(jnp.ones((CHUNK, CHUNK), bool))
    ws_m = jnp.where(tril[None], w_s[0], 0.0)
    ws_cat = jnp.concatenate([ws_m[0::2], ws_m[1::2]], axis=2).astype(BF16)
    bs_lane = jnp.repeat(jnp.transpose(b_s[0]), HEAD_DIM, axis=1)
    w_rt = jnp.zeros((CLS_ROWS, D_MODEL), F32)
    w_rt = w_rt.at[0:N_GROUPS].set(jnp.transpose(w_router_group[0]))
    w_rt = w_rt.at[N_GROUPS:N_GROUPS + N_EXPERTS].set(jnp.transpose(w_router_expert[0])).astype(BF16)
    b_r = jnp.zeros((CLS_ROWS, 1), F32)
    b_r = b_r.at[0:N_GROUPS, 0].set(b_router_group[0]).at[N_GROUPS:N_GROUPS + N_EXPERTS, 0].set(b_router_expert[0])
    tri = jnp.triu(jnp.ones((TP, TP), F32)).astype(BF16)
    wsl = jnp.where(jnp.tril(jnp.ones((dec_seq, dec_seq), bool))[None], w_s[0][:, :dec_seq, :dec_seq], 0.0)
    wsl = jnp.repeat(jnp.transpose(wsl, (1, 2, 0)).reshape(dec_seq * dec_seq, N_HEADS), HEAD_DIM, axis=1)
    bsl = jnp.repeat(jnp.transpose(b_s[0][:, :dec_seq]), HEAD_DIM, axis=1)
    w_gu = jnp.concatenate([w_exp_gate[0], w_exp_up[0]], axis=2).astype(BF16)
    w_d = w_exp_down[0].astype(BF16)

    weights = (row(norm_mix_g[0]), w_in_b, cw, row(conv_b[0]), row(conv_ln_g[0]), row(conv_ln_b[0]),
               row(gmlp_ln_g[0]), row(gmlp_ln_b[0]), ws_cat, bs_lane, wsl, bsl,
               w_out_b, row(norm_ffn_g[0]), w_rt, b_r)

    x_tm = jnp.transpose(x_sample, (1, 0, 2))
    hist_tm = jnp.transpose(state_conv[0], (1, 0, 2))
    h1, w1, lp, tab, cst_p, cst_s, v_s = _mixer(x_prompt, x_tm, hist_tm, weights, tri)

    rank0 = tab[:, :N_CLASSES, 0].astype(I32)
    cnt = tab[:, :N_CLASSES, 1].astype(I32)
    lsrc = tab[:, :N_CLASSES, 2].astype(I32) + (jnp.arange(n_src_tiles, dtype=I32) * TP)[:, None]
    total = rank0[-1] + cnt[-1]
    padded = ((total + TM - 1) // TM) * TM
    ends = jnp.cumsum(padded)
    offs = ends - padded
    n_tiles = (n_tok + N_CLASSES * (TM - 1)) // TM + 2
    tile_start = jnp.arange(n_tiles, dtype=I32) * TM
    n_used = ends[-1] // TM
    tile_cls = jnp.sum((tile_start[:, None] >= ends[None, :]).astype(I32), axis=1)
    tile_cls = jnp.where(tile_start < ends[-1], tile_cls, jnp.take(tile_cls, n_used - 1))
    tile_r0 = tile_start - jnp.take(offs, tile_cls)
    tile_nval = jnp.where(tile_start < ends[-1], jnp.clip(jnp.take(total, tile_cls) - tile_r0, 0, TM), 0).astype(I32)
    pair_lo = jnp.asarray(np.array([p[0] for p in _PAIRS], np.int32))
    pair_hi = jnp.asarray(np.array([p[1] for p in _PAIRS], np.int32))
    tile_lo = jnp.take(pair_lo, tile_cls % 6)
    tile_hi = jnp.take(pair_hi, tile_cls % 6)
    run_beg = jnp.take(rank0, tile_cls, axis=1)
    run_end = run_beg + jnp.take(cnt, tile_cls, axis=1)
    tile_jlo = jnp.sum((run_end <= tile_r0[None, :]).astype(I32), axis=0)
    tile_jhi = jnp.sum((run_beg < (tile_r0 + tile_nval)[None, :]).astype(I32), axis=0)
    tile_jhi = jnp.maximum(tile_jhi, tile_jlo)

    h2 = _moe((tile_cls // 6, tile_lo, tile_hi, tile_nval, tile_cls, tile_r0, tile_jlo, tile_jhi),
              (rank0.reshape(-1), cnt.reshape(-1), lsrc.reshape(-1)),
              h1, w1, row(norm_ffn_g[0]), w_gu, w_d)

    lp = lp[:, 0:1, :]
    ple_w = (row(norm_ple_g[0]), w_ple_gate[0].astype(BF16), w_ple_proj[0].astype(BF16), row(norm_final_g))
    y_p = _ple(lp, h2, 0, p_prompt[0].reshape(n_prompt, PLE_DIM), *ple_w, name="ple_prompt")
    p_s_tm = jnp.transpose(p_sample[0].reshape(dec_batch // sblk, sblk, dec_seq, PLE_DIM), (0, 2, 1, 3))
    y_s = _ple(lp, h2, n_prompt // TP, p_s_tm.reshape(n_sample, PLE_DIM), *ple_w, name="ple_sample")

    y_prompt = y_p.reshape(batch, seq, D_MODEL)
    y_sample = jnp.transpose(y_s.reshape(dec_batch // sblk, dec_seq, sblk, D_MODEL), (0, 2, 1, 3))
    y_sample = y_sample.reshape(dec_batch, dec_seq, D_MODEL)
    state_conv_prompt = cst_p[:, HIST_OFF:, :][None]
    state_conv_sample = jnp.transpose(cst_s, (1, 0, 2))[None]
    state_gmlp_v_sample = jnp.transpose(v_s, (1, 0, 2))[None]
    return (y_prompt, y_sample, state_conv_prompt, state_conv_sample, state_gmlp_v_sample)
```

```python
import functools

import numpy as np
import jax
import jax.numpy as jnp
from jax import lax
from jax.experimental import pallas as pl
from jax.experimental.pallas import tpu as pltpu

F32 = jnp.float32
BF16 = jnp.bfloat16
I32 = jnp.int32

D_MODEL = 1024
C_CONV = 512
C_GMLP = 512
N_HEADS = 8
HEAD_DIM = 64
CONV_WIDTH = 31
CHUNK = 128
PLE_DIM = 256
N_GROUPS = 4
EPG = 4
N_EXPERTS = 16
D_EXPERT = 512
EPS = 1e-6
LANES = 128
SUBLANES = 8

N_CLASSES = N_GROUPS * 6
CLS_ROWS = 32
HIST = 32
HIST_OFF = HIST - (CONV_WIDTH - 1)
D_ROWS = D_MODEL // LANES
assert D_ROWS == SUBLANES
GROUP_ROWS = D_ROWS * SUBLANES

TP = 512
TM = 256
CONV_ROWS = 64
CONV_PITCH = 2
PIECES = tuple(TM >> b for b in range(TM.bit_length()))

VMEM_LIMIT = 58 * 1024 * 1024

_PAIRS = ((0, 1), (0, 2), (0, 3), (1, 2), (1, 3), (2, 3))


def _rms(x, g):
    ms = jnp.mean(x * x, axis=-1, keepdims=True)
    return x * lax.rsqrt(ms + EPS) * g


def _ln(x, g, b):
    mu = jnp.mean(x, axis=-1, keepdims=True)
    xc = x - mu
    var = jnp.mean(xc * xc, axis=-1, keepdims=True)
    return xc * lax.rsqrt(var + EPS) * g + b


def _max4(v):
    return jnp.maximum(jnp.maximum(v[0], v[1]), jnp.maximum(v[2], v[3]))


def _first4(v, m):
    return jnp.where(v[0] == m, 0, jnp.where(v[1] == m, 1, jnp.where(v[2] == m, 2, 3))).astype(I32)


def _route(lt):
    gl = [lt[i:i + 1, :] for i in range(N_GROUPS)]
    m = _max4(gl)
    g = _first4(gl, m)
    den = jnp.exp(gl[0] - m) + jnp.exp(gl[1] - m) + jnp.exp(gl[2] - m) + jnp.exp(gl[3] - m)
    g_w = 1.0 / den
    a = []
    for j in range(EPG):
        rows = [lt[N_GROUPS + EPG * q + j:N_GROUPS + EPG * q + j + 1, :] for q in range(N_GROUPS)]
        a.append(jnp.where(g == 0, rows[0], jnp.where(g == 1, rows[1], jnp.where(g == 2, rows[2], rows[3]))))
    v1 = _max4(a)
    i1 = _first4(a, v1)
    a2 = [jnp.where(i1 == j, -jnp.inf, a[j]) for j in range(EPG)]
    v2 = _max4(a2)
    i2 = _first4(a2, v2)
    e = jnp.exp(v2 - v1)
    s = 1.0 + e
    w1 = g_w / s
    w2 = g_w * e / s
    first_lo = i1 < i2
    lo = jnp.minimum(i1, i2)
    hi = jnp.maximum(i1, i2)
    w_lo = jnp.where(first_lo, w1, w2)
    w_hi = jnp.where(first_lo, w2, w1)
    pair = jnp.where(lo == 0, hi - 1, jnp.where(lo == 1, hi + 1, 5))
    return g * 6 + pair, w_lo, w_hi


def _load_token_blocks(ref, n_tok):
    return jnp.concatenate([ref[pl.ds(s, n_tok, stride=D_ROWS), :] for s in range(D_ROWS)], axis=1)


def _store_token_blocks(ref, x, n_tok):
    for s in range(D_ROWS):
        ref[pl.ds(s, n_tok, stride=D_ROWS), :] = x[:, s * LANES:(s + 1) * LANES]


def _router_tail(h, g_ffn_ref, w_rt_ref, b_r_ref, tri_ref, carry_ref, h1_ref, w1_ref, lp_ref, tab_ref,
                 stage_h, stage_w, lp_vmem, lp_smem):
    t = TP
    c = _rms(h, g_ffn_ref[...]).astype(BF16)
    lt = lax.dot_general(w_rt_ref[...], c, (((1,), (1,)), ((), ())), preferred_element_type=F32)
    lt = lt + b_r_ref[...]
    cls, w_lo, w_hi = _route(lt)
    rows = lax.broadcasted_iota(I32, (CLS_ROWS, t), 0)
    ohf = jnp.where(rows == cls, 1.0, 0.0).astype(F32)
    pre = jnp.dot(ohf.astype(BF16), tri_ref[...], preferred_element_type=F32)
    tot = jnp.sum(ohf, axis=1, keepdims=True)
    lstart = jnp.sum(jnp.where(cls < rows, 1.0, 0.0).astype(F32), axis=1, keepdims=True)
    lpos = jnp.sum(ohf * (pre - 1.0 + lstart), axis=0, keepdims=True).astype(I32)
    carry = carry_ref[:, 0:1]
    lane = lax.broadcasted_iota(I32, (CLS_ROWS, LANES), 1)
    tab_ref[0] = jnp.where(lane == 0, carry, jnp.where(lane == 1, tot, jnp.where(lane == 2, lstart, 0.0)))
    carry_ref[...] = jnp.broadcast_to(carry + tot, (CLS_ROWS, LANES))
    lp_ref[0] = jnp.broadcast_to(lpos, (SUBLANES, t))
    lp_vmem[...] = jnp.broadcast_to(lpos, (SUBLANES, t))

    ng = t // SUBLANES
    for s in range(D_ROWS):
        stage_h[:, s * SUBLANES:(s + 1) * SUBLANES, :] = h[:, s * LANES:(s + 1) * LANES].reshape(ng, SUBLANES, LANES)
    stage_w[:, 0:SUBLANES, :] = jnp.transpose(jnp.broadcast_to(w_lo, (LANES, t))).reshape(ng, SUBLANES, LANES)
    stage_w[:, SUBLANES:2 * SUBLANES, :] = jnp.transpose(jnp.broadcast_to(w_hi, (LANES, t))).reshape(
        ng, SUBLANES, LANES)
    pltpu.sync_copy(lp_vmem.at[pl.ds(0, 1)], lp_smem)

    def group(g, carry_):
        for k in range(SUBLANES):
            dst = pl.multiple_of(lp_smem[0, g * SUBLANES + k] * D_ROWS, D_ROWS)
            h1_ref[pl.ds(dst, D_ROWS), :] = stage_h[g, pl.ds(k, D_ROWS, stride=SUBLANES), :]
            w1_ref[pl.ds(dst, D_ROWS), :] = stage_w[g, pl.ds(k, D_ROWS, stride=SUBLANES), :]
        return carry_

    lax.fori_loop(0, ng, group, 0)


def _mixer_front(x, g_mix_ref, w_in_ref):
    a = _rms(x, g_mix_ref[...]).astype(BF16)
    z = jnp.dot(a, w_in_ref[...], preferred_element_type=F32)
    glu = z[:, 0:C_CONV] * jax.nn.sigmoid(z[:, C_CONV:2 * C_CONV])
    u = jax.nn.gelu(z[:, 2 * C_CONV:2 * C_CONV + C_GMLP])
    gv = jax.nn.gelu(z[:, 2 * C_CONV + C_GMLP:])
    return glu, u, gv


def _mixer_body(xp_ref, xs_ref, hist_ref, g_mix_ref, w_in_ref, cw_ref, cb_ref, clg_ref, clb_ref, glg_ref, glb_ref,
                ws_ref, bs_ref, wsl_ref, bsl_ref, w_out_ref, g_ffn_ref, w_rt_ref, b_r_ref, tri_ref,
                eg_ref, eu_ref, ed_ref,
                h1_ref, w1_ref, lp_ref, tab_ref, cstp_ref, csts_ref, vs_ref, egb_ref, eub_ref, edb_ref,
                glu_scr, xs_scr, cat_scr, carry_ref, stage_h, stage_w, lp_vmem, lp_smem,
                *, n_prompt_tiles, nj, dec_seq, sblk):
    i = pl.program_id(0)

    egb_ref[...] = eg_ref[...].astype(BF16)
    eub_ref[...] = eu_ref[...].astype(BF16)
    edb_ref[...] = ed_ref[...].astype(BF16)

    @pl.when(i == 0)
    def _():
        carry_ref[...] = jnp.zeros_like(carry_ref)
        stage_w[...] = jnp.zeros_like(stage_w)

    def tail(x):
        h = x + jnp.dot(cat_scr[...], w_out_ref[...], preferred_element_type=F32)
        _router_tail(h, g_ffn_ref, w_rt_ref, b_r_ref, tri_ref, carry_ref, h1_ref, w1_ref, lp_ref, tab_ref,
                     stage_h, stage_w, lp_vmem, lp_smem)

    @pl.when(i < n_prompt_tiles)
    def _prompt():
        j = lax.rem(i, nj)
        x = xp_ref[0]
        glu, u, gv = _mixer_front(x, g_mix_ref, w_in_ref)

        def time_rows(t0, n):
            return pl.ds(CONV_PITCH * t0, n, stride=CONV_PITCH)

        @pl.when(j == 0)
        def _():
            for s in range(C_CONV // LANES):
                glu_scr[s, time_rows(0, HIST), :] = jnp.zeros((HIST, LANES), F32)

        @pl.when(j > 0)
        def _():
            for s in range(C_CONV // LANES):
                glu_scr[s, time_rows(0, HIST), :] = glu_scr[s, time_rows(TP, HIST), :]

        for s in range(C_CONV // LANES):
            glu_scr[s, time_rows(HIST, TP), :] = glu[:, s * LANES:(s + 1) * LANES]
        cstp_ref[0] = jnp.concatenate([glu_scr[s, time_rows(TP, HIST), :] for s in range(C_CONV // LANES)], axis=1)

        cb = cb_ref[...]
        for r0 in range(0, TP, CONV_ROWS):
            slabs = []
            for s in range(C_CONV // LANES):
                acc = None
                for k in range(CONV_WIDTH):
                    term = (cw_ref[k:k + 1, s * LANES:(s + 1) * LANES]
                            * glu_scr[s, time_rows(r0 + HIST_OFF + k, CONV_ROWS), :])
                    acc = term if acc is None else acc + term
                slabs.append(acc)
            ya = jax.nn.silu(_ln(jnp.concatenate(slabs, axis=1) + cb, clg_ref[...], clb_ref[...]))
            cat_scr[r0:r0 + CONV_ROWS, 0:C_CONV] = ya.astype(BF16)

        v = _ln(gv, glg_ref[...], glb_ref[...])
        vb = v.astype(BF16)
        lane = lax.broadcasted_iota(I32, (CHUNK, LANES), 1)
        is_lo = lane < HEAD_DIM
        zero = jnp.zeros((CHUNK, LANES), BF16)
        for c in range(TP // CHUNK):
            for p in range(N_HEADS // 2):
                blk = vb[c * CHUNK:(c + 1) * CHUNK, p * LANES:(p + 1) * LANES]
                rhs = jnp.concatenate([jnp.where(is_lo, blk, zero), jnp.where(is_lo, zero, blk)], axis=0)
                mixed = jnp.dot(ws_ref[p], rhs, preferred_element_type=F32) + bs_ref[:, p * LANES:(p + 1) * LANES]
                yb = u[c * CHUNK:(c + 1) * CHUNK, p * LANES:(p + 1) * LANES] * mixed
                cat_scr[c * CHUNK:(c + 1) * CHUNK, C_CONV + p * LANES:C_CONV + (p + 1) * LANES] = yb.astype(BF16)
        tail(x)

    @pl.when(i >= n_prompt_tiles)
    def _sample():
        x = xs_ref[...].reshape(TP, D_MODEL)
        glu, u, gv = _mixer_front(x, g_mix_ref, w_in_ref)

        nh = CONV_WIDTH - 1
        xs_scr[0:nh] = hist_ref[...]
        xs_scr[nh:nh + dec_seq] = glu.reshape(dec_seq, sblk, C_CONV)
        csts_ref[...] = xs_scr[dec_seq:dec_seq + nh]

        cb = cb_ref[...]
        for t in range(dec_seq):
            acc = jnp.zeros((sblk, C_CONV), F32)
            for k in range(CONV_WIDTH):
                acc = acc + cw_ref[k:k + 1, :] * xs_scr[t + k]
            ya = jax.nn.silu(_ln(acc + cb, clg_ref[...], clb_ref[...]))
            cat_scr[t * sblk:(t + 1) * sblk, 0:C_CONV] = ya.astype(BF16)

        v = _ln(gv, glg_ref[...], glb_ref[...])
        vs_ref[...] = v.reshape(dec_seq, sblk, C_GMLP)
        for t in range(dec_seq):
            mixed = jnp.broadcast_to(bsl_ref[t:t + 1, :], (sblk, C_GMLP))
            for tp in range(t + 1):
                r = t * dec_seq + tp
                mixed = mixed + wsl_ref[r:r + 1, :] * v[tp * sblk:(tp + 1) * sblk, :]
            yb = u[t * sblk:(t + 1) * sblk, :] * mixed
            cat_scr[t * sblk:(t + 1) * sblk, C_CONV:] = yb.astype(BF16)
        tail(x)


def _for_pieces(ln, fn):
    for b, size in enumerate(PIECES):
        shift = size.bit_length()
        off = lax.shift_left(lax.shift_right_logical(ln, shift), shift)

        @pl.when((ln & size) != 0)
        def _(off=off, size=size):
            fn(off, size)


def _for_runs(k, tcls_ref, r0_ref, nval_ref, jlo_ref, jhi_ref, rank0_ref, cnt_ref, lsrc_ref, fn):
    c = tcls_ref[k]
    ra = r0_ref[k]
    rb = ra + nval_ref[k]

    def body(j, carry):
        idx = j * N_CLASSES + c
        s = rank0_ref[idx]
        lo = jnp.maximum(s, ra)
        hi = jnp.minimum(s + cnt_ref[idx], rb)
        ln = hi - lo

        @pl.when(ln > 0)
        def _():
            src = lsrc_ref[idx] + (lo - s)
            dst = lo - ra
            _for_pieces(ln, lambda off, size: fn(src + off, dst + off, size))

        return carry

    lax.fori_loop(jlo_ref[k], jhi_ref[k], body, 0)


def _blocks(ref, first, n):
    start = first * D_ROWS if isinstance(first, int) else pl.multiple_of(first * D_ROWS, D_ROWS)
    return ref.at[pl.ds(start, n * D_ROWS), :]


def _moe_body(grp_ref, lo_ref, hi_ref, nval_ref, tcls_ref, r0_ref, jlo_ref, jhi_ref, rank0_ref, cnt_ref, lsrc_ref,
              h1_hbm, w1_hbm, g_ffn_ref, wg_ref, wu_ref, wd_ref,
              h2_hbm, hbuf, wbuf, obuf, gsem, ssem):
    del grp_ref
    k = pl.program_id(0)
    n = pl.num_programs(0)
    slot = lax.rem(k, 2)
    tables = (tcls_ref, r0_ref, nval_ref, jlo_ref, jhi_ref, rank0_ref, cnt_ref, lsrc_ref)

    def gather_tile(tile, to_slot):
        def copy(src, dst, size):
            pltpu.make_async_copy(_blocks(h1_hbm, src, size), _blocks(hbuf.at[to_slot], dst, size),
                                  gsem.at[to_slot]).start()
            pltpu.make_async_copy(_blocks(w1_hbm, src, size), _blocks(wbuf.at[to_slot], dst, size),
                                  gsem.at[to_slot]).start(priority=1)
        _for_runs(tile, *tables, copy)

    def wait_rows(n_rows, make_copy):
        _for_pieces(n_rows, lambda off, size: make_copy(size).wait())

    @pl.when(k == 0)
    def _():
        hbuf[...] = jnp.zeros_like(hbuf)
        wbuf[...] = jnp.zeros_like(wbuf)
        gather_tile(0, 0)

    @pl.when(jnp.logical_and(k >= 2, nval_ref[jnp.maximum(k - 2, 0)] > 0))
    def _():
        wait_rows(nval_ref[jnp.maximum(k - 2, 0)],
                  lambda size: pltpu.make_async_copy(_blocks(obuf.at[slot], 0, size), _blocks(h2_hbm, 0, size),
                                                     ssem.at[slot]))

    nxt = jnp.minimum(k + 1, n - 1)

    @pl.when(jnp.logical_and(k + 1 < n, nval_ref[nxt] > 0))
    def _():
        gather_tile(nxt, 1 - slot)

    @pl.when(nval_ref[k] > 0)
    def _():
        for src_hbm, dst_buf in ((h1_hbm, hbuf), (w1_hbm, wbuf)):
            wait_rows(nval_ref[k],
                      lambda size: pltpu.make_async_copy(_blocks(src_hbm, 0, size), _blocks(dst_buf.at[slot], 0, size),
                                                         gsem.at[slot]))
        h = _load_token_blocks(hbuf.at[slot], TM)
        wl = wbuf[slot, pl.ds(0, TM, stride=D_ROWS), :]
        wh = wbuf[slot, pl.ds(1, TM, stride=D_ROWS), :]
        c = _rms(h, g_ffn_ref[...]).astype(BF16)
        e_lo = lo_ref[k]
        e_hi = hi_ref[k]

        def hidden(e):
            gate = jnp.dot(c, wg_ref[e], preferred_element_type=F32)
            return (jax.nn.silu(gate) * jnp.dot(c, wu_ref[e], preferred_element_type=F32)).astype(BF16)

        hl = hidden(e_lo)
        hh = hidden(e_hi)
        yl = jnp.dot(hl, wd_ref[e_lo], preferred_element_type=F32)
        yh = jnp.dot(hh, wd_ref[e_hi], preferred_element_type=F32)
        wl8 = jnp.concatenate([wl] * D_ROWS, axis=1)
        wh8 = jnp.concatenate([wh] * D_ROWS, axis=1)
        _store_token_blocks(obuf.at[slot], h + (wl8 * yl + wh8 * yh), TM)

        def copy_back(src, dst, size):
            pltpu.make_async_copy(_blocks(obuf.at[slot], dst, size), _blocks(h2_hbm, src, size),
                                  ssem.at[slot]).start()
        _for_runs(k, *tables, copy_back)


def _ple_body(lp_ref, h2_ref, p_ref, g_ple_ref, w_gate_ref, w_proj_ref, g_fin_ref, y_ref, stage_h):
    ng = TP // SUBLANES

    def group(g, carry):
        for k in range(SUBLANES):
            src = pl.multiple_of(lp_ref[0, 0, g * SUBLANES + k] * D_ROWS, D_ROWS)
            stage_h[g, pl.ds(k, D_ROWS, stride=SUBLANES), :] = h2_ref[pl.ds(src, D_ROWS), :]
        return carry

    lax.fori_loop(0, ng, group, 0)
    h = jnp.concatenate([stage_h[:, s * SUBLANES:(s + 1) * SUBLANES, :].reshape(TP, LANES) for s in range(D_ROWS)],
                        axis=1)
    c = _rms(h, g_ple_ref[...]).astype(BF16)
    gate = jax.nn.sigmoid(jnp.dot(c, w_gate_ref[...], preferred_element_type=F32))
    proj = jnp.dot(p_ref[...].astype(BF16), w_proj_ref[...], preferred_element_type=F32)
    h = h + proj * gate
    y_ref[...] = _rms(h, g_fin_ref[...])


def _full(shape, single=False):
    nd = len(shape)
    if single:
        return pl.BlockSpec(shape, lambda *_: (0,) * nd, pipeline_mode=pl.Buffered(1))
    return pl.BlockSpec(shape, lambda *_: (0,) * nd)


def _mixer(x_prompt, x_tm, hist_tm, weights, tri, expert_w):
    batch, seq, _ = x_prompt.shape
    dec_seq, dec_batch, _ = x_tm.shape
    nj = seq // TP
    npt = batch * nj
    sblk = TP // dec_seq
    nst = dec_batch // sblk
    n_tiles = npt + nst
    nh = CONV_WIDTH - 1
    w_specs = [_full(w.shape, single=True) for w in weights]
    body = functools.partial(_mixer_body, n_prompt_tiles=npt, nj=nj, dec_seq=dec_seq, sblk=sblk)

    def p_idx(i):
        return jnp.minimum(i, npt - 1)

    def s_idx(i):
        return jnp.maximum(i - npt, 0)

    assert npt % N_EXPERTS == 0
    parts = npt // N_EXPERTS
    e_specs = [pl.BlockSpec((1, w.shape[1] // parts, w.shape[2]), lambda i: (p_idx(i) // parts, p_idx(i) % parts, 0))
               for w in expert_w]
    e_shapes = [jax.ShapeDtypeStruct(w.shape, BF16) for w in expert_w]

    return pl.pallas_call(
        body,
        grid=(n_tiles,),
        in_specs=[pl.BlockSpec((1, TP, D_MODEL), lambda i: (p_idx(i) // nj, p_idx(i) % nj, 0)),
                  pl.BlockSpec((dec_seq, sblk, D_MODEL), lambda i: (0, s_idx(i), 0), pipeline_mode=pl.Buffered(1)),
                  pl.BlockSpec((nh, sblk, C_CONV), lambda i: (0, s_idx(i), 0), pipeline_mode=pl.Buffered(1))]
        + w_specs + [_full(tri.shape, single=True)] + e_specs,
        out_specs=[
            pl.BlockSpec((TP * D_ROWS, LANES), lambda i: (i, 0)),
            pl.BlockSpec((TP * D_ROWS, LANES), lambda i: (i, 0)),
            pl.BlockSpec((1, SUBLANES, TP), lambda i: (i, 0, 0)),
            pl.BlockSpec((1, CLS_ROWS, LANES), lambda i: (i, 0, 0)),
            pl.BlockSpec((1, HIST, C_CONV), lambda i: (p_idx(i) // nj, 0, 0)),
            pl.BlockSpec((nh, sblk, C_CONV), lambda i: (0, s_idx(i), 0)),
            pl.BlockSpec((dec_seq, sblk, C_GMLP), lambda i: (0, s_idx(i), 0)),
        ] + e_specs,
        out_shape=[
            jax.ShapeDtypeStruct((n_tiles * TP * D_ROWS, LANES), F32),
            jax.ShapeDtypeStruct((n_tiles * TP * D_ROWS, LANES), F32),
            jax.ShapeDtypeStruct((n_tiles, SUBLANES, TP), I32),
            jax.ShapeDtypeStruct((n_tiles, CLS_ROWS, LANES), F32),
            jax.ShapeDtypeStruct((batch, HIST, C_CONV), F32),
            jax.ShapeDtypeStruct((nh, dec_batch, C_CONV), F32),
            jax.ShapeDtypeStruct((dec_seq, dec_batch, C_GMLP), F32),
        ] + e_shapes,
        scratch_shapes=[
            pltpu.VMEM((C_CONV // LANES, CONV_PITCH * (HIST + TP), LANES), F32),
            pltpu.VMEM((nh + dec_seq, sblk, C_CONV), F32),
            pltpu.VMEM((TP, D_MODEL), BF16),
            pltpu.VMEM((CLS_ROWS, LANES), F32),
            pltpu.VMEM((TP // SUBLANES, GROUP_ROWS, LANES), F32),
            pltpu.VMEM((TP // SUBLANES, GROUP_ROWS, LANES), F32),
            pltpu.VMEM((SUBLANES, TP), I32),
            pltpu.SMEM((1, TP), I32),
        ],
        compiler_params=pltpu.CompilerParams(
            dimension_semantics=("arbitrary",), vmem_limit_bytes=VMEM_LIMIT),
        name="mixer",
    )(x_prompt, x_tm, hist_tm, *weights, tri, *expert_w)


def _moe(tile_tabs, run_tabs, h1, w1, g_ffn, w_g, w_u, w_d):
    n_tiles = tile_tabs[0].shape[0]
    n_prefetch = len(tile_tabs) + len(run_tabs)
    grid_spec = pltpu.PrefetchScalarGridSpec(
        num_scalar_prefetch=n_prefetch,
        grid=(n_tiles,),
        in_specs=[
            pl.BlockSpec(memory_space=pl.ANY),
            pl.BlockSpec(memory_space=pl.ANY),
            pl.BlockSpec((1, D_MODEL), lambda i, *_: (0, 0)),
            pl.BlockSpec((EPG, D_MODEL, D_EXPERT), lambda i, grp, *_: (grp[i], 0, 0)),
            pl.BlockSpec((EPG, D_MODEL, D_EXPERT), lambda i, grp, *_: (grp[i], 0, 0)),
            pl.BlockSpec((EPG, D_EXPERT, D_MODEL), lambda i, grp, *_: (grp[i], 0, 0)),
        ],
        out_specs=pl.BlockSpec(memory_space=pl.ANY),
        scratch_shapes=[pltpu.VMEM((2, TM * D_ROWS, LANES), F32),
                        pltpu.VMEM((2, TM * D_ROWS, LANES), F32),
                        pltpu.VMEM((2, TM * D_ROWS, LANES), F32),
                        pltpu.SemaphoreType.DMA((2,)),
                        pltpu.SemaphoreType.DMA((2,))],
    )
    return pl.pallas_call(
        _moe_body,
        grid_spec=grid_spec,
        out_shape=jax.ShapeDtypeStruct(h1.shape, F32),
        compiler_params=pltpu.CompilerParams(
            dimension_semantics=("arbitrary",), vmem_limit_bytes=VMEM_LIMIT),
        name="moe",
    )(*tile_tabs, *run_tabs, h1, w1, g_ffn, w_g, w_u, w_d)


def _ple(lp, h2, tile0, p, g_ple, w_gate, w_proj, g_fin, name):
    n_tiles = p.shape[0] // TP
    return pl.pallas_call(
        _ple_body,
        grid=(n_tiles,),
        in_specs=[
            pl.BlockSpec((1, 1, TP), lambda i: (tile0 + i, 0, 0), memory_space=pltpu.SMEM),
            pl.BlockSpec((TP * D_ROWS, LANES), lambda i: (tile0 + i, 0)),
            pl.BlockSpec((TP, PLE_DIM), lambda i: (i, 0)),
            _full(g_ple.shape), _full(w_gate.shape), _full(w_proj.shape), _full(g_fin.shape),
        ],
        out_specs=pl.BlockSpec((TP, D_MODEL), lambda i: (i, 0)),
        out_shape=jax.ShapeDtypeStruct((n_tiles * TP, D_MODEL), F32),
        scratch_shapes=[pltpu.VMEM((TP // SUBLANES, GROUP_ROWS, LANES), F32)],
        compiler_params=pltpu.CompilerParams(
            dimension_semantics=("arbitrary",), vmem_limit_bytes=VMEM_LIMIT),
        name=name,
    )(lp, h2, p, g_ple, w_gate, w_proj, g_fin)


def kernel(x_prompt, x_sample, p_prompt, p_sample, state_conv, norm_mix_g, w_in, conv_w, conv_b, conv_ln_g, conv_ln_b, gmlp_ln_g, gmlp_ln_b, w_s, b_s, w_out, norm_ffn_g, w_router_group, b_router_group, w_router_expert, b_router_expert, w_exp_gate, w_exp_up, w_exp_down, norm_ple_g, w_ple_gate, w_ple_proj, norm_final_g):
    depth = w_in.shape[0]
    assert depth == 1, "single-layer pipeline"
    batch, seq, _ = x_prompt.shape
    dec_batch, dec_seq, _ = x_sample.shape
    assert seq % TP == 0 and TP % CHUNK == 0 and TP % dec_seq == 0 and dec_batch % (TP // dec_seq) == 0
    sblk = TP // dec_seq
    n_prompt = batch * seq
    n_sample = dec_batch * dec_seq
    n_tok = n_prompt + n_sample
    n_src_tiles = n_tok // TP

    row = lambda a: a.reshape(1, -1)
    w_in_b = w_in[0].astype(BF16)
    w_out_b = w_out[0].astype(BF16)
    cw = jnp.concatenate([conv_w[0], jnp.zeros((1, C_CONV), F32)], axis=0)
    tril = jnp.asarray(np.tril(np.ones((CHUNK, CHUNK), bool)))
    ws_m = jnp.where(tril[None], w_s[0], 0.0)
    ws_cat = jnp.concatenate([ws_m[0::2], ws_m[1::2]], axis=2).astype(BF16)
    bs_lane = jnp.repeat(jnp.transpose(b_s[0]), HEAD_DIM, axis=1)
    w_rt = jnp.zeros((CLS_ROWS, D_MODEL), F32)
    w_rt = w_rt.at[0:N_GROUPS].set(jnp.transpose(w_router_group[0]))
    w_rt = w_rt.at[N_GROUPS:N_GROUPS + N_EXPERTS].set(jnp.transpose(w_router_expert[0])).astype(BF16)
    b_r = jnp.zeros((CLS_ROWS, 1), F32)
    b_r = b_r.at[0:N_GROUPS, 0].set(b_router_group[0]).at[N_GROUPS:N_GROUPS + N_EXPERTS, 0].set(b_router_expert[0])
    tri = jnp.asarray(np.triu(np.ones((TP, TP), np.float32)), dtype=BF16)
    wsl = jnp.where(jnp.asarray(np.tril(np.ones((dec_seq, dec_seq), bool)))[None], w_s[0][:, :dec_seq, :dec_seq], 0.0)
    wsl = jnp.repeat(jnp.transpose(wsl, (1, 2, 0)).reshape(dec_seq * dec_seq, N_HEADS), HEAD_DIM, axis=1)
    bsl = jnp.repeat(jnp.transpose(b_s[0][:, :dec_seq]), HEAD_DIM, axis=1)

    weights = (row(norm_mix_g[0]), w_in_b, cw, row(conv_b[0]), row(conv_ln_g[0]), row(conv_ln_b[0]),
               row(gmlp_ln_g[0]), row(gmlp_ln_b[0]), ws_cat, bs_lane, wsl, bsl,
               w_out_b, row(norm_ffn_g[0]), w_rt, b_r)

    x_tm = jnp.transpose(x_sample, (1, 0, 2))
    hist_tm = jnp.transpose(state_conv[0], (1, 0, 2))
    h1, w1, lp, tab, cst_p, cst_s, v_s, w_g, w_u, w_d = _mixer(
        x_prompt, x_tm, hist_tm, weights, tri, (w_exp_gate[0], w_exp_up[0], w_exp_down[0]))

    rank0 = tab[:, :N_CLASSES, 0].astype(I32)
    cnt = tab[:, :N_CLASSES, 1].astype(I32)
    lsrc = tab[:, :N_CLASSES, 2].astype(I32) + (jnp.arange(n_src_tiles, dtype=I32) * TP)[:, None]
    total = rank0[-1] + cnt[-1]
    padded = ((total + TM - 1) // TM) * TM
    ends = jnp.cumsum(padded)
    offs = ends - padded
    n_tiles = (n_tok + N_CLASSES * (TM - 1)) // TM + 2
    tile_start = jnp.arange(n_tiles, dtype=I32) * TM
    n_used = ends[-1] // TM
    tile_cls = jnp.sum((tile_start[:, None] >= ends[None, :]).astype(I32), axis=1)
    tile_cls = jnp.where(tile_start < ends[-1], tile_cls, jnp.take(tile_cls, n_used - 1))
    tile_r0 = tile_start - jnp.take(offs, tile_cls)
    tile_nval = jnp.where(tile_start < ends[-1], jnp.clip(jnp.take(total, tile_cls) - tile_r0, 0, TM), 0).astype(I32)
    pair_lo = jnp.asarray(np.array([p[0] for p in _PAIRS], np.int32))
    pair_hi = jnp.asarray(np.array([p[1] for p in _PAIRS], np.int32))
    tile_lo = jnp.take(pair_lo, tile_cls % 6)
    tile_hi = jnp.take(pair_hi, tile_cls % 6)
    run_beg = jnp.take(rank0, tile_cls, axis=1)
    run_end = run_beg + jnp.take(cnt, tile_cls, axis=1)
    tile_jlo = jnp.sum((run_end <= tile_r0[None, :]).astype(I32), axis=0)
    tile_jhi = jnp.sum((run_beg < (tile_r0 + tile_nval)[None, :]).astype(I32), axis=0)
    tile_jhi = jnp.maximum(tile_jhi, tile_jlo)

    h2 = _moe((tile_cls // 6, tile_lo, tile_hi, tile_nval, tile_cls, tile_r0, tile_jlo, tile_jhi),
              (rank0.reshape(-1), cnt.reshape(-1), lsrc.reshape(-1)),
              h1, w1, row(norm_ffn_g[0]), w_g, w_u, w_d)

    lp = lp[:, 0:1, :]
    ple_w = (row(norm_ple_g[0]), w_ple_gate[0].astype(BF16), w_ple_proj[0].astype(BF16), row(norm_final_g))
    y_p = _ple(lp, h2, 0, p_prompt[0].reshape(n_prompt, PLE_DIM), *ple_w, name="ple_prompt")
    p_s_tm = jnp.transpose(p_sample[0].reshape(dec_batch // sblk, sblk, dec_seq, PLE_DIM), (0, 2, 1, 3))
    y_s = _ple(lp, h2, n_prompt // TP, p_s_tm.reshape(n_sample, PLE_DIM), *ple_w, name="ple_sample")

    y_prompt = y_p.reshape(batch, seq, D_MODEL)
    y_sample = jnp.transpose(y_s.reshape(dec_batch // sblk, dec_seq, sblk, D_MODEL), (0, 2, 1, 3))
    y_sample = y_sample.reshape(dec_batch, dec_seq, D_MODEL)
    state_conv_prompt = cst_p[:, HIST_OFF:, :][None]
    state_conv_sample = jnp.transpose(cst_s, (1, 0, 2))[None]
    state_gmlp_v_sample = jnp.transpose(v_s, (1, 0, 2))[None]
    return (y_prompt, y_sample, state_conv_prompt, state_conv_sample, state_gmlp_v_sample)
```

```python
import functools

import numpy as np
import jax
import jax.numpy as jnp
from jax import lax
from jax.experimental import pallas as pl
from jax.experimental.pallas import tpu as pltpu

F32 = jnp.float32
BF16 = jnp.bfloat16
I32 = jnp.int32

D_MODEL = 1024
C_CONV = 512
C_GMLP = 512
N_HEADS = 8
HEAD_DIM = 64
CONV_WIDTH = 31
CHUNK = 128
PLE_DIM = 256
N_GROUPS = 4
EPG = 4
N_EXPERTS = 16
D_EXPERT = 512
EPS = 1e-6
LANES = 128
SUBLANES = 8

N_CLASSES = N_GROUPS * 6
CLS_ROWS = 32
HIST = 32
HIST_OFF = HIST - (CONV_WIDTH - 1)
D_ROWS = D_MODEL // LANES
assert D_ROWS == SUBLANES
GROUP_ROWS = D_ROWS * SUBLANES

TP = 512
TM = 256
CONV_ROWS = 64
CONV_PITCH = 2
PIECES = tuple(TM >> b for b in range(TM.bit_length()))
LARGE_PIECE = 32

VMEM_LIMIT = 58 * 1024 * 1024

_PAIRS = ((0, 1), (0, 2), (0, 3), (1, 2), (1, 3), (2, 3))


def _rms(x, g):
    ms = jnp.mean(x * x, axis=-1, keepdims=True)
    return x * lax.rsqrt(ms + EPS) * g


def _ln(x, g, b):
    mu = jnp.mean(x, axis=-1, keepdims=True)
    xc = x - mu
    var = jnp.mean(xc * xc, axis=-1, keepdims=True)
    return xc * lax.rsqrt(var + EPS) * g + b


def _max4(v):
    return jnp.maximum(jnp.maximum(v[0], v[1]), jnp.maximum(v[2], v[3]))


def _first4(v, m):
    return jnp.where(v[0] == m, 0, jnp.where(v[1] == m, 1, jnp.where(v[2] == m, 2, 3))).astype(I32)


def _route(lt):
    gl = [lt[i:i + 1, :] for i in range(N_GROUPS)]
    g = _first4(gl, _max4(gl))
    a = []
    for j in range(EPG):
        rows = [lt[N_GROUPS + EPG * q + j:N_GROUPS + EPG * q + j + 1, :] for q in range(N_GROUPS)]
        a.append(jnp.where(g == 0, rows[0], jnp.where(g == 1, rows[1], jnp.where(g == 2, rows[2], rows[3]))))
    v1 = _max4(a)
    i1 = _first4(a, v1)
    a2 = [jnp.where(i1 == j, -jnp.inf, a[j]) for j in range(EPG)]
    i2 = _first4(a2, _max4(a2))
    lo = jnp.minimum(i1, i2)
    hi = jnp.maximum(i1, i2)
    pair = jnp.where(lo == 0, hi - 1, jnp.where(lo == 1, hi + 1, 5))
    return g * 6 + pair


def _combine_weights(logits, grp, e_lo, e_hi):
    lane = lax.broadcasted_iota(I32, logits.shape, 1)
    is_group = lane < N_GROUPS
    m = jnp.max(jnp.where(is_group, logits, -jnp.inf), axis=1, keepdims=True)
    den = jnp.sum(jnp.where(is_group, jnp.exp(logits - m), 0.0), axis=1, keepdims=True)
    base = N_GROUPS + EPG * grp
    v_lo = jnp.sum(jnp.where(lane == base + e_lo, logits, 0.0), axis=1, keepdims=True)
    v_hi = jnp.sum(jnp.where(lane == base + e_hi, logits, 0.0), axis=1, keepdims=True)
    top = jnp.maximum(v_lo, v_hi)
    p_lo = jnp.exp(v_lo - top)
    p_hi = jnp.exp(v_hi - top)
    scale = 1.0 / (den * (p_lo + p_hi))
    return p_lo * scale, p_hi * scale


def _load_token_blocks(ref, n_tok):
    return jnp.concatenate([ref[pl.ds(s, n_tok, stride=D_ROWS), :] for s in range(D_ROWS)], axis=1)


def _store_token_blocks(ref, x, n_tok):
    for s in range(D_ROWS):
        ref[pl.ds(s, n_tok, stride=D_ROWS), :] = x[:, s * LANES:(s + 1) * LANES]


def _router_tail(h, g_ffn_ref, w_rt_ref, b_r_ref, tri_ref, carry_ref, h1_ref, lp_ref, tab_ref,
                 stage_h, lp_vmem, lp_smem):
    t = TP
    c = _rms(h, g_ffn_ref[...]).astype(BF16)
    lt = lax.dot_general(w_rt_ref[...], c, (((1,), (1,)), ((), ())), preferred_element_type=F32)
    lt = lt + b_r_ref[...]
    cls = _route(lt)
    rows = lax.broadcasted_iota(I32, (CLS_ROWS, t), 0)
    ohf = jnp.where(rows == cls, 1.0, 0.0).astype(F32)
    pre = jnp.dot(ohf.astype(BF16), tri_ref[...], preferred_element_type=F32)
    tot = jnp.sum(ohf, axis=1, keepdims=True)
    lstart = jnp.sum(jnp.where(cls < rows, 1.0, 0.0).astype(F32), axis=1, keepdims=True)
    lpos = jnp.sum(ohf * (pre - 1.0 + lstart), axis=0, keepdims=True).astype(I32)
    carry = carry_ref[:, 0:1]
    lane = lax.broadcasted_iota(I32, (CLS_ROWS, LANES), 1)
    tab_ref[0] = jnp.where(lane == 0, carry, jnp.where(lane == 1, tot, jnp.where(lane == 2, lstart, 0.0)))
    carry_ref[...] = jnp.broadcast_to(carry + tot, (CLS_ROWS, LANES))
    lp_ref[0] = jnp.broadcast_to(lpos, (SUBLANES, t))
    lp_vmem[...] = jnp.broadcast_to(lpos, (SUBLANES, t))

    ng = t // SUBLANES
    for s in range(D_ROWS):
        stage_h[:, s * SUBLANES:(s + 1) * SUBLANES, :] = h[:, s * LANES:(s + 1) * LANES].reshape(ng, SUBLANES, LANES)
    pltpu.sync_copy(lp_vmem.at[pl.ds(0, 1)], lp_smem)

    def group(g, carry_):
        for k in range(SUBLANES):
            dst = pl.multiple_of(lp_smem[0, g * SUBLANES + k] * D_ROWS, D_ROWS)
            h1_ref[pl.ds(dst, D_ROWS), :] = stage_h[g, pl.ds(k, D_ROWS, stride=SUBLANES), :]
        return carry_

    lax.fori_loop(0, ng, group, 0)


def _mixer_front(x, g_mix_ref, w_in_ref):
    a = _rms(x, g_mix_ref[...]).astype(BF16)
    z = jnp.dot(a, w_in_ref[...], preferred_element_type=F32)
    glu = z[:, 0:C_CONV] * jax.nn.sigmoid(z[:, C_CONV:2 * C_CONV])
    u = jax.nn.gelu(z[:, 2 * C_CONV:2 * C_CONV + C_GMLP])
    gv = jax.nn.gelu(z[:, 2 * C_CONV + C_GMLP:])
    return glu, u, gv


def _mixer_body(xp_ref, xs_ref, hist_ref, g_mix_ref, w_in_ref, cw_ref, cb_ref, clg_ref, clb_ref, glg_ref, glb_ref,
                ws_ref, bs_ref, wsl_ref, bsl_ref, w_out_ref, g_ffn_ref, w_rt_ref, b_r_ref, tri_ref,
                eg_ref, eu_ref, ed_ref,
                h1_ref, lp_ref, tab_ref, cstp_ref, csts_ref, vs_ref, egb_ref, eub_ref, edb_ref,
                glu_scr, xs_scr, cat_scr, carry_ref, stage_h, lp_vmem, lp_smem,
                *, n_prompt_tiles, nj, dec_seq, sblk):
    i = pl.program_id(0)

    egb_ref[...] = eg_ref[...].astype(BF16)
    eub_ref[...] = eu_ref[...].astype(BF16)
    edb_ref[...] = ed_ref[...].astype(BF16)

    @pl.when(i == 0)
    def _():
        carry_ref[...] = jnp.zeros_like(carry_ref)

    def tail(x):
        h = x + jnp.dot(cat_scr[...], w_out_ref[...], preferred_element_type=F32)
        _router_tail(h, g_ffn_ref, w_rt_ref, b_r_ref, tri_ref, carry_ref, h1_ref, lp_ref, tab_ref,
                     stage_h, lp_vmem, lp_smem)

    @pl.when(i < n_prompt_tiles)
    def _prompt():
        j = lax.rem(i, nj)
        x = xp_ref[0]
        glu, u, gv = _mixer_front(x, g_mix_ref, w_in_ref)

        def time_rows(t0, n):
            return pl.ds(CONV_PITCH * t0, n, stride=CONV_PITCH)

        @pl.when(j == 0)
        def _():
            for s in range(C_CONV // LANES):
                glu_scr[s, time_rows(0, HIST), :] = jnp.zeros((HIST, LANES), F32)

        @pl.when(j > 0)
        def _():
            for s in range(C_CONV // LANES):
                glu_scr[s, time_rows(0, HIST), :] = glu_scr[s, time_rows(TP, HIST), :]

        for s in range(C_CONV // LANES):
            glu_scr[s, time_rows(HIST, TP), :] = glu[:, s * LANES:(s + 1) * LANES]
        cstp_ref[0] = jnp.concatenate([glu_scr[s, time_rows(TP, HIST), :] for s in range(C_CONV // LANES)], axis=1)

        cb = cb_ref[...]
        for r0 in range(0, TP, CONV_ROWS):
            slabs = []
            for s in range(C_CONV // LANES):
                acc = None
                for k in range(CONV_WIDTH):
                    term = (cw_ref[k:k + 1, s * LANES:(s + 1) * LANES]
                            * glu_scr[s, time_rows(r0 + HIST_OFF + k, CONV_ROWS), :])
                    acc = term if acc is None else acc + term
                slabs.append(acc)
            ya = jax.nn.silu(_ln(jnp.concatenate(slabs, axis=1) + cb, clg_ref[...], clb_ref[...]))
            cat_scr[r0:r0 + CONV_ROWS, 0:C_CONV] = ya.astype(BF16)

        v = _ln(gv, glg_ref[...], glb_ref[...])
        vb = v.astype(BF16)
        lane = lax.broadcasted_iota(I32, (CHUNK, LANES), 1)
        is_lo = lane < HEAD_DIM
        zero = jnp.zeros((CHUNK, LANES), BF16)
        for c in range(TP // CHUNK):
            for p in range(N_HEADS // 2):
                blk = vb[c * CHUNK:(c + 1) * CHUNK, p * LANES:(p + 1) * LANES]
                rhs = jnp.concatenate([jnp.where(is_lo, blk, zero), jnp.where(is_lo, zero, blk)], axis=0)
                mixed = jnp.dot(ws_ref[p], rhs, preferred_element_type=F32) + bs_ref[:, p * LANES:(p + 1) * LANES]
                yb = u[c * CHUNK:(c + 1) * CHUNK, p * LANES:(p + 1) * LANES] * mixed
                cat_scr[c * CHUNK:(c + 1) * CHUNK, C_CONV + p * LANES:C_CONV + (p + 1) * LANES] = yb.astype(BF16)
        tail(x)

    @pl.when(i >= n_prompt_tiles)
    def _sample():
        x = xs_ref[...].reshape(TP, D_MODEL)
        glu, u, gv = _mixer_front(x, g_mix_ref, w_in_ref)

        nh = CONV_WIDTH - 1
        xs_scr[0:nh] = hist_ref[...]
        xs_scr[nh:nh + dec_seq] = glu.reshape(dec_seq, sblk, C_CONV)
        csts_ref[...] = xs_scr[dec_seq:dec_seq + nh]

        cb = cb_ref[...]
        for t in range(dec_seq):
            acc = jnp.zeros((sblk, C_CONV), F32)
            for k in range(CONV_WIDTH):
                acc = acc + cw_ref[k:k + 1, :] * xs_scr[t + k]
            ya = jax.nn.silu(_ln(acc + cb, clg_ref[...], clb_ref[...]))
            cat_scr[t * sblk:(t + 1) * sblk, 0:C_CONV] = ya.astype(BF16)

        v = _ln(gv, glg_ref[...], glb_ref[...])
        vs_ref[...] = v.reshape(dec_seq, sblk, C_GMLP)
        for t in range(dec_seq):
            mixed = jnp.broadcast_to(bsl_ref[t:t + 1, :], (sblk, C_GMLP))
            for tp in range(t + 1):
                r = t * dec_seq + tp
                mixed = mixed + wsl_ref[r:r + 1, :] * v[tp * sblk:(tp + 1) * sblk, :]
            yb = u[t * sblk:(t + 1) * sblk, :] * mixed
            cat_scr[t * sblk:(t + 1) * sblk, C_CONV:] = yb.astype(BF16)
        tail(x)


def _for_pieces(ln, fn):
    def emit(sizes):
        for size in sizes:
            shift = size.bit_length()
            off = lax.shift_left(lax.shift_right_logical(ln, shift), shift)

            @pl.when((ln & size) != 0)
            def _(off=off, size=size):
                fn(off, size)

    large = tuple(s for s in PIECES if s >= LARGE_PIECE)

    @pl.when(ln >= LARGE_PIECE)
    def _():
        emit(large)

    emit(tuple(s for s in PIECES if s < LARGE_PIECE))


def _for_runs(k, tcls_ref, r0_ref, nval_ref, jlo_ref, jhi_ref, rank0_ref, cnt_ref, lsrc_ref, fn):
    c = tcls_ref[k]
    ra = r0_ref[k]
    rb = ra + nval_ref[k]

    def body(j, carry):
        idx = j * N_CLASSES + c
        s = rank0_ref[idx]
        lo = jnp.maximum(s, ra)
        hi = jnp.minimum(s + cnt_ref[idx], rb)
        ln = hi - lo

        @pl.when(ln > 0)
        def _():
            src = lsrc_ref[idx] + (lo - s)
            dst = lo - ra
            _for_pieces(ln, lambda off, size: fn(src + off, dst + off, size))

        return carry

    lax.fori_loop(jlo_ref[k], jhi_ref[k], body, 0)


def _blocks(ref, first, n):
    start = first * D_ROWS if isinstance(first, int) else pl.multiple_of(first * D_ROWS, D_ROWS)
    return ref.at[pl.ds(start, n * D_ROWS), :]


def _moe_body(grp_ref, lo_ref, hi_ref, nval_ref, tcls_ref, r0_ref, jlo_ref, jhi_ref, rank0_ref, cnt_ref, lsrc_ref,
              h1_hbm, g_ffn_ref, w_r_ref, b_r_ref, wg_ref, wu_ref, wd_ref,
              h2_hbm, hbuf, obuf, gsem, ssem):
    k = pl.program_id(0)
    n = pl.num_programs(0)
    slot = lax.rem(k, 2)
    tables = (tcls_ref, r0_ref, nval_ref, jlo_ref, jhi_ref, rank0_ref, cnt_ref, lsrc_ref)

    def gather_tile(tile, to_slot):
        def copy(src, dst, size):
            pltpu.make_async_copy(_blocks(h1_hbm, src, size), _blocks(hbuf.at[to_slot], dst, size),
                                  gsem.at[to_slot]).start()
        _for_runs(tile, *tables, copy)

    def wait_rows(n_rows, make_copy):
        _for_pieces(n_rows, lambda off, size: make_copy(size).wait())

    @pl.when(k == 0)
    def _():
        hbuf[...] = jnp.zeros_like(hbuf)
        gather_tile(0, 0)

    @pl.when(jnp.logical_and(k >= 2, nval_ref[jnp.maximum(k - 2, 0)] > 0))
    def _():
        wait_rows(nval_ref[jnp.maximum(k - 2, 0)],
                  lambda size: pltpu.make_async_copy(_blocks(obuf.at[slot], 0, size), _blocks(h2_hbm, 0, size),
                                                     ssem.at[slot]))

    nxt = jnp.minimum(k + 1, n - 1)

    @pl.when(jnp.logical_and(k + 1 < n, nval_ref[nxt] > 0))
    def _():
        gather_tile(nxt, 1 - slot)

    @pl.when(nval_ref[k] > 0)
    def _():
        wait_rows(nval_ref[k],
                  lambda size: pltpu.make_async_copy(_blocks(h1_hbm, 0, size), _blocks(hbuf.at[slot], 0, size),
                                                     gsem.at[slot]))
        h = _load_token_blocks(hbuf.at[slot], TM)
        c = _rms(h, g_ffn_ref[...]).astype(BF16)
        e_lo = lo_ref[k]
        e_hi = hi_ref[k]
        logits = jnp.dot(c, w_r_ref[...], preferred_element_type=F32) + b_r_ref[...]
        wl, wh = _combine_weights(logits, grp_ref[k], e_lo, e_hi)

        def hidden(e):
            gate = jnp.dot(c, wg_ref[e], preferred_element_type=F32)
            return (jax.nn.silu(gate) * jnp.dot(c, wu_ref[e], preferred_element_type=F32)).astype(BF16)

        hl = hidden(e_lo)
        hh = hidden(e_hi)
        yl = jnp.dot(hl, wd_ref[e_lo], preferred_element_type=F32)
        yh = jnp.dot(hh, wd_ref[e_hi], preferred_element_type=F32)
        _store_token_blocks(obuf.at[slot], h + (wl * yl + wh * yh), TM)

        def copy_back(src, dst, size):
            pltpu.make_async_copy(_blocks(obuf.at[slot], dst, size), _blocks(h2_hbm, src, size),
                                  ssem.at[slot]).start()
        _for_runs(k, *tables, copy_back)


def _ple_body(lp_ref, h2_ref, p_ref, g_ple_ref, w_gate_ref, w_proj_ref, g_fin_ref, y_ref, stage_h):
    ng = TP // SUBLANES

    def group(g, carry):
        for k in range(SUBLANES):
            src = pl.multiple_of(lp_ref[0, 0, g * SUBLANES + k] * D_ROWS, D_ROWS)
            stage_h[g, pl.ds(k, D_ROWS, stride=SUBLANES), :] = h2_ref[pl.ds(src, D_ROWS), :]
        return carry

    lax.fori_loop(0, ng, group, 0)
    h = jnp.concatenate([stage_h[:, s * SUBLANES:(s + 1) * SUBLANES, :].reshape(TP, LANES) for s in range(D_ROWS)],
                        axis=1)
    c = _rms(h, g_ple_ref[...]).astype(BF16)
    gate = jax.nn.sigmoid(jnp.dot(c, w_gate_ref[...], preferred_element_type=F32))
    proj = jnp.dot(p_ref[...].astype(BF16), w_proj_ref[...], preferred_element_type=F32)
    h = h + proj * gate
    y_ref[...] = _rms(h, g_fin_ref[...])


def _full(shape, single=False):
    nd = len(shape)
    if single:
        return pl.BlockSpec(shape, lambda *_: (0,) * nd, pipeline_mode=pl.Buffered(1))
    return pl.BlockSpec(shape, lambda *_: (0,) * nd)


def _mixer(x_prompt, x_tm, hist_tm, weights, tri, expert_w):
    batch, seq, _ = x_prompt.shape
    dec_seq, dec_batch, _ = x_tm.shape
    nj = seq // TP
    npt = batch * nj
    sblk = TP // dec_seq
    nst = dec_batch // sblk
    n_tiles = npt + nst
    nh = CONV_WIDTH - 1
    w_specs = [_full(w.shape, single=True) for w in weights]
    body = functools.partial(_mixer_body, n_prompt_tiles=npt, nj=nj, dec_seq=dec_seq, sblk=sblk)

    def p_idx(i):
        return jnp.minimum(i, npt - 1)

    def s_idx(i):
        return jnp.maximum(i - npt, 0)

    assert npt % N_EXPERTS == 0
    parts = npt // N_EXPERTS
    e_specs = [pl.BlockSpec((1, w.shape[1] // parts, w.shape[2]), lambda i: (p_idx(i) // parts, p_idx(i) % parts, 0))
               for w in expert_w]
    e_shapes = [jax.ShapeDtypeStruct(w.shape, BF16) for w in expert_w]

    return pl.pallas_call(
        body,
        grid=(n_tiles,),
        in_specs=[pl.BlockSpec((1, TP, D_MODEL), lambda i: (p_idx(i) // nj, p_idx(i) % nj, 0)),
                  pl.BlockSpec((dec_seq, sblk, D_MODEL), lambda i: (0, s_idx(i), 0), pipeline_mode=pl.Buffered(1)),
                  pl.BlockSpec((nh, sblk, C_CONV), lambda i: (0, s_idx(i), 0), pipeline_mode=pl.Buffered(1))]
        + w_specs + [_full(tri.shape, single=True)] + e_specs,
        out_specs=[
            pl.BlockSpec((TP * D_ROWS, LANES), lambda i: (i, 0)),
            pl.BlockSpec((1, SUBLANES, TP), lambda i: (i, 0, 0)),
            pl.BlockSpec((1, CLS_ROWS, LANES), lambda i: (i, 0, 0)),
            pl.BlockSpec((1, HIST, C_CONV), lambda i: (p_idx(i) // nj, 0, 0)),
            pl.BlockSpec((nh, sblk, C_CONV), lambda i: (0, s_idx(i), 0)),
            pl.BlockSpec((dec_seq, sblk, C_GMLP), lambda i: (0, s_idx(i), 0)),
        ] + e_specs,
        out_shape=[
            jax.ShapeDtypeStruct((n_tiles * TP * D_ROWS, LANES), F32),
            jax.ShapeDtypeStruct((n_tiles, SUBLANES, TP), I32),
            jax.ShapeDtypeStruct((n_tiles, CLS_ROWS, LANES), F32),
            jax.ShapeDtypeStruct((batch, HIST, C_CONV), F32),
            jax.ShapeDtypeStruct((nh, dec_batch, C_CONV), F32),
            jax.ShapeDtypeStruct((dec_seq, dec_batch, C_GMLP), F32),
        ] + e_shapes,
        scratch_shapes=[
            pltpu.VMEM((C_CONV // LANES, CONV_PITCH * (HIST + TP), LANES), F32),
            pltpu.VMEM((nh + dec_seq, sblk, C_CONV), F32),
            pltpu.VMEM((TP, D_MODEL), BF16),
            pltpu.VMEM((CLS_ROWS, LANES), F32),
            pltpu.VMEM((TP // SUBLANES, GROUP_ROWS, LANES), F32),
            pltpu.VMEM((SUBLANES, TP), I32),
            pltpu.SMEM((1, TP), I32),
        ],
        compiler_params=pltpu.CompilerParams(
            dimension_semantics=("arbitrary",), vmem_limit_bytes=VMEM_LIMIT),
        name="mixer",
    )(x_prompt, x_tm, hist_tm, *weights, tri, *expert_w)


def _moe(tile_tabs, run_tabs, h1, g_ffn, w_r, b_r, w_g, w_u, w_d):
    n_tiles = tile_tabs[0].shape[0]
    n_prefetch = len(tile_tabs) + len(run_tabs)
    grid_spec = pltpu.PrefetchScalarGridSpec(
        num_scalar_prefetch=n_prefetch,
        grid=(n_tiles,),
        in_specs=[
            pl.BlockSpec(memory_space=pl.ANY),
            pl.BlockSpec((1, D_MODEL), lambda i, *_: (0, 0)),
            pl.BlockSpec((D_MODEL, LANES), lambda i, *_: (0, 0)),
            pl.BlockSpec((1, LANES), lambda i, *_: (0, 0)),
            pl.BlockSpec((EPG, D_MODEL, D_EXPERT), lambda i, grp, *_: (grp[i], 0, 0)),
            pl.BlockSpec((EPG, D_MODEL, D_EXPERT), lambda i, grp, *_: (grp[i], 0, 0)),
            pl.BlockSpec((EPG, D_EXPERT, D_MODEL), lambda i, grp, *_: (grp[i], 0, 0)),
        ],
        out_specs=pl.BlockSpec(memory_space=pl.ANY),
        scratch_shapes=[pltpu.VMEM((2, TM * D_ROWS, LANES), F32),
                        pltpu.VMEM((2, TM * D_ROWS, LANES), F32),
                        pltpu.SemaphoreType.DMA((2,)),
                        pltpu.SemaphoreType.DMA((2,))],
    )
    return pl.pallas_call(
        _moe_body,
        grid_spec=grid_spec,
        out_shape=jax.ShapeDtypeStruct(h1.shape, F32),
        compiler_params=pltpu.CompilerParams(
            dimension_semantics=("arbitrary",), vmem_limit_bytes=VMEM_LIMIT),
        name="moe",
    )(*tile_tabs, *run_tabs, h1, g_ffn, w_r, b_r, w_g, w_u, w_d)


def _ple(lp, h2, tile0, p, g_ple, w_gate, w_proj, g_fin, name):
    n_tiles = p.shape[0] // TP
    return pl.pallas_call(
        _ple_body,
        grid=(n_tiles,),
        in_specs=[
            pl.BlockSpec((1, 1, TP), lambda i: (tile0 + i, 0, 0), memory_space=pltpu.SMEM),
            pl.BlockSpec((TP * D_ROWS, LANES), lambda i: (tile0 + i, 0)),
            pl.BlockSpec((TP, PLE_DIM), lambda i: (i, 0)),
            _full(g_ple.shape), _full(w_gate.shape), _full(w_proj.shape), _full(g_fin.shape),
        ],
        out_specs=pl.BlockSpec((TP, D_MODEL), lambda i: (i, 0)),
        out_shape=jax.ShapeDtypeStruct((n_tiles * TP, D_MODEL), F32),
        scratch_shapes=[pltpu.VMEM((TP // SUBLANES, GROUP_ROWS, LANES), F32)],
        compiler_params=pltpu.CompilerParams(
            dimension_semantics=("arbitrary",), vmem_limit_bytes=VMEM_LIMIT),
        name=name,
    )(lp, h2, p, g_ple, w_gate, w_proj, g_fin)


def kernel(x_prompt, x_sample, p_prompt, p_sample, state_conv, norm_mix_g, w_in, conv_w, conv_b, conv_ln_g, conv_ln_b, gmlp_ln_g, gmlp_ln_b, w_s, b_s, w_out, norm_ffn_g, w_router_group, b_router_group, w_router_expert, b_router_expert, w_exp_gate, w_exp_up, w_exp_down, norm_ple_g, w_ple_gate, w_ple_proj, norm_final_g):
    depth = w_in.shape[0]
    assert depth == 1, "single-layer pipeline"
    batch, seq, _ = x_prompt.shape
    dec_batch, dec_seq, _ = x_sample.shape
    assert seq % TP == 0 and TP % CHUNK == 0 and TP % dec_seq == 0 and dec_batch % (TP // dec_seq) == 0
    sblk = TP // dec_seq
    n_prompt = batch * seq
    n_sample = dec_batch * dec_seq
    n_tok = n_prompt + n_sample
    n_src_tiles = n_tok // TP

    row = lambda a: a.reshape(1, -1)
    w_in_b = w_in[0].astype(BF16)
    w_out_b = w_out[0].astype(BF16)
    cw = jnp.concatenate([conv_w[0], jnp.zeros((1, C_CONV), F32)], axis=0)
    tril = jnp.asarray(np.tril(np.ones((CHUNK, CHUNK), bool)))
    ws_m = jnp.where(tril[None], w_s[0], 0.0)
    ws_cat = jnp.concatenate([ws_m[0::2], ws_m[1::2]], axis=2).astype(BF16)
    bs_lane = jnp.repeat(jnp.transpose(b_s[0]), HEAD_DIM, axis=1)
    w_rt = jnp.zeros((CLS_ROWS, D_MODEL), F32)
    w_rt = w_rt.at[0:N_GROUPS].set(jnp.transpose(w_router_group[0]))
    w_rt = w_rt.at[N_GROUPS:N_GROUPS + N_EXPERTS].set(jnp.transpose(w_router_expert[0])).astype(BF16)
    b_r = jnp.zeros((CLS_ROWS, 1), F32)
    b_r = b_r.at[0:N_GROUPS, 0].set(b_router_group[0]).at[N_GROUPS:N_GROUPS + N_EXPERTS, 0].set(b_router_expert[0])
    n_logit = N_GROUPS + N_EXPERTS
    w_rn = jnp.concatenate([w_router_group[0], w_router_expert[0], jnp.zeros((D_MODEL, LANES - n_logit), F32)],
                           axis=1).astype(BF16)
    b_rn = jnp.concatenate([b_router_group[0], b_router_expert[0], jnp.zeros((LANES - n_logit,), F32)]).reshape(1, LANES)
    tri = jnp.asarray(np.triu(np.ones((TP, TP), np.float32)), dtype=BF16)
    wsl = jnp.where(jnp.asarray(np.tril(np.ones((dec_seq, dec_seq), bool)))[None], w_s[0][:, :dec_seq, :dec_seq], 0.0)
    wsl = jnp.repeat(jnp.transpose(wsl, (1, 2, 0)).reshape(dec_seq * dec_seq, N_HEADS), HEAD_DIM, axis=1)
    bsl = jnp.repeat(jnp.transpose(b_s[0][:, :dec_seq]), HEAD_DIM, axis=1)

    weights = (row(norm_mix_g[0]), w_in_b, cw, row(conv_b[0]), row(conv_ln_g[0]), row(conv_ln_b[0]),
               row(gmlp_ln_g[0]), row(gmlp_ln_b[0]), ws_cat, bs_lane, wsl, bsl,
               w_out_b, row(norm_ffn_g[0]), w_rt, b_r)

    x_tm = jnp.transpose(x_sample, (1, 0, 2))
    hist_tm = jnp.transpose(state_conv[0], (1, 0, 2))
    h1, lp, tab, cst_p, cst_s, v_s, w_g, w_u, w_d = _mixer(
        x_prompt, x_tm, hist_tm, weights, tri, (w_exp_gate[0], w_exp_up[0], w_exp_down[0]))

    rank0 = tab[:, :N_CLASSES, 0].astype(I32)
    cnt = tab[:, :N_CLASSES, 1].astype(I32)
    lsrc = tab[:, :N_CLASSES, 2].astype(I32) + (jnp.arange(n_src_tiles, dtype=I32) * TP)[:, None]
    total = rank0[-1] + cnt[-1]
    padded = ((total + TM - 1) // TM) * TM
    ends = jnp.cumsum(padded)
    offs = ends - padded
    n_tiles = (n_tok + N_CLASSES * (TM - 1)) // TM + 2
    tile_start = jnp.arange(n_tiles, dtype=I32) * TM
    n_used = ends[-1] // TM
    tile_cls = jnp.sum((tile_start[:, None] >= ends[None, :]).astype(I32), axis=1)
    tile_cls = jnp.where(tile_start < ends[-1], tile_cls, jnp.take(tile_cls, n_used - 1))
    tile_r0 = tile_start - jnp.take(offs, tile_cls)
    tile_nval = jnp.where(tile_start < ends[-1], jnp.clip(jnp.take(total, tile_cls) - tile_r0, 0, TM), 0).astype(I32)
    pair_lo = jnp.asarray(np.array([p[0] for p in _PAIRS], np.int32))
    pair_hi = jnp.asarray(np.array([p[1] for p in _PAIRS], np.int32))
    tile_lo = jnp.take(pair_lo, tile_cls % 6)
    tile_hi = jnp.take(pair_hi, tile_cls % 6)
    run_beg = jnp.take(rank0, tile_cls, axis=1)
    run_end = run_beg + jnp.take(cnt, tile_cls, axis=1)
    tile_jlo = jnp.sum((run_end <= tile_r0[None, :]).astype(I32), axis=0)
    tile_jhi = jnp.sum((run_beg < (tile_r0 + tile_nval)[None, :]).astype(I32), axis=0)
    tile_jhi = jnp.maximum(tile_jhi, tile_jlo)

    h2 = _moe((tile_cls // 6, tile_lo, tile_hi, tile_nval, tile_cls, tile_r0, tile_jlo, tile_jhi),
              (rank0.reshape(-1), cnt.reshape(-1), lsrc.reshape(-1)),
              h1, row(norm_ffn_g[0]), w_rn, b_rn, w_g, w_u, w_d)

    lp = lp[:, 0:1, :]
    ple_w = (row(norm_ple_g[0]), w_ple_gate[0].astype(BF16), w_ple_proj[0].astype(BF16), row(norm_final_g))
    y_p = _ple(lp, h2, 0, p_prompt[0].reshape(n_prompt, PLE_DIM), *ple_w, name="ple_prompt")
    p_s_tm = jnp.transpose(p_sample[0].reshape(dec_batch // sblk, sblk, dec_seq, PLE_DIM), (0, 2, 1, 3))
    y_s = _ple(lp, h2, n_prompt // TP, p_s_tm.reshape(n_sample, PLE_DIM), *ple_w, name="ple_sample")

    y_prompt = y_p.reshape(batch, seq, D_MODEL)
    y_sample = jnp.transpose(y_s.reshape(dec_batch // sblk, dec_seq, sblk, D_MODEL), (0, 2, 1, 3))
    y_sample = y_sample.reshape(dec_batch, dec_seq, D_MODEL)
    state_conv_prompt = cst_p[:, HIST_OFF:, :][None]
    state_conv_sample = jnp.transpose(cst_s, (1, 0, 2))[None]
    state_gmlp_v_sample = jnp.transpose(v_s, (1, 0, 2))[None]
    return (y_prompt, y_sample, state_conv_prompt, state_conv_sample, state_gmlp_v_sample)
```

```python
import functools

import numpy as np
import jax
import jax.numpy as jnp
from jax import lax
from jax.experimental import pallas as pl
from jax.experimental.pallas import tpu as pltpu

F32 = jnp.float32
BF16 = jnp.bfloat16
I32 = jnp.int32

D_MODEL = 1024
C_CONV = 512
C_GMLP = 512
N_HEADS = 8
HEAD_DIM = 64
CONV_WIDTH = 31
CHUNK = 128
PLE_DIM = 256
N_GROUPS = 4
EPG = 4
N_EXPERTS = 16
D_EXPERT = 512
EPS = 1e-6
LANES = 128
SUBLANES = 8

N_CLASSES = N_GROUPS * 6
CLS_ROWS = 32
HIST = 32
HIST_OFF = HIST - (CONV_WIDTH - 1)
D_ROWS = D_MODEL // LANES
assert D_ROWS == SUBLANES
GROUP_ROWS = D_ROWS * SUBLANES

TP = 512
TM = 256
CONV_ROWS = 64
CONV_PITCH = 2
PIECES = tuple(TM >> b for b in range(TM.bit_length()))
LARGE_PIECE = 32

VMEM_LIMIT = 58 * 1024 * 1024

_PAIRS = ((0, 1), (0, 2), (0, 3), (1, 2), (1, 3), (2, 3))


def _rms(x, g):
    ms = jnp.mean(x * x, axis=-1, keepdims=True)
    return x * lax.rsqrt(ms + EPS) * g


def _ln(x, g, b):
    mu = jnp.mean(x, axis=-1, keepdims=True)
    xc = x - mu
    var = jnp.mean(xc * xc, axis=-1, keepdims=True)
    return xc * lax.rsqrt(var + EPS) * g + b


def _max4(v):
    return jnp.maximum(jnp.maximum(v[0], v[1]), jnp.maximum(v[2], v[3]))


def _first4(v, m):
    return jnp.where(v[0] == m, 0, jnp.where(v[1] == m, 1, jnp.where(v[2] == m, 2, 3))).astype(I32)


def _route(lt):
    gl = [lt[i:i + 1, :] for i in range(N_GROUPS)]
    g = _first4(gl, _max4(gl))
    a = []
    for j in range(EPG):
        rows = [lt[N_GROUPS + EPG * q + j:N_GROUPS + EPG * q + j + 1, :] for q in range(N_GROUPS)]
        a.append(jnp.where(g == 0, rows[0], jnp.where(g == 1, rows[1], jnp.where(g == 2, rows[2], rows[3]))))
    v1 = _max4(a)
    i1 = _first4(a, v1)
    a2 = [jnp.where(i1 == j, -jnp.inf, a[j]) for j in range(EPG)]
    i2 = _first4(a2, _max4(a2))
    lo = jnp.minimum(i1, i2)
    hi = jnp.maximum(i1, i2)
    pair = jnp.where(lo == 0, hi - 1, jnp.where(lo == 1, hi + 1, 5))
    return g * 6 + pair


def _combine_weights(logits, grp, e_lo, e_hi):
    lane = lax.broadcasted_iota(I32, logits.shape, 1)
    is_group = lane < N_GROUPS
    m = jnp.max(jnp.where(is_group, logits, -jnp.inf), axis=1, keepdims=True)
    den = jnp.sum(jnp.where(is_group, jnp.exp(logits - m), 0.0), axis=1, keepdims=True)
    base = N_GROUPS + EPG * grp
    v_lo = jnp.sum(jnp.where(lane == base + e_lo, logits, 0.0), axis=1, keepdims=True)
    v_hi = jnp.sum(jnp.where(lane == base + e_hi, logits, 0.0), axis=1, keepdims=True)
    top = jnp.maximum(v_lo, v_hi)
    p_lo = jnp.exp(v_lo - top)
    p_hi = jnp.exp(v_hi - top)
    scale = 1.0 / (den * (p_lo + p_hi))
    return p_lo * scale, p_hi * scale


def _load_token_blocks(ref, n_tok):
    return jnp.concatenate([ref[pl.ds(s, n_tok, stride=D_ROWS), :] for s in range(D_ROWS)], axis=1)


def _store_token_blocks(ref, x, n_tok):
    for s in range(D_ROWS):
        ref[pl.ds(s, n_tok, stride=D_ROWS), :] = x[:, s * LANES:(s + 1) * LANES]


def _router_tail(h, g_ffn_ref, w_rt_ref, b_r_ref, tri_ref, carry_ref, h1_ref, lp_ref, tab_ref,
                 stage_h, lp_vmem, lp_smem):
    t = TP
    c = _rms(h, g_ffn_ref[...]).astype(BF16)
    lt = lax.dot_general(w_rt_ref[...], c, (((1,), (1,)), ((), ())), preferred_element_type=F32)
    lt = lt + b_r_ref[...]
    cls = _route(lt)
    rows = lax.broadcasted_iota(I32, (CLS_ROWS, t), 0)
    ohf = jnp.where(rows == cls, 1.0, 0.0).astype(F32)
    pre = jnp.dot(ohf.astype(BF16), tri_ref[...], preferred_element_type=F32)
    tot = jnp.sum(ohf, axis=1, keepdims=True)
    lstart = jnp.sum(jnp.where(cls < rows, 1.0, 0.0).astype(F32), axis=1, keepdims=True)
    lpos = jnp.sum(ohf * (pre - 1.0 + lstart), axis=0, keepdims=True).astype(I32)
    carry = carry_ref[:, 0:1]
    lane = lax.broadcasted_iota(I32, (CLS_ROWS, LANES), 1)
    tab_ref[0] = jnp.where(lane == 0, carry, jnp.where(lane == 1, tot, jnp.where(lane == 2, lstart, 0.0)))
    carry_ref[...] = jnp.broadcast_to(carry + tot, (CLS_ROWS, LANES))
    lp_ref[0] = jnp.broadcast_to(lpos, (SUBLANES, t))
    lp_vmem[...] = jnp.broadcast_to(lpos, (SUBLANES, t))

    ng = t // SUBLANES
    for s in range(D_ROWS):
        stage_h[:, s * SUBLANES:(s + 1) * SUBLANES, :] = h[:, s * LANES:(s + 1) * LANES].reshape(ng, SUBLANES, LANES)
    pltpu.sync_copy(lp_vmem.at[pl.ds(0, 1)], lp_smem)

    def group(g, carry_):
        for k in range(SUBLANES):
            dst = pl.multiple_of(lp_smem[0, g * SUBLANES + k] * D_ROWS, D_ROWS)
            h1_ref[pl.ds(dst, D_ROWS), :] = stage_h[g, pl.ds(k, D_ROWS, stride=SUBLANES), :]
        return carry_

    lax.fori_loop(0, ng, group, 0, unroll=2)


def _mixer_front(x, g_mix_ref, w_in_ref):
    a = _rms(x, g_mix_ref[...]).astype(BF16)
    z = jnp.dot(a, w_in_ref[...], preferred_element_type=F32)
    glu = z[:, 0:C_CONV] * jax.nn.sigmoid(z[:, C_CONV:2 * C_CONV])
    u = jax.nn.gelu(z[:, 2 * C_CONV:2 * C_CONV + C_GMLP])
    gv = jax.nn.gelu(z[:, 2 * C_CONV + C_GMLP:])
    return glu, u, gv


def _mixer_body(xp_ref, xs_ref, hist_ref, g_mix_ref, w_in_ref, cw_ref, cb_ref, clg_ref, clb_ref, glg_ref, glb_ref,
                ws_ref, bs_ref, wsl_ref, bsl_ref, w_out_ref, g_ffn_ref, w_rt_ref, b_r_ref, tri_ref,
                eg_ref, eu_ref, ed_ref,
                h1_ref, lp_ref, tab_ref, cstp_ref, csts_ref, vs_ref, egb_ref, eub_ref, edb_ref,
                glu_scr, xs_scr, cat_scr, carry_ref, stage_h, lp_vmem, lp_smem,
                *, n_prompt_tiles, nj, dec_seq, sblk):
    i = pl.program_id(0)

    egb_ref[...] = eg_ref[...].astype(BF16)
    eub_ref[...] = eu_ref[...].astype(BF16)
    edb_ref[...] = ed_ref[...].astype(BF16)

    @pl.when(i == 0)
    def _():
        carry_ref[...] = jnp.zeros_like(carry_ref)

    def tail(x):
        h = x + jnp.dot(cat_scr[...], w_out_ref[...], preferred_element_type=F32)
        _router_tail(h, g_ffn_ref, w_rt_ref, b_r_ref, tri_ref, carry_ref, h1_ref, lp_ref, tab_ref,
                     stage_h, lp_vmem, lp_smem)

    @pl.when(i < n_prompt_tiles)
    def _prompt():
        j = lax.rem(i, nj)
        x = xp_ref[0]
        glu, u, gv = _mixer_front(x, g_mix_ref, w_in_ref)

        def time_rows(t0, n):
            return pl.ds(CONV_PITCH * t0, n, stride=CONV_PITCH)

        @pl.when(j == 0)
        def _():
            for s in range(C_CONV // LANES):
                glu_scr[s, time_rows(0, HIST), :] = jnp.zeros((HIST, LANES), F32)

        @pl.when(j > 0)
        def _():
            for s in range(C_CONV // LANES):
                glu_scr[s, time_rows(0, HIST), :] = glu_scr[s, time_rows(TP, HIST), :]

        for s in range(C_CONV // LANES):
            glu_scr[s, time_rows(HIST, TP), :] = glu[:, s * LANES:(s + 1) * LANES]
        cstp_ref[0] = jnp.concatenate([glu_scr[s, time_rows(TP, HIST), :] for s in range(C_CONV // LANES)], axis=1)

        cb = cb_ref[...]
        for r0 in range(0, TP, CONV_ROWS):
            slabs = []
            for s in range(C_CONV // LANES):
                acc = None
                for k in range(CONV_WIDTH):
                    term = (cw_ref[k:k + 1, s * LANES:(s + 1) * LANES]
                            * glu_scr[s, time_rows(r0 + HIST_OFF + k, CONV_ROWS), :])
                    acc = term if acc is None else acc + term
                slabs.append(acc)
            ya = jax.nn.silu(_ln(jnp.concatenate(slabs, axis=1) + cb, clg_ref[...], clb_ref[...]))
            cat_scr[r0:r0 + CONV_ROWS, 0:C_CONV] = ya.astype(BF16)

        v = _ln(gv, glg_ref[...], glb_ref[...])
        vb = v.astype(BF16)
        lane = lax.broadcasted_iota(I32, (CHUNK, LANES), 1)
        is_lo = lane < HEAD_DIM
        zero = jnp.zeros((CHUNK, LANES), BF16)
        for c in range(TP // CHUNK):
            for p in range(N_HEADS // 2):
                blk = vb[c * CHUNK:(c + 1) * CHUNK, p * LANES:(p + 1) * LANES]
                rhs = jnp.concatenate([jnp.where(is_lo, blk, zero), jnp.where(is_lo, zero, blk)], axis=0)
                mixed = jnp.dot(ws_ref[p], rhs, preferred_element_type=F32) + bs_ref[:, p * LANES:(p + 1) * LANES]
                yb = u[c * CHUNK:(c + 1) * CHUNK, p * LANES:(p + 1) * LANES] * mixed
                cat_scr[c * CHUNK:(c + 1) * CHUNK, C_CONV + p * LANES:C_CONV + (p + 1) * LANES] = yb.astype(BF16)
        tail(x)

    @pl.when(i >= n_prompt_tiles)
    def _sample():
        x = xs_ref[...].reshape(TP, D_MODEL)
        glu, u, gv = _mixer_front(x, g_mix_ref, w_in_ref)

        nh = CONV_WIDTH - 1
        xs_scr[0:nh] = hist_ref[...]
        xs_scr[nh:nh + dec_seq] = glu.reshape(dec_seq, sblk, C_CONV)
        csts_ref[...] = xs_scr[dec_seq:dec_seq + nh]

        cb = cb_ref[...]
        for t in range(dec_seq):
            acc = jnp.zeros((sblk, C_CONV), F32)
            for k in range(CONV_WIDTH):
                acc = acc + cw_ref[k:k + 1, :] * xs_scr[t + k]
            ya = jax.nn.silu(_ln(acc + cb, clg_ref[...], clb_ref[...]))
            cat_scr[t * sblk:(t + 1) * sblk, 0:C_CONV] = ya.astype(BF16)

        v = _ln(gv, glg_ref[...], glb_ref[...])
        vs_ref[...] = v.reshape(dec_seq, sblk, C_GMLP)
        for t in range(dec_seq):
            mixed = jnp.broadcast_to(bsl_ref[t:t + 1, :], (sblk, C_GMLP))
            for tp in range(t + 1):
                r = t * dec_seq + tp
                mixed = mixed + wsl_ref[r:r + 1, :] * v[tp * sblk:(tp + 1) * sblk, :]
            yb = u[t * sblk:(t + 1) * sblk, :] * mixed
            cat_scr[t * sblk:(t + 1) * sblk, C_CONV:] = yb.astype(BF16)
        tail(x)


def _for_pieces(ln, fn):
    def emit(sizes):
        for size in sizes:
            shift = size.bit_length()
            off = lax.shift_left(lax.shift_right_logical(ln, shift), shift)

            @pl.when((ln & size) != 0)
            def _(off=off, size=size):
                fn(off, size)

    large = tuple(s for s in PIECES if s >= LARGE_PIECE)

    @pl.when(ln >= LARGE_PIECE)
    def _():
        emit(large)

    emit(tuple(s for s in PIECES if s < LARGE_PIECE))


def _for_runs(k, tcls_ref, r0_ref, nval_ref, jlo_ref, jhi_ref, rank0_ref, cnt_ref, lsrc_ref, fn):
    c = tcls_ref[k]
    ra = r0_ref[k]
    rb = ra + nval_ref[k]

    def body(j, carry):
        idx = j * N_CLASSES + c
        s = rank0_ref[idx]
        lo = jnp.maximum(s, ra)
        hi = jnp.minimum(s + cnt_ref[idx], rb)
        ln = hi - lo

        @pl.when(ln > 0)
        def _():
            src = lsrc_ref[idx] + (lo - s)
            dst = lo - ra
            _for_pieces(ln, lambda off, size: fn(src + off, dst + off, size))

        return carry

    lax.fori_loop(jlo_ref[k], jhi_ref[k], body, 0)


def _blocks(ref, first, n):
    start = first * D_ROWS if isinstance(first, int) else pl.multiple_of(first * D_ROWS, D_ROWS)
    return ref.at[pl.ds(start, n * D_ROWS), :]


def _moe_body(grp_ref, lo_ref, hi_ref, nval_ref, tcls_ref, r0_ref, jlo_ref, jhi_ref, rank0_ref, cnt_ref, lsrc_ref,
              h1_hbm, g_ffn_ref, w_r_ref, b_r_ref, wg_ref, wu_ref, wd_ref,
              h2_hbm, hbuf, obuf, gsem, ssem):
    k = pl.program_id(0)
    n = pl.num_programs(0)
    slot = lax.rem(k, 2)
    tables = (tcls_ref, r0_ref, nval_ref, jlo_ref, jhi_ref, rank0_ref, cnt_ref, lsrc_ref)

    def gather_tile(tile, to_slot):
        def copy(src, dst, size):
            pltpu.make_async_copy(_blocks(h1_hbm, src, size), _blocks(hbuf.at[to_slot], dst, size),
                                  gsem.at[to_slot]).start()
        _for_runs(tile, *tables, copy)

    def wait_rows(n_rows, make_copy):
        _for_pieces(n_rows, lambda off, size: make_copy(size).wait())

    @pl.when(k == 0)
    def _():
        hbuf[...] = jnp.zeros_like(hbuf)
        gather_tile(0, 0)

    @pl.when(jnp.logical_and(k >= 2, nval_ref[jnp.maximum(k - 2, 0)] > 0))
    def _():
        wait_rows(nval_ref[jnp.maximum(k - 2, 0)],
                  lambda size: pltpu.make_async_copy(_blocks(obuf.at[slot], 0, size), _blocks(h2_hbm, 0, size),
                                                     ssem.at[slot]))

    nxt = jnp.minimum(k + 1, n - 1)

    @pl.when(jnp.logical_and(k + 1 < n, nval_ref[nxt] > 0))
    def _():
        gather_tile(nxt, 1 - slot)

    @pl.when(nval_ref[k] > 0)
    def _():
        wait_rows(nval_ref[k],
                  lambda size: pltpu.make_async_copy(_blocks(h1_hbm, 0, size), _blocks(hbuf.at[slot], 0, size),
                                                     gsem.at[slot]))
        e_lo = lo_ref[k]
        e_hi = hi_ref[k]

        def experts(m):
            h = _load_token_blocks(hbuf.at[slot], m)
            c = _rms(h, g_ffn_ref[...]).astype(BF16)
            logits = jnp.dot(c, w_r_ref[...], preferred_element_type=F32) + b_r_ref[...]
            wl, wh = _combine_weights(logits, grp_ref[k], e_lo, e_hi)

            def hidden(e):
                gate = jnp.dot(c, wg_ref[e], preferred_element_type=F32)
                return (jax.nn.silu(gate) * jnp.dot(c, wu_ref[e], preferred_element_type=F32)).astype(BF16)

            yl = jnp.dot(hidden(e_lo), wd_ref[e_lo], preferred_element_type=F32)
            yh = jnp.dot(hidden(e_hi), wd_ref[e_hi], preferred_element_type=F32)
            _store_token_blocks(obuf.at[slot], h + (wl * yl + wh * yh), m)

        @pl.when(nval_ref[k] > TM // 2)
        def _():
            experts(TM)

        @pl.when(nval_ref[k] <= TM // 2)
        def _():
            experts(TM // 2)

        def copy_back(src, dst, size):
            pltpu.make_async_copy(_blocks(obuf.at[slot], dst, size), _blocks(h2_hbm, src, size),
                                  ssem.at[slot]).start()
        _for_runs(k, *tables, copy_back)


def _ple_body(lp_ref, h2_ref, p_ref, g_ple_ref, w_gate_ref, w_proj_ref, g_fin_ref, y_ref, stage_h):
    ng = TP // SUBLANES

    def group(g, carry):
        for k in range(SUBLANES):
            src = pl.multiple_of(lp_ref[0, 0, g * SUBLANES + k] * D_ROWS, D_ROWS)
            stage_h[g, pl.ds(k, D_ROWS, stride=SUBLANES), :] = h2_ref[pl.ds(src, D_ROWS), :]
        return carry

    lax.fori_loop(0, ng, group, 0, unroll=2)
    h = jnp.concatenate([stage_h[:, s * SUBLANES:(s + 1) * SUBLANES, :].reshape(TP, LANES) for s in range(D_ROWS)],
                        axis=1)
    c = _rms(h, g_ple_ref[...]).astype(BF16)
    gate = jax.nn.sigmoid(jnp.dot(c, w_gate_ref[...], preferred_element_type=F32))
    proj = jnp.dot(p_ref[...].astype(BF16), w_proj_ref[...], preferred_element_type=F32)
    h = h + proj * gate
    y_ref[...] = _rms(h, g_fin_ref[...])


def _full(shape, single=False):
    nd = len(shape)
    if single:
        return pl.BlockSpec(shape, lambda *_: (0,) * nd, pipeline_mode=pl.Buffered(1))
    return pl.BlockSpec(shape, lambda *_: (0,) * nd)


def _mixer(x_prompt, x_tm, hist_tm, weights, tri, expert_w):
    batch, seq, _ = x_prompt.shape
    dec_seq, dec_batch, _ = x_tm.shape
    nj = seq // TP
    npt = batch * nj
    sblk = TP // dec_seq
    nst = dec_batch // sblk
    n_tiles = npt + nst
    nh = CONV_WIDTH - 1
    w_specs = [_full(w.shape, single=True) for w in weights]
    body = functools.partial(_mixer_body, n_prompt_tiles=npt, nj=nj, dec_seq=dec_seq, sblk=sblk)

    def p_idx(i):
        return jnp.minimum(i, npt - 1)

    def s_idx(i):
        return jnp.maximum(i - npt, 0)

    assert npt % N_EXPERTS == 0
    parts = npt // N_EXPERTS
    e_specs = [pl.BlockSpec((1, w.shape[1] // parts, w.shape[2]), lambda i: (p_idx(i) // parts, p_idx(i) % parts, 0))
               for w in expert_w]
    e_shapes = [jax.ShapeDtypeStruct(w.shape, BF16) for w in expert_w]

    return pl.pallas_call(
        body,
        grid=(n_tiles,),
        in_specs=[pl.BlockSpec((1, TP, D_MODEL), lambda i: (p_idx(i) // nj, p_idx(i) % nj, 0)),
                  pl.BlockSpec((dec_seq, sblk, D_MODEL), lambda i: (0, s_idx(i), 0), pipeline_mode=pl.Buffered(1)),
                  pl.BlockSpec((nh, sblk, C_CONV), lambda i: (0, s_idx(i), 0), pipeline_mode=pl.Buffered(1))]
        + w_specs + [_full(tri.shape, single=True)] + e_specs,
        out_specs=[
            pl.BlockSpec((TP * D_ROWS, LANES), lambda i: (i, 0)),
            pl.BlockSpec((1, SUBLANES, TP), lambda i: (i, 0, 0)),
            pl.BlockSpec((1, CLS_ROWS, LANES), lambda i: (i, 0, 0)),
            pl.BlockSpec((1, HIST, C_CONV), lambda i: (p_idx(i) // nj, 0, 0)),
            pl.BlockSpec((nh, sblk, C_CONV), lambda i: (0, s_idx(i), 0)),
            pl.BlockSpec((dec_seq, sblk, C_GMLP), lambda i: (0, s_idx(i), 0)),
        ] + e_specs,
        out_shape=[
            jax.ShapeDtypeStruct((n_tiles * TP * D_ROWS, LANES), F32),
            jax.ShapeDtypeStruct((n_tiles, SUBLANES, TP), I32),
            jax.ShapeDtypeStruct((n_tiles, CLS_ROWS, LANES), F32),
            jax.ShapeDtypeStruct((batch, HIST, C_CONV), F32),
            jax.ShapeDtypeStruct((nh, dec_batch, C_CONV), F32),
            jax.ShapeDtypeStruct((dec_seq, dec_batch, C_GMLP), F32),
        ] + e_shapes,
        scratch_shapes=[
            pltpu.VMEM((C_CONV // LANES, CONV_PITCH * (HIST + TP), LANES), F32),
            pltpu.VMEM((nh + dec_seq, sblk, C_CONV), F32),
            pltpu.VMEM((TP, D_MODEL), BF16),
            pltpu.VMEM((CLS_ROWS, LANES), F32),
            pltpu.VMEM((TP // SUBLANES, GROUP_ROWS, LANES), F32),
            pltpu.VMEM((SUBLANES, TP), I32),
            pltpu.SMEM((1, TP), I32),
        ],
        compiler_params=pltpu.CompilerParams(
            dimension_semantics=("arbitrary",), vmem_limit_bytes=VMEM_LIMIT),
        name="mixer",
    )(x_prompt, x_tm, hist_tm, *weights, tri, *expert_w)


def _moe(tile_tabs, run_tabs, h1, g_ffn, w_r, b_r, w_g, w_u, w_d):
    n_tiles = tile_tabs[0].shape[0]
    n_prefetch = len(tile_tabs) + len(run_tabs)
    grid_spec = pltpu.PrefetchScalarGridSpec(
        num_scalar_prefetch=n_prefetch,
        grid=(n_tiles,),
        in_specs=[
            pl.BlockSpec(memory_space=pl.ANY),
            pl.BlockSpec((1, D_MODEL), lambda i, *_: (0, 0)),
            pl.BlockSpec((D_MODEL, LANES), lambda i, *_: (0, 0)),
            pl.BlockSpec((1, LANES), lambda i, *_: (0, 0)),
            pl.BlockSpec((EPG, D_MODEL, D_EXPERT), lambda i, grp, *_: (grp[i], 0, 0)),
            pl.BlockSpec((EPG, D_MODEL, D_EXPERT), lambda i, grp, *_: (grp[i], 0, 0)),
            pl.BlockSpec((EPG, D_EXPERT, D_MODEL), lambda i, grp, *_: (grp[i], 0, 0)),
        ],
        out_specs=pl.BlockSpec(memory_space=pl.ANY),
        scratch_shapes=[pltpu.VMEM((2, TM * D_ROWS, LANES), F32),
                        pltpu.VMEM((2, TM * D_ROWS, LANES), F32),
                        pltpu.SemaphoreType.DMA((2,)),
                        pltpu.SemaphoreType.DMA((2,))],
    )
    return pl.pallas_call(
        _moe_body,
        grid_spec=grid_spec,
        out_shape=jax.ShapeDtypeStruct(h1.shape, F32),
        compiler_params=pltpu.CompilerParams(
            dimension_semantics=("arbitrary",), vmem_limit_bytes=VMEM_LIMIT),
        name="moe",
    )(*tile_tabs, *run_tabs, h1, g_ffn, w_r, b_r, w_g, w_u, w_d)


def _ple(lp, h2, tile0, p, g_ple, w_gate, w_proj, g_fin, name):
    n_tiles = p.shape[0] // TP
    return pl.pallas_call(
        _ple_body,
        grid=(n_tiles,),
        in_specs=[
            pl.BlockSpec((1, 1, TP), lambda i: (tile0 + i, 0, 0), memory_space=pltpu.SMEM),
            pl.BlockSpec((TP * D_ROWS, LANES), lambda i: (tile0 + i, 0)),
            pl.BlockSpec((TP, PLE_DIM), lambda i: (i, 0)),
            _full(g_ple.shape), _full(w_gate.shape), _full(w_proj.shape), _full(g_fin.shape),
        ],
        out_specs=pl.BlockSpec((TP, D_MODEL), lambda i: (i, 0)),
        out_shape=jax.ShapeDtypeStruct((n_tiles * TP, D_MODEL), F32),
        scratch_shapes=[pltpu.VMEM((TP // SUBLANES, GROUP_ROWS, LANES), F32)],
        compiler_params=pltpu.CompilerParams(
            dimension_semantics=("arbitrary",), vmem_limit_bytes=VMEM_LIMIT),
        name=name,
    )(lp, h2, p, g_ple, w_gate, w_proj, g_fin)


def kernel(x_prompt, x_sample, p_prompt, p_sample, state_conv, norm_mix_g, w_in, conv_w, conv_b, conv_ln_g, conv_ln_b, gmlp_ln_g, gmlp_ln_b, w_s, b_s, w_out, norm_ffn_g, w_router_group, b_router_group, w_router_expert, b_router_expert, w_exp_gate, w_exp_up, w_exp_down, norm_ple_g, w_ple_gate, w_ple_proj, norm_final_g):
    depth = w_in.shape[0]
    assert depth == 1, "single-layer pipeline"
    batch, seq, _ = x_prompt.shape
    dec_batch, dec_seq, _ = x_sample.shape
    assert seq % TP == 0 and TP % CHUNK == 0 and TP % dec_seq == 0 and dec_batch % (TP // dec_seq) == 0
    sblk = TP // dec_seq
    n_prompt = batch * seq
    n_sample = dec_batch * dec_seq
    n_tok = n_prompt + n_sample
    n_src_tiles = n_tok // TP

    row = lambda a: a.reshape(1, -1)
    w_in_b = w_in[0].astype(BF16)
    w_out_b = w_out[0].astype(BF16)
    cw = jnp.concatenate([conv_w[0], jnp.zeros((1, C_CONV), F32)], axis=0)
    tril = jnp.asarray(np.tril(np.ones((CHUNK, CHUNK), bool)))
    ws_m = jnp.where(tril[None], w_s[0], 0.0)
    ws_cat = jnp.concatenate([ws_m[0::2], ws_m[1::2]], axis=2).astype(BF16)
    bs_lane = jnp.repeat(jnp.transpose(b_s[0]), HEAD_DIM, axis=1)
    w_rt = jnp.zeros((CLS_ROWS, D_MODEL), F32)
    w_rt = w_rt.at[0:N_GROUPS].set(jnp.transpose(w_router_group[0]))
    w_rt = w_rt.at[N_GROUPS:N_GROUPS + N_EXPERTS].set(jnp.transpose(w_router_expert[0])).astype(BF16)
    b_r = jnp.zeros((CLS_ROWS, 1), F32)
    b_r = b_r.at[0:N_GROUPS, 0].set(b_router_group[0]).at[N_GROUPS:N_GROUPS + N_EXPERTS, 0].set(b_router_expert[0])
    n_logit = N_GROUPS + N_EXPERTS
    w_rn = jnp.concatenate([w_router_group[0], w_router_expert[0], jnp.zeros((D_MODEL, LANES - n_logit), F32)],
                           axis=1).astype(BF16)
    b_rn = jnp.concatenate([b_router_group[0], b_router_expert[0], jnp.zeros((LANES - n_logit,), F32)]).reshape(1, LANES)
    tri = jnp.asarray(np.triu(np.ones((TP, TP), np.float32)), dtype=BF16)
    wsl = jnp.where(jnp.asarray(np.tril(np.ones((dec_seq, dec_seq), bool)))[None], w_s[0][:, :dec_seq, :dec_seq], 0.0)
    wsl = jnp.repeat(jnp.transpose(wsl, (1, 2, 0)).reshape(dec_seq * dec_seq, N_HEADS), HEAD_DIM, axis=1)
    bsl = jnp.repeat(jnp.transpose(b_s[0][:, :dec_seq]), HEAD_DIM, axis=1)

    weights = (row(norm_mix_g[0]), w_in_b, cw, row(conv_b[0]), row(conv_ln_g[0]), row(conv_ln_b[0]),
               row(gmlp_ln_g[0]), row(gmlp_ln_b[0]), ws_cat, bs_lane, wsl, bsl,
               w_out_b, row(norm_ffn_g[0]), w_rt, b_r)

    x_tm = jnp.transpose(x_sample, (1, 0, 2))
    hist_tm = jnp.transpose(state_conv[0], (1, 0, 2))
    h1, lp, tab, cst_p, cst_s, v_s, w_g, w_u, w_d = _mixer(
        x_prompt, x_tm, hist_tm, weights, tri, (w_exp_gate[0], w_exp_up[0], w_exp_down[0]))

    rank0 = tab[:, :N_CLASSES, 0].astype(I32)
    cnt = tab[:, :N_CLASSES, 1].astype(I32)
    lsrc = tab[:, :N_CLASSES, 2].astype(I32) + (jnp.arange(n_src_tiles, dtype=I32) * TP)[:, None]
    total = rank0[-1] + cnt[-1]
    padded = ((total + TM - 1) // TM) * TM
    ids = jnp.arange(N_CLASSES, dtype=I32)
    ends = jnp.sum(jnp.where(ids[None, :] <= ids[:, None], padded[None, :], 0), axis=1)
    offs = ends - padded
    all_rows = jnp.sum(padded)
    n_tiles = (n_tok + N_CLASSES * (TM - 1)) // TM + 2
    tile_start = jnp.arange(n_tiles, dtype=I32) * TM
    used = tile_start < all_rows
    last_cls = jnp.max(jnp.where(padded > 0, ids, 0))
    tile_cls = jnp.sum((tile_start[:, None] >= ends[None, :]).astype(I32), axis=1)
    tile_cls = jnp.where(used, tile_cls, last_cls)
    of_tile = tile_cls[:, None] == ids[None, :]
    tile_r0 = tile_start - jnp.sum(jnp.where(of_tile, offs[None, :], 0), axis=1)
    tile_total = jnp.sum(jnp.where(of_tile, total[None, :], 0), axis=1)
    tile_nval = jnp.where(used, jnp.clip(tile_total - tile_r0, 0, TM), 0).astype(I32)
    pair = tile_cls % 6
    assert _PAIRS == ((0, 1), (0, 2), (0, 3), (1, 2), (1, 3), (2, 3))
    tile_lo = jnp.where(pair < 3, 0, jnp.where(pair < 5, 1, 2)).astype(I32)
    tile_hi = jnp.where(pair == 0, 1, jnp.where(jnp.logical_or(pair == 1, pair == 3), 2, 3)).astype(I32)
    run_beg = jnp.sum(jnp.where(of_tile[None], rank0[:, None, :], 0), axis=2)
    run_end = run_beg + jnp.sum(jnp.where(of_tile[None], cnt[:, None, :], 0), axis=2)
    tile_jlo = jnp.sum((run_end <= tile_r0[None, :]).astype(I32), axis=0)
    tile_jhi = jnp.sum((run_beg < (tile_r0 + tile_nval)[None, :]).astype(I32), axis=0)
    tile_jhi = jnp.maximum(tile_jhi, tile_jlo)

    h2 = _moe((tile_cls // 6, tile_lo, tile_hi, tile_nval, tile_cls, tile_r0, tile_jlo, tile_jhi),
              (rank0.reshape(-1), cnt.reshape(-1), lsrc.reshape(-1)),
              h1, row(norm_ffn_g[0]), w_rn, b_rn, w_g, w_u, w_d)

    lp = lp[:, 0:1, :]
    ple_w = (row(norm_ple_g[0]), w_ple_gate[0].astype(BF16), w_ple_proj[0].astype(BF16), row(norm_final_g))
    y_p = _ple(lp, h2, 0, p_prompt[0].reshape(n_prompt, PLE_DIM), *ple_w, name="ple_prompt")
    p_s_tm = jnp.transpose(p_sample[0].reshape(dec_batch // sblk, sblk, dec_seq, PLE_DIM), (0, 2, 1, 3))
    y_s = _ple(lp, h2, n_prompt // TP, p_s_tm.reshape(n_sample, PLE_DIM), *ple_w, name="ple_sample")

    y_prompt = y_p.reshape(batch, seq, D_MODEL)
    y_sample = jnp.transpose(y_s.reshape(dec_batch // sblk, dec_seq, sblk, D_MODEL), (0, 2, 1, 3))
    y_sample = y_sample.reshape(dec_batch, dec_seq, D_MODEL)
    state_conv_prompt = cst_p[:, HIST_OFF:, :][None]
    state_conv_sample = jnp.transpose(cst_s, (1, 0, 2))[None]
    state_gmlp_v_sample = jnp.transpose(v_s, (1, 0, 2))[None]
    return (y_prompt, y_sample, state_conv_prompt, state_conv_sample, state_gmlp_v_sample)
```

```python
import functools

import numpy as np
import jax
import jax.numpy as jnp
from jax import lax
from jax.experimental import pallas as pl
from jax.experimental.pallas import tpu as pltpu

F32 = jnp.float32
BF16 = jnp.bfloat16
I32 = jnp.int32

D_MODEL = 1024
C_CONV = 512
C_GMLP = 512
N_HEADS = 8
HEAD_DIM = 64
CONV_WIDTH = 31
CHUNK = 128
PLE_DIM = 256
N_GROUPS = 4
EPG = 4
N_EXPERTS = 16
D_EXPERT = 512
EPS = 1e-6
LANES = 128
SUBLANES = 8

N_CLASSES = N_GROUPS * 6
CLS_ROWS = 32
HIST = 32
HIST_OFF = HIST - (CONV_WIDTH - 1)
D_ROWS = D_MODEL // LANES
assert D_ROWS == SUBLANES
GROUP_ROWS = D_ROWS * SUBLANES

TP = 512
WIN_TILES = 2
WIN = WIN_TILES * TP
TM = 256
CONV_ROWS = 64
CONV_PITCH = 2
PIECES = tuple(TM >> b for b in range(TM.bit_length()))
LARGE_PIECE = 32

VMEM_LIMIT = 58 * 1024 * 1024

_PAIRS = ((0, 1), (0, 2), (0, 3), (1, 2), (1, 3), (2, 3))


def _rms(x, g):
    ms = jnp.mean(x * x, axis=-1, keepdims=True)
    return x * lax.rsqrt(ms + EPS) * g


def _ln(x, g, b):
    mu = jnp.mean(x, axis=-1, keepdims=True)
    xc = x - mu
    var = jnp.mean(xc * xc, axis=-1, keepdims=True)
    return xc * lax.rsqrt(var + EPS) * g + b


def _max4(v):
    return jnp.maximum(jnp.maximum(v[0], v[1]), jnp.maximum(v[2], v[3]))


def _first4(v, m):
    return jnp.where(v[0] == m, 0, jnp.where(v[1] == m, 1, jnp.where(v[2] == m, 2, 3))).astype(I32)


def _route(lt):
    gl = [lt[i:i + 1, :] for i in range(N_GROUPS)]
    g = _first4(gl, _max4(gl))
    a = []
    for j in range(EPG):
        rows = [lt[N_GROUPS + EPG * q + j:N_GROUPS + EPG * q + j + 1, :] for q in range(N_GROUPS)]
        a.append(jnp.where(g == 0, rows[0], jnp.where(g == 1, rows[1], jnp.where(g == 2, rows[2], rows[3]))))
    v1 = _max4(a)
    i1 = _first4(a, v1)
    a2 = [jnp.where(i1 == j, -jnp.inf, a[j]) for j in range(EPG)]
    i2 = _first4(a2, _max4(a2))
    lo = jnp.minimum(i1, i2)
    hi = jnp.maximum(i1, i2)
    pair = jnp.where(lo == 0, hi - 1, jnp.where(lo == 1, hi + 1, 5))
    return g * 6 + pair


def _combine_weights(logits, grp, e_lo, e_hi):
    lane = lax.broadcasted_iota(I32, logits.shape, 1)
    is_group = lane < N_GROUPS
    m = jnp.max(jnp.where(is_group, logits, -jnp.inf), axis=1, keepdims=True)
    den = jnp.sum(jnp.where(is_group, jnp.exp(logits - m), 0.0), axis=1, keepdims=True)
    base = N_GROUPS + EPG * grp
    v_lo = jnp.sum(jnp.where(lane == base + e_lo, logits, 0.0), axis=1, keepdims=True)
    v_hi = jnp.sum(jnp.where(lane == base + e_hi, logits, 0.0), axis=1, keepdims=True)
    top = jnp.maximum(v_lo, v_hi)
    p_lo = jnp.exp(v_lo - top)
    p_hi = jnp.exp(v_hi - top)
    scale = 1.0 / (den * (p_lo + p_hi))
    return p_lo * scale, p_hi * scale


def _load_token_blocks(ref, n_tok):
    return jnp.concatenate([ref[pl.ds(s, n_tok, stride=D_ROWS), :] for s in range(D_ROWS)], axis=1)


def _store_token_blocks(ref, x, n_tok):
    for s in range(D_ROWS):
        ref[pl.ds(s, n_tok, stride=D_ROWS), :] = x[:, s * LANES:(s + 1) * LANES]


def _route_tile(h, g_ffn_ref, w_rt_ref, b_r_ref, stage_h, cls_scr, sub):
    c = _rms(h, g_ffn_ref[...]).astype(BF16)
    lt = lax.dot_general(w_rt_ref[...], c, (((1,), (1,)), ((), ())), preferred_element_type=F32)
    cls_scr[sub] = jnp.broadcast_to(_route(lt + b_r_ref[...]), (SUBLANES, TP))
    ng = TP // SUBLANES
    for s in range(D_ROWS):
        stage_h[sub, :, s * SUBLANES:(s + 1) * SUBLANES, :] = h[:, s * LANES:(s + 1) * LANES].reshape(
            ng, SUBLANES, LANES)


def _sort_window(tri_ref, carry_ref, h1_ref, lp_ref, tab_ref, stage_h, cls_scr, lp_vmem, lp_smem):
    rows = lax.broadcasted_iota(I32, (CLS_ROWS, TP), 0)
    onehot, counts = [], []
    lstart = jnp.zeros((CLS_ROWS, 1), F32)
    for s in range(WIN_TILES):
        cls = cls_scr[s, 0:1, :]
        onehot.append(jnp.where(rows == cls, 1.0, 0.0).astype(F32))
        counts.append(jnp.sum(onehot[s], axis=1, keepdims=True))
        lstart = lstart + jnp.sum(jnp.where(cls < rows, 1.0, 0.0).astype(F32), axis=1, keepdims=True)
    base = lstart
    for s in range(WIN_TILES):
        pre = jnp.dot(onehot[s].astype(BF16), tri_ref[...], preferred_element_type=F32)
        lpos = jnp.sum(onehot[s] * (pre - 1.0 + base), axis=0, keepdims=True).astype(I32)
        lp_ref[0, :, s * TP:(s + 1) * TP] = jnp.broadcast_to(lpos, (SUBLANES, TP))
        lp_vmem[:, s * TP:(s + 1) * TP] = jnp.broadcast_to(lpos, (SUBLANES, TP))
        base = base + counts[s]
    tot = base - lstart
    carry = carry_ref[:, 0:1]
    lane = lax.broadcasted_iota(I32, (CLS_ROWS, LANES), 1)
    tab_ref[0] = jnp.where(lane == 0, carry, jnp.where(lane == 1, tot, jnp.where(lane == 2, lstart, 0.0)))
    carry_ref[...] = jnp.broadcast_to(carry + tot, (CLS_ROWS, LANES))
    pltpu.sync_copy(lp_vmem.at[pl.ds(0, 1)], lp_smem)

    for s in range(WIN_TILES):
        def group(g, carry_, s=s):
            for k in range(SUBLANES):
                dst = pl.multiple_of(lp_smem[0, s * TP + g * SUBLANES + k] * D_ROWS, D_ROWS)
                h1_ref[pl.ds(dst, D_ROWS), :] = stage_h[s, g, pl.ds(k, D_ROWS, stride=SUBLANES), :]
            return carry_

        lax.fori_loop(0, TP // SUBLANES, group, 0, unroll=2)


def _mixer_front(x, g_mix_ref, w_in_ref):
    a = _rms(x, g_mix_ref[...]).astype(BF16)
    z = jnp.dot(a, w_in_ref[...], preferred_element_type=F32)
    glu = z[:, 0:C_CONV] * jax.nn.sigmoid(z[:, C_CONV:2 * C_CONV])
    u = jax.nn.gelu(z[:, 2 * C_CONV:2 * C_CONV + C_GMLP])
    gv = jax.nn.gelu(z[:, 2 * C_CONV + C_GMLP:])
    return glu, u, gv


def _mixer_body(xp_ref, xs_ref, hist_ref, g_mix_ref, w_in_ref, cw_ref, cb_ref, clg_ref, clb_ref, glg_ref, glb_ref,
                ws_ref, bs_ref, wsl_ref, bsl_ref, w_out_ref, g_ffn_ref, w_rt_ref, b_r_ref, tri_ref,
                eg_ref, eu_ref, ed_ref,
                h1_ref, lp_ref, tab_ref, cstp_ref, csts_ref, vs_ref, egb_ref, eub_ref, edb_ref,
                glu_scr, xs_scr, cat_scr, carry_ref, stage_h, cls_scr, lp_vmem, lp_smem,
                *, n_prompt_tiles, nj, dec_seq, sblk):
    i = pl.program_id(0)
    sub = lax.rem(i, WIN_TILES)

    egb_ref[...] = eg_ref[...].astype(BF16)
    eub_ref[...] = eu_ref[...].astype(BF16)
    edb_ref[...] = ed_ref[...].astype(BF16)

    @pl.when(i == 0)
    def _():
        carry_ref[...] = jnp.zeros_like(carry_ref)

    def tail(x):
        h = x + jnp.dot(cat_scr[...], w_out_ref[...], preferred_element_type=F32)
        _route_tile(h, g_ffn_ref, w_rt_ref, b_r_ref, stage_h, cls_scr, sub)

    @pl.when(i < n_prompt_tiles)
    def _prompt():
        j = lax.rem(i, nj)
        x = xp_ref[0]
        glu, u, gv = _mixer_front(x, g_mix_ref, w_in_ref)

        def time_rows(t0, n):
            return pl.ds(CONV_PITCH * t0, n, stride=CONV_PITCH)

        @pl.when(j == 0)
        def _():
            for s in range(C_CONV // LANES):
                glu_scr[s, time_rows(0, HIST), :] = jnp.zeros((HIST, LANES), F32)

        @pl.when(j > 0)
        def _():
            for s in range(C_CONV // LANES):
                glu_scr[s, time_rows(0, HIST), :] = glu_scr[s, time_rows(TP, HIST), :]

        for s in range(C_CONV // LANES):
            glu_scr[s, time_rows(HIST, TP), :] = glu[:, s * LANES:(s + 1) * LANES]
        cstp_ref[0] = jnp.concatenate([glu_scr[s, time_rows(TP, HIST), :] for s in range(C_CONV // LANES)], axis=1)

        cb = cb_ref[...]
        for r0 in range(0, TP, CONV_ROWS):
            slabs = []
            for s in range(C_CONV // LANES):
                acc = None
                for k in range(CONV_WIDTH):
                    term = (cw_ref[k:k + 1, s * LANES:(s + 1) * LANES]
                            * glu_scr[s, time_rows(r0 + HIST_OFF + k, CONV_ROWS), :])
                    acc = term if acc is None else acc + term
                slabs.append(acc)
            ya = jax.nn.silu(_ln(jnp.concatenate(slabs, axis=1) + cb, clg_ref[...], clb_ref[...]))
            cat_scr[r0:r0 + CONV_ROWS, 0:C_CONV] = ya.astype(BF16)

        v = _ln(gv, glg_ref[...], glb_ref[...])
        vb = v.astype(BF16)
        lane = lax.broadcasted_iota(I32, (CHUNK, LANES), 1)
        is_lo = lane < HEAD_DIM
        zero = jnp.zeros((CHUNK, LANES), BF16)
        for c in range(TP // CHUNK):
            for p in range(N_HEADS // 2):
                blk = vb[c * CHUNK:(c + 1) * CHUNK, p * LANES:(p + 1) * LANES]
                rhs = jnp.concatenate([jnp.where(is_lo, blk, zero), jnp.where(is_lo, zero, blk)], axis=0)
                mixed = jnp.dot(ws_ref[p], rhs, preferred_element_type=F32) + bs_ref[:, p * LANES:(p + 1) * LANES]
                yb = u[c * CHUNK:(c + 1) * CHUNK, p * LANES:(p + 1) * LANES] * mixed
                cat_scr[c * CHUNK:(c + 1) * CHUNK, C_CONV + p * LANES:C_CONV + (p + 1) * LANES] = yb.astype(BF16)
        tail(x)

    @pl.when(i >= n_prompt_tiles)
    def _sample():
        x = xs_ref[...].reshape(TP, D_MODEL)
        glu, u, gv = _mixer_front(x, g_mix_ref, w_in_ref)

        nh = CONV_WIDTH - 1
        xs_scr[0:nh] = hist_ref[...]
        xs_scr[nh:nh + dec_seq] = glu.reshape(dec_seq, sblk, C_CONV)
        csts_ref[...] = xs_scr[dec_seq:dec_seq + nh]

        cb = cb_ref[...]
        for t in range(dec_seq):
            acc = jnp.zeros((sblk, C_CONV), F32)
            for k in range(CONV_WIDTH):
                acc = acc + cw_ref[k:k + 1, :] * xs_scr[t + k]
            ya = jax.nn.silu(_ln(acc + cb, clg_ref[...], clb_ref[...]))
            cat_scr[t * sblk:(t + 1) * sblk, 0:C_CONV] = ya.astype(BF16)

        v = _ln(gv, glg_ref[...], glb_ref[...])
        vs_ref[...] = v.reshape(dec_seq, sblk, C_GMLP)
        for t in range(dec_seq):
            mixed = jnp.broadcast_to(bsl_ref[t:t + 1, :], (sblk, C_GMLP))
            for tp in range(t + 1):
                r = t * dec_seq + tp
                mixed = mixed + wsl_ref[r:r + 1, :] * v[tp * sblk:(tp + 1) * sblk, :]
            yb = u[t * sblk:(t + 1) * sblk, :] * mixed
            cat_scr[t * sblk:(t + 1) * sblk, C_CONV:] = yb.astype(BF16)
        tail(x)

    @pl.when(sub == WIN_TILES - 1)
    def _():
        _sort_window(tri_ref, carry_ref, h1_ref, lp_ref, tab_ref, stage_h, cls_scr, lp_vmem, lp_smem)


def _for_pieces(ln, fn):
    def emit(sizes):
        for size in sizes:
            shift = size.bit_length()
            off = lax.shift_left(lax.shift_right_logical(ln, shift), shift)

            @pl.when((ln & size) != 0)
            def _(off=off, size=size):
                fn(off, size)

    large = tuple(s for s in PIECES if s >= LARGE_PIECE)

    @pl.when(ln >= LARGE_PIECE)
    def _():
        emit(large)

    emit(tuple(s for s in PIECES if s < LARGE_PIECE))


def _for_runs(k, tcls_ref, r0_ref, nval_ref, jlo_ref, jhi_ref, rank0_ref, cnt_ref, lsrc_ref, fn):
    c = tcls_ref[k]
    ra = r0_ref[k]
    rb = ra + nval_ref[k]

    def body(j, carry):
        idx = j * N_CLASSES + c
        s = rank0_ref[idx]
        lo = jnp.maximum(s, ra)
        hi = jnp.minimum(s + cnt_ref[idx], rb)
        ln = hi - lo

        @pl.when(ln > 0)
        def _():
            src = lsrc_ref[idx] + (lo - s)
            dst = lo - ra
            _for_pieces(ln, lambda off, size: fn(src + off, dst + off, size))

        return carry

    lax.fori_loop(jlo_ref[k], jhi_ref[k], body, 0)


def _blocks(ref, first, n):
    start = first * D_ROWS if isinstance(first, int) else pl.multiple_of(first * D_ROWS, D_ROWS)
    return ref.at[pl.ds(start, n * D_ROWS), :]


def _moe_body(grp_ref, lo_ref, hi_ref, nval_ref, tcls_ref, r0_ref, jlo_ref, jhi_ref, rank0_ref, cnt_ref, lsrc_ref,
              h1_hbm, g_ffn_ref, w_r_ref, b_r_ref, wg_ref, wu_ref, wd_ref,
              h2_hbm, hbuf, obuf, gsem, ssem):
    k = pl.program_id(0)
    n = pl.num_programs(0)
    slot = lax.rem(k, 2)
    tables = (tcls_ref, r0_ref, nval_ref, jlo_ref, jhi_ref, rank0_ref, cnt_ref, lsrc_ref)

    def gather_tile(tile, to_slot):
        def copy(src, dst, size):
            pltpu.make_async_copy(_blocks(h1_hbm, src, size), _blocks(hbuf.at[to_slot], dst, size),
                                  gsem.at[to_slot]).start()
        _for_runs(tile, *tables, copy)

    def wait_rows(n_rows, make_copy):
        _for_pieces(n_rows, lambda off, size: make_copy(size).wait())

    @pl.when(k == 0)
    def _():
        hbuf[...] = jnp.zeros_like(hbuf)
        gather_tile(0, 0)

    @pl.when(jnp.logical_and(k >= 2, nval_ref[jnp.maximum(k - 2, 0)] > 0))
    def _():
        wait_rows(nval_ref[jnp.maximum(k - 2, 0)],
                  lambda size: pltpu.make_async_copy(_blocks(obuf.at[slot], 0, size), _blocks(h2_hbm, 0, size),
                                                     ssem.at[slot]))

    nxt = jnp.minimum(k + 1, n - 1)

    @pl.when(jnp.logical_and(k + 1 < n, nval_ref[nxt] > 0))
    def _():
        gather_tile(nxt, 1 - slot)

    @pl.when(nval_ref[k] > 0)
    def _():
        wait_rows(nval_ref[k],
                  lambda size: pltpu.make_async_copy(_blocks(h1_hbm, 0, size), _blocks(hbuf.at[slot], 0, size),
                                                     gsem.at[slot]))
        e_lo = lo_ref[k]
        e_hi = hi_ref[k]

        def experts(m):
            h = _load_token_blocks(hbuf.at[slot], m)
            c = _rms(h, g_ffn_ref[...]).astype(BF16)
            logits = jnp.dot(c, w_r_ref[...], preferred_element_type=F32) + b_r_ref[...]
            wl, wh = _combine_weights(logits, grp_ref[k], e_lo, e_hi)

            def hidden(e):
                gate = jnp.dot(c, wg_ref[e], preferred_element_type=F32)
                return (jax.nn.silu(gate) * jnp.dot(c, wu_ref[e], preferred_element_type=F32)).astype(BF16)

            yl = jnp.dot(hidden(e_lo), wd_ref[e_lo], preferred_element_type=F32)
            yh = jnp.dot(hidden(e_hi), wd_ref[e_hi], preferred_element_type=F32)
            _store_token_blocks(obuf.at[slot], h + (wl * yl + wh * yh), m)

        @pl.when(nval_ref[k] > TM // 2)
        def _():
            experts(TM)

        @pl.when(nval_ref[k] <= TM // 2)
        def _():
            experts(TM // 2)

        def copy_back(src, dst, size):
            pltpu.make_async_copy(_blocks(obuf.at[slot], dst, size), _blocks(h2_hbm, src, size),
                                  ssem.at[slot]).start()
        _for_runs(k, *tables, copy_back)


def _ple_body(lp_ref, h2_ref, p_ref, g_ple_ref, w_gate_ref, w_proj_ref, g_fin_ref, y_ref, stage_h, *, tile0):
    ng = TP // SUBLANES
    first = lax.rem(tile0 + pl.program_id(0), WIN_TILES) * TP

    def group(g, carry):
        for k in range(SUBLANES):
            src = pl.multiple_of(lp_ref[0, 0, first + g * SUBLANES + k] * D_ROWS, D_ROWS)
            stage_h[g, pl.ds(k, D_ROWS, stride=SUBLANES), :] = h2_ref[pl.ds(src, D_ROWS), :]
        return carry

    lax.fori_loop(0, ng, group, 0, unroll=2)
    h = jnp.concatenate([stage_h[:, s * SUBLANES:(s + 1) * SUBLANES, :].reshape(TP, LANES) for s in range(D_ROWS)],
                        axis=1)
    c = _rms(h, g_ple_ref[...]).astype(BF16)
    gate = jax.nn.sigmoid(jnp.dot(c, w_gate_ref[...], preferred_element_type=F32))
    proj = jnp.dot(p_ref[...].astype(BF16), w_proj_ref[...], preferred_element_type=F32)
    h = h + proj * gate
    y_ref[...] = _rms(h, g_fin_ref[...])


def _full(shape, single=False):
    nd = len(shape)
    if single:
        return pl.BlockSpec(shape, lambda *_: (0,) * nd, pipeline_mode=pl.Buffered(1))
    return pl.BlockSpec(shape, lambda *_: (0,) * nd)


def _mixer(x_prompt, x_tm, hist_tm, weights, tri, expert_w):
    batch, seq, _ = x_prompt.shape
    dec_seq, dec_batch, _ = x_tm.shape
    nj = seq // TP
    npt = batch * nj
    sblk = TP // dec_seq
    nst = dec_batch // sblk
    n_tiles = npt + nst
    assert npt % WIN_TILES == 0 and nst % WIN_TILES == 0
    nh = CONV_WIDTH - 1
    w_specs = [_full(w.shape, single=True) for w in weights]
    body = functools.partial(_mixer_body, n_prompt_tiles=npt, nj=nj, dec_seq=dec_seq, sblk=sblk)

    def p_idx(i):
        return jnp.minimum(i, npt - 1)

    def s_idx(i):
        return jnp.maximum(i - npt, 0)

    assert npt % N_EXPERTS == 0
    parts = npt // N_EXPERTS
    e_specs = [pl.BlockSpec((1, w.shape[1] // parts, w.shape[2]), lambda i: (p_idx(i) // parts, p_idx(i) % parts, 0))
               for w in expert_w]
    e_shapes = [jax.ShapeDtypeStruct(w.shape, BF16) for w in expert_w]

    return pl.pallas_call(
        body,
        grid=(n_tiles,),
        in_specs=[pl.BlockSpec((1, TP, D_MODEL), lambda i: (p_idx(i) // nj, p_idx(i) % nj, 0)),
                  pl.BlockSpec((dec_seq, sblk, D_MODEL), lambda i: (0, s_idx(i), 0), pipeline_mode=pl.Buffered(1)),
                  pl.BlockSpec((nh, sblk, C_CONV), lambda i: (0, s_idx(i), 0), pipeline_mode=pl.Buffered(1))]
        + w_specs + [_full(tri.shape, single=True)] + e_specs,
        out_specs=[
            pl.BlockSpec((WIN * D_ROWS, LANES), lambda i: (i // WIN_TILES, 0)),
            pl.BlockSpec((1, SUBLANES, WIN), lambda i: (i // WIN_TILES, 0, 0)),
            pl.BlockSpec((1, CLS_ROWS, LANES), lambda i: (i // WIN_TILES, 0, 0)),
            pl.BlockSpec((1, HIST, C_CONV), lambda i: (p_idx(i) // nj, 0, 0)),
            pl.BlockSpec((nh, sblk, C_CONV), lambda i: (0, s_idx(i), 0)),
            pl.BlockSpec((dec_seq, sblk, C_GMLP), lambda i: (0, s_idx(i), 0)),
        ] + e_specs,
        out_shape=[
            jax.ShapeDtypeStruct((n_tiles * TP * D_ROWS, LANES), F32),
            jax.ShapeDtypeStruct((n_tiles // WIN_TILES, SUBLANES, WIN), I32),
            jax.ShapeDtypeStruct((n_tiles // WIN_TILES, CLS_ROWS, LANES), F32),
            jax.ShapeDtypeStruct((batch, HIST, C_CONV), F32),
            jax.ShapeDtypeStruct((nh, dec_batch, C_CONV), F32),
            jax.ShapeDtypeStruct((dec_seq, dec_batch, C_GMLP), F32),
        ] + e_shapes,
        scratch_shapes=[
            pltpu.VMEM((C_CONV // LANES, CONV_PITCH * (HIST + TP), LANES), F32),
            pltpu.VMEM((nh + dec_seq, sblk, C_CONV), F32),
            pltpu.VMEM((TP, D_MODEL), BF16),
            pltpu.VMEM((CLS_ROWS, LANES), F32),
            pltpu.VMEM((WIN_TILES, TP // SUBLANES, GROUP_ROWS, LANES), F32),
            pltpu.VMEM((WIN_TILES, SUBLANES, TP), I32),
            pltpu.VMEM((SUBLANES, WIN), I32),
            pltpu.SMEM((1, WIN), I32),
        ],
        compiler_params=pltpu.CompilerParams(
            dimension_semantics=("arbitrary",), vmem_limit_bytes=VMEM_LIMIT),
        name="mixer",
    )(x_prompt, x_tm, hist_tm, *weights, tri, *expert_w)


def _moe(tile_tabs, run_tabs, h1, g_ffn, w_r, b_r, w_g, w_u, w_d):
    n_tiles = tile_tabs[0].shape[0]
    n_prefetch = len(tile_tabs) + len(run_tabs)
    grid_spec = pltpu.PrefetchScalarGridSpec(
        num_scalar_prefetch=n_prefetch,
        grid=(n_tiles,),
        in_specs=[
            pl.BlockSpec(memory_space=pl.ANY),
            pl.BlockSpec((1, D_MODEL), lambda i, *_: (0, 0)),
            pl.BlockSpec((D_MODEL, LANES), lambda i, *_: (0, 0)),
            pl.BlockSpec((1, LANES), lambda i, *_: (0, 0)),
            pl.BlockSpec((EPG, D_MODEL, D_EXPERT), lambda i, grp, *_: (grp[i], 0, 0)),
            pl.BlockSpec((EPG, D_MODEL, D_EXPERT), lambda i, grp, *_: (grp[i], 0, 0)),
            pl.BlockSpec((EPG, D_EXPERT, D_MODEL), lambda i, grp, *_: (grp[i], 0, 0)),
        ],
        out_specs=pl.BlockSpec(memory_space=pl.ANY),
        scratch_shapes=[pltpu.VMEM((2, TM * D_ROWS, LANES), F32),
                        pltpu.VMEM((2, TM * D_ROWS, LANES), F32),
                        pltpu.SemaphoreType.DMA((2,)),
                        pltpu.SemaphoreType.DMA((2,))],
    )
    return pl.pallas_call(
        _moe_body,
        grid_spec=grid_spec,
        out_shape=jax.ShapeDtypeStruct(h1.shape, F32),
        compiler_params=pltpu.CompilerParams(
            dimension_semantics=("arbitrary",), vmem_limit_bytes=VMEM_LIMIT),
        name="moe",
    )(*tile_tabs, *run_tabs, h1, g_ffn, w_r, b_r, w_g, w_u, w_d)


def _ple(lp, h2, tile0, p, g_ple, w_gate, w_proj, g_fin, name):
    n_tiles = p.shape[0] // TP
    return pl.pallas_call(
        functools.partial(_ple_body, tile0=tile0),
        grid=(n_tiles,),
        in_specs=[
            pl.BlockSpec((1, 1, WIN), lambda i: ((tile0 + i) // WIN_TILES, 0, 0), memory_space=pltpu.SMEM),
            pl.BlockSpec((WIN * D_ROWS, LANES), lambda i: ((tile0 + i) // WIN_TILES, 0)),
            pl.BlockSpec((TP, PLE_DIM), lambda i: (i, 0)),
            _full(g_ple.shape), _full(w_gate.shape), _full(w_proj.shape), _full(g_fin.shape),
        ],
        out_specs=pl.BlockSpec((TP, D_MODEL), lambda i: (i, 0)),
        out_shape=jax.ShapeDtypeStruct((n_tiles * TP, D_MODEL), F32),
        scratch_shapes=[pltpu.VMEM((TP // SUBLANES, GROUP_ROWS, LANES), F32)],
        compiler_params=pltpu.CompilerParams(
            dimension_semantics=("arbitrary",), vmem_limit_bytes=VMEM_LIMIT),
        name=name,
    )(lp, h2, p, g_ple, w_gate, w_proj, g_fin)


def kernel(x_prompt, x_sample, p_prompt, p_sample, state_conv, norm_mix_g, w_in, conv_w, conv_b, conv_ln_g, conv_ln_b, gmlp_ln_g, gmlp_ln_b, w_s, b_s, w_out, norm_ffn_g, w_router_group, b_router_group, w_router_expert, b_router_expert, w_exp_gate, w_exp_up, w_exp_down, norm_ple_g, w_ple_gate, w_ple_proj, norm_final_g):
    depth = w_in.shape[0]
    assert depth == 1, "single-layer pipeline"
    batch, seq, _ = x_prompt.shape
    dec_batch, dec_seq, _ = x_sample.shape
    assert seq % TP == 0 and TP % CHUNK == 0 and TP % dec_seq == 0 and dec_batch % (TP // dec_seq) == 0
    sblk = TP // dec_seq
    n_prompt = batch * seq
    n_sample = dec_batch * dec_seq
    n_tok = n_prompt + n_sample
    n_windows = n_tok // WIN

    row = lambda a: a.reshape(1, -1)
    w_in_b = w_in[0].astype(BF16)
    w_out_b = w_out[0].astype(BF16)
    cw = jnp.concatenate([conv_w[0], jnp.zeros((1, C_CONV), F32)], axis=0)
    tril = jnp.asarray(np.tril(np.ones((CHUNK, CHUNK), bool)))
    ws_m = jnp.where(tril[None], w_s[0], 0.0)
    ws_cat = jnp.concatenate([ws_m[0::2], ws_m[1::2]], axis=2).astype(BF16)
    bs_lane = jnp.repeat(jnp.transpose(b_s[0]), HEAD_DIM, axis=1)
    w_rt = jnp.zeros((CLS_ROWS, D_MODEL), F32)
    w_rt = w_rt.at[0:N_GROUPS].set(jnp.transpose(w_router_group[0]))
    w_rt = w_rt.at[N_GROUPS:N_GROUPS + N_EXPERTS].set(jnp.transpose(w_router_expert[0])).astype(BF16)
    b_r = jnp.zeros((CLS_ROWS, 1), F32)
    b_r = b_r.at[0:N_GROUPS, 0].set(b_router_group[0]).at[N_GROUPS:N_GROUPS + N_EXPERTS, 0].set(b_router_expert[0])
    n_logit = N_GROUPS + N_EXPERTS
    w_rn = jnp.concatenate([w_router_group[0], w_router_expert[0], jnp.zeros((D_MODEL, LANES - n_logit), F32)],
                           axis=1).astype(BF16)
    b_rn = jnp.concatenate([b_router_group[0], b_router_expert[0], jnp.zeros((LANES - n_logit,), F32)]).reshape(1, LANES)
    tri = jnp.asarray(np.triu(np.ones((TP, TP), np.float32)), dtype=BF16)
    wsl = jnp.where(jnp.asarray(np.tril(np.ones((dec_seq, dec_seq), bool)))[None], w_s[0][:, :dec_seq, :dec_seq], 0.0)
    wsl = jnp.repeat(jnp.transpose(wsl, (1, 2, 0)).reshape(dec_seq * dec_seq, N_HEADS), HEAD_DIM, axis=1)
    bsl = jnp.repeat(jnp.transpose(b_s[0][:, :dec_seq]), HEAD_DIM, axis=1)

    weights = (row(norm_mix_g[0]), w_in_b, cw, row(conv_b[0]), row(conv_ln_g[0]), row(conv_ln_b[0]),
               row(gmlp_ln_g[0]), row(gmlp_ln_b[0]), ws_cat, bs_lane, wsl, bsl,
               w_out_b, row(norm_ffn_g[0]), w_rt, b_r)

    x_tm = jnp.transpose(x_sample, (1, 0, 2))
    hist_tm = jnp.transpose(state_conv[0], (1, 0, 2))
    h1, lp, tab, cst_p, cst_s, v_s, w_g, w_u, w_d = _mixer(
        x_prompt, x_tm, hist_tm, weights, tri, (w_exp_gate[0], w_exp_up[0], w_exp_down[0]))

    rank0 = tab[:, :N_CLASSES, 0].astype(I32)
    cnt = tab[:, :N_CLASSES, 1].astype(I32)
    lsrc = tab[:, :N_CLASSES, 2].astype(I32) + (jnp.arange(n_windows, dtype=I32) * WIN)[:, None]
    total = rank0[-1] + cnt[-1]
    padded = ((total + TM - 1) // TM) * TM
    ids = jnp.arange(N_CLASSES, dtype=I32)
    ends = jnp.sum(jnp.where(ids[None, :] <= ids[:, None], padded[None, :], 0), axis=1)
    offs = ends - padded
    all_rows = jnp.sum(padded)
    n_tiles = (n_tok + N_CLASSES * (TM - 1)) // TM + 2
    tile_start = jnp.arange(n_tiles, dtype=I32) * TM
    used = tile_start < all_rows
    last_cls = jnp.max(jnp.where(padded > 0, ids, 0))
    tile_cls = jnp.sum((tile_start[:, None] >= ends[None, :]).astype(I32), axis=1)
    tile_cls = jnp.where(used, tile_cls, last_cls)
    of_tile = tile_cls[:, None] == ids[None, :]
    tile_r0 = tile_start - jnp.sum(jnp.where(of_tile, offs[None, :], 0), axis=1)
    tile_total = jnp.sum(jnp.where(of_tile, total[None, :], 0), axis=1)
    tile_nval = jnp.where(used, jnp.clip(tile_total - tile_r0, 0, TM), 0).astype(I32)
    pair = tile_cls % 6
    assert _PAIRS == ((0, 1), (0, 2), (0, 3), (1, 2), (1, 3), (2, 3))
    tile_lo = jnp.where(pair < 3, 0, jnp.where(pair < 5, 1, 2)).astype(I32)
    tile_hi = jnp.where(pair == 0, 1, jnp.where(jnp.logical_or(pair == 1, pair == 3), 2, 3)).astype(I32)
    run_beg = jnp.sum(jnp.where(of_tile[None], rank0[:, None, :], 0), axis=2)
    run_end = run_beg + jnp.sum(jnp.where(of_tile[None], cnt[:, None, :], 0), axis=2)
    tile_jlo = jnp.sum((run_end <= tile_r0[None, :]).astype(I32), axis=0)
    tile_jhi = jnp.sum((run_beg < (tile_r0 + tile_nval)[None, :]).astype(I32), axis=0)
    tile_jhi = jnp.maximum(tile_jhi, tile_jlo)

    h2 = _moe((tile_cls // 6, tile_lo, tile_hi, tile_nval, tile_cls, tile_r0, tile_jlo, tile_jhi),
              (rank0.reshape(-1), cnt.reshape(-1), lsrc.reshape(-1)),
              h1, row(norm_ffn_g[0]), w_rn, b_rn, w_g, w_u, w_d)

    lp = lp[:, 0:1, :]
    ple_w = (row(norm_ple_g[0]), w_ple_gate[0].astype(BF16), w_ple_proj[0].astype(BF16), row(norm_final_g))
    y_p = _ple(lp, h2, 0, p_prompt[0].reshape(n_prompt, PLE_DIM), *ple_w, name="ple_prompt")
    p_s_tm = jnp.transpose(p_sample[0].reshape(dec_batch // sblk, sblk, dec_seq, PLE_DIM), (0, 2, 1, 3))
    y_s = _ple(lp, h2, n_prompt // TP, p_s_tm.reshape(n_sample, PLE_DIM), *ple_w, name="ple_sample")

    y_prompt = y_p.reshape(batch, seq, D_MODEL)
    y_sample = jnp.transpose(y_s.reshape(dec_batch // sblk, dec_seq, sblk, D_MODEL), (0, 2, 1, 3))
    y_sample = y_sample.reshape(dec_batch, dec_seq, D_MODEL)
    state_conv_prompt = cst_p[:, HIST_OFF:, :][None]
    state_conv_sample = jnp.transpose(cst_s, (1, 0, 2))[None]
    state_gmlp_v_sample = jnp.transpose(v_s, (1, 0, 2))[None]
    return (y_prompt, y_sample, state_conv_prompt, state_conv_sample, state_gmlp_v_sample)
```

```python
import functools

import numpy as np
import jax
import jax.numpy as jnp
from jax import lax
from jax.experimental import pallas as pl
from jax.experimental.pallas import tpu as pltpu

F32 = jnp.float32
BF16 = jnp.bfloat16
I32 = jnp.int32

D_MODEL = 1024
C_CONV = 512
C_GMLP = 512
N_HEADS = 8
HEAD_DIM = 64
CONV_WIDTH = 31
CHUNK = 128
PLE_DIM = 256
N_GROUPS = 4
EPG = 4
N_EXPERTS = 16
D_EXPERT = 512
EPS = 1e-6
LANES = 128
SUBLANES = 8

N_CLASSES = N_GROUPS * 6
CLS_ROWS = 32
HIST = 32
HIST_OFF = HIST - (CONV_WIDTH - 1)
D_ROWS = D_MODEL // LANES
assert D_ROWS == SUBLANES
GROUP_ROWS = D_ROWS * SUBLANES

TP = 512
WIN_TILES = 4
WIN = WIN_TILES * TP
TM = 256
CONV_ROWS = 64
CONV_PITCH = 2
PIECES = tuple(TM >> b for b in range(TM.bit_length()))
LARGE_PIECE = 32

VMEM_LIMIT = 58 * 1024 * 1024

_PAIRS = ((0, 1), (0, 2), (0, 3), (1, 2), (1, 3), (2, 3))


def _rms(x, g):
    ms = jnp.mean(x * x, axis=-1, keepdims=True)
    return x * lax.rsqrt(ms + EPS) * g


def _ln(x, g, b):
    mu = jnp.mean(x, axis=-1, keepdims=True)
    xc = x - mu
    var = jnp.mean(xc * xc, axis=-1, keepdims=True)
    return xc * lax.rsqrt(var + EPS) * g + b


def _max4(v):
    return jnp.maximum(jnp.maximum(v[0], v[1]), jnp.maximum(v[2], v[3]))


def _first4(v, m):
    return jnp.where(v[0] == m, 0, jnp.where(v[1] == m, 1, jnp.where(v[2] == m, 2, 3))).astype(I32)


def _route(lt):
    gl = [lt[i:i + 1, :] for i in range(N_GROUPS)]
    g = _first4(gl, _max4(gl))
    a = []
    for j in range(EPG):
        rows = [lt[N_GROUPS + EPG * q + j:N_GROUPS + EPG * q + j + 1, :] for q in range(N_GROUPS)]
        a.append(jnp.where(g == 0, rows[0], jnp.where(g == 1, rows[1], jnp.where(g == 2, rows[2], rows[3]))))
    v1 = _max4(a)
    i1 = _first4(a, v1)
    a2 = [jnp.where(i1 == j, -jnp.inf, a[j]) for j in range(EPG)]
    i2 = _first4(a2, _max4(a2))
    lo = jnp.minimum(i1, i2)
    hi = jnp.maximum(i1, i2)
    pair = jnp.where(lo == 0, hi - 1, jnp.where(lo == 1, hi + 1, 5))
    return g * 6 + pair


def _combine_weights(logits, grp, e_lo, e_hi):
    lane = lax.broadcasted_iota(I32, logits.shape, 1)
    is_group = lane < N_GROUPS
    m = jnp.max(jnp.where(is_group, logits, -jnp.inf), axis=1, keepdims=True)
    den = jnp.sum(jnp.where(is_group, jnp.exp(logits - m), 0.0), axis=1, keepdims=True)
    base = N_GROUPS + EPG * grp
    v_lo = jnp.sum(jnp.where(lane == base + e_lo, logits, 0.0), axis=1, keepdims=True)
    v_hi = jnp.sum(jnp.where(lane == base + e_hi, logits, 0.0), axis=1, keepdims=True)
    top = jnp.maximum(v_lo, v_hi)
    p_lo = jnp.exp(v_lo - top)
    p_hi = jnp.exp(v_hi - top)
    scale = 1.0 / (den * (p_lo + p_hi))
    return p_lo * scale, p_hi * scale


def _load_token_blocks(ref, n_tok):
    return jnp.concatenate([ref[pl.ds(s, n_tok, stride=D_ROWS), :] for s in range(D_ROWS)], axis=1)


def _store_token_blocks(ref, x, n_tok):
    for s in range(D_ROWS):
        ref[pl.ds(s, n_tok, stride=D_ROWS), :] = x[:, s * LANES:(s + 1) * LANES]


def _route_tile(h, g_ffn_ref, w_rt_ref, b_r_ref, stage_h, cls_scr, sub):
    c = _rms(h, g_ffn_ref[...]).astype(BF16)
    lt = lax.dot_general(w_rt_ref[...], c, (((1,), (1,)), ((), ())), preferred_element_type=F32)
    cls_scr[sub] = jnp.broadcast_to(_route(lt + b_r_ref[...]), (SUBLANES, TP))
    ng = TP // SUBLANES
    for s in range(D_ROWS):
        stage_h[sub, :, s * SUBLANES:(s + 1) * SUBLANES, :] = h[:, s * LANES:(s + 1) * LANES].reshape(
            ng, SUBLANES, LANES)


def _sort_window(tri_ref, carry_ref, h1_ref, lp_ref, tab_ref, stage_h, cls_scr, lp_vmem, lp_smem, n_sub):
    rows = lax.broadcasted_iota(I32, (CLS_ROWS, TP), 0)
    onehot, counts = [], []
    lstart = jnp.zeros((CLS_ROWS, 1), F32)
    if n_sub < WIN_TILES:
        lp_ref[0, :, n_sub * TP:] = jnp.zeros((SUBLANES, (WIN_TILES - n_sub) * TP), I32)
        lp_vmem[:, n_sub * TP:] = jnp.zeros((SUBLANES, (WIN_TILES - n_sub) * TP), I32)
    for s in range(n_sub):
        cls = cls_scr[s, 0:1, :]
        onehot.append(jnp.where(rows == cls, 1.0, 0.0).astype(F32))
        counts.append(jnp.sum(onehot[s], axis=1, keepdims=True))
        lstart = lstart + jnp.sum(jnp.where(cls < rows, 1.0, 0.0).astype(F32), axis=1, keepdims=True)
    base = lstart
    for s in range(n_sub):
        pre = jnp.dot(onehot[s].astype(BF16), tri_ref[...], preferred_element_type=F32)
        lpos = jnp.sum(onehot[s] * (pre - 1.0 + base), axis=0, keepdims=True).astype(I32)
        lp_ref[0, :, s * TP:(s + 1) * TP] = jnp.broadcast_to(lpos, (SUBLANES, TP))
        lp_vmem[:, s * TP:(s + 1) * TP] = jnp.broadcast_to(lpos, (SUBLANES, TP))
        base = base + counts[s]
    tot = base - lstart
    carry = carry_ref[:, 0:1]
    lane = lax.broadcasted_iota(I32, (CLS_ROWS, LANES), 1)
    tab_ref[0] = jnp.where(lane == 0, carry, jnp.where(lane == 1, tot, jnp.where(lane == 2, lstart, 0.0)))
    carry_ref[...] = jnp.broadcast_to(carry + tot, (CLS_ROWS, LANES))
    pltpu.sync_copy(lp_vmem.at[pl.ds(0, 1)], lp_smem)

    for s in range(n_sub):
        def group(g, carry_, s=s):
            for k in range(SUBLANES):
                dst = pl.multiple_of(lp_smem[0, s * TP + g * SUBLANES + k] * D_ROWS, D_ROWS)
                h1_ref[pl.ds(dst, D_ROWS), :] = stage_h[s, g, pl.ds(k, D_ROWS, stride=SUBLANES), :]
            return carry_

        lax.fori_loop(0, TP // SUBLANES, group, 0, unroll=2)


def _mixer_front(x, g_mix_ref, w_in_ref):
    a = _rms(x, g_mix_ref[...]).astype(BF16)
    z = jnp.dot(a, w_in_ref[...], preferred_element_type=F32)
    glu = z[:, 0:C_CONV] * jax.nn.sigmoid(z[:, C_CONV:2 * C_CONV])
    u = jax.nn.gelu(z[:, 2 * C_CONV:2 * C_CONV + C_GMLP])
    gv = jax.nn.gelu(z[:, 2 * C_CONV + C_GMLP:])
    return glu, u, gv


def _mixer_body(xp_ref, xs_ref, hist_ref, g_mix_ref, w_in_ref, cw_ref, cb_ref, clg_ref, clb_ref, glg_ref, glb_ref,
                ws_ref, bs_ref, wsl_ref, bsl_ref, w_out_ref, g_ffn_ref, w_rt_ref, b_r_ref, tri_ref,
                eg_ref, eu_ref, ed_ref,
                h1_ref, lp_ref, tab_ref, cstp_ref, csts_ref, vs_ref, egb_ref, eub_ref, edb_ref,
                glu_scr, xs_scr, cat_scr, carry_ref, stage_h, cls_scr, lp_vmem, lp_smem,
                *, n_prompt_tiles, nj, dec_seq, sblk, n_tail_tiles):
    i = pl.program_id(0)
    sub = lax.rem(i, WIN_TILES)

    egb_ref[...] = eg_ref[...].astype(BF16)
    eub_ref[...] = eu_ref[...].astype(BF16)
    edb_ref[...] = ed_ref[...].astype(BF16)

    @pl.when(i == 0)
    def _():
        carry_ref[...] = jnp.zeros_like(carry_ref)

    def tail(x):
        h = x + jnp.dot(cat_scr[...], w_out_ref[...], preferred_element_type=F32)
        _route_tile(h, g_ffn_ref, w_rt_ref, b_r_ref, stage_h, cls_scr, sub)

    @pl.when(i < n_prompt_tiles)
    def _prompt():
        j = lax.rem(i, nj)
        x = xp_ref[0]
        glu, u, gv = _mixer_front(x, g_mix_ref, w_in_ref)

        def time_rows(t0, n):
            return pl.ds(CONV_PITCH * t0, n, stride=CONV_PITCH)

        @pl.when(j == 0)
        def _():
            for s in range(C_CONV // LANES):
                glu_scr[s, time_rows(0, HIST), :] = jnp.zeros((HIST, LANES), F32)

        @pl.when(j > 0)
        def _():
            for s in range(C_CONV // LANES):
                glu_scr[s, time_rows(0, HIST), :] = glu_scr[s, time_rows(TP, HIST), :]

        for s in range(C_CONV // LANES):
            glu_scr[s, time_rows(HIST, TP), :] = glu[:, s * LANES:(s + 1) * LANES]
        cstp_ref[0] = jnp.concatenate([glu_scr[s, time_rows(TP, HIST), :] for s in range(C_CONV // LANES)], axis=1)

        cb = cb_ref[...]
        for r0 in range(0, TP, CONV_ROWS):
            slabs = []
            for s in range(C_CONV // LANES):
                acc = None
                for k in range(CONV_WIDTH):
                    term = (cw_ref[k:k + 1, s * LANES:(s + 1) * LANES]
                            * glu_scr[s, time_rows(r0 + HIST_OFF + k, CONV_ROWS), :])
                    acc = term if acc is None else acc + term
                slabs.append(acc)
            ya = jax.nn.silu(_ln(jnp.concatenate(slabs, axis=1) + cb, clg_ref[...], clb_ref[...]))
            cat_scr[r0:r0 + CONV_ROWS, 0:C_CONV] = ya.astype(BF16)

        v = _ln(gv, glg_ref[...], glb_ref[...])
        vb = v.astype(BF16)
        lane = lax.broadcasted_iota(I32, (CHUNK, LANES), 1)
        is_lo = lane < HEAD_DIM
        zero = jnp.zeros((CHUNK, LANES), BF16)
        for c in range(TP // CHUNK):
            for p in range(N_HEADS // 2):
                blk = vb[c * CHUNK:(c + 1) * CHUNK, p * LANES:(p + 1) * LANES]
                rhs = jnp.concatenate([jnp.where(is_lo, blk, zero), jnp.where(is_lo, zero, blk)], axis=0)
                mixed = jnp.dot(ws_ref[p], rhs, preferred_element_type=F32) + bs_ref[:, p * LANES:(p + 1) * LANES]
                yb = u[c * CHUNK:(c + 1) * CHUNK, p * LANES:(p + 1) * LANES] * mixed
                cat_scr[c * CHUNK:(c + 1) * CHUNK, C_CONV + p * LANES:C_CONV + (p + 1) * LANES] = yb.astype(BF16)
        tail(x)

    @pl.when(i >= n_prompt_tiles)
    def _sample():
        x = xs_ref[...].reshape(TP, D_MODEL)
        glu, u, gv = _mixer_front(x, g_mix_ref, w_in_ref)

        nh = CONV_WIDTH - 1
        xs_scr[0:nh] = hist_ref[...]
        xs_scr[nh:nh + dec_seq] = glu.reshape(dec_seq, sblk, C_CONV)
        csts_ref[...] = xs_scr[dec_seq:dec_seq + nh]

        cb = cb_ref[...]
        for t in range(dec_seq):
            acc = jnp.zeros((sblk, C_CONV), F32)
            for k in range(CONV_WIDTH):
                acc = acc + cw_ref[k:k + 1, :] * xs_scr[t + k]
            ya = jax.nn.silu(_ln(acc + cb, clg_ref[...], clb_ref[...]))
            cat_scr[t * sblk:(t + 1) * sblk, 0:C_CONV] = ya.astype(BF16)

        v = _ln(gv, glg_ref[...], glb_ref[...])
        vs_ref[...] = v.reshape(dec_seq, sblk, C_GMLP)
        for t in range(dec_seq):
            mixed = jnp.broadcast_to(bsl_ref[t:t + 1, :], (sblk, C_GMLP))
            for tp in range(t + 1):
                r = t * dec_seq + tp
                mixed = mixed + wsl_ref[r:r + 1, :] * v[tp * sblk:(tp + 1) * sblk, :]
            yb = u[t * sblk:(t + 1) * sblk, :] * mixed
            cat_scr[t * sblk:(t + 1) * sblk, C_CONV:] = yb.astype(BF16)
        tail(x)

    sort_args = (tri_ref, carry_ref, h1_ref, lp_ref, tab_ref, stage_h, cls_scr, lp_vmem, lp_smem)

    @pl.when(sub == WIN_TILES - 1)
    def _():
        _sort_window(*sort_args, WIN_TILES)

    if n_tail_tiles:
        @pl.when(i == pl.num_programs(0) - 1)
        def _():
            _sort_window(*sort_args, n_tail_tiles)


def _for_pieces(ln, fn):
    def emit(sizes):
        for size in sizes:
            shift = size.bit_length()
            off = lax.shift_left(lax.shift_right_logical(ln, shift), shift)

            @pl.when((ln & size) != 0)
            def _(off=off, size=size):
                fn(off, size)

    large = tuple(s for s in PIECES if s >= LARGE_PIECE)

    @pl.when(ln >= LARGE_PIECE)
    def _():
        emit(large)

    emit(tuple(s for s in PIECES if s < LARGE_PIECE))


def _for_runs(k, tcls_ref, r0_ref, nval_ref, jlo_ref, jhi_ref, rank0_ref, cnt_ref, lsrc_ref, fn):
    c = tcls_ref[k]
    ra = r0_ref[k]
    rb = ra + nval_ref[k]

    def body(j, carry):
        idx = j * N_CLASSES + c
        s = rank0_ref[idx]
        lo = jnp.maximum(s, ra)
        hi = jnp.minimum(s + cnt_ref[idx], rb)
        ln = hi - lo

        @pl.when(ln > 0)
        def _():
            src = lsrc_ref[idx] + (lo - s)
            dst = lo - ra
            _for_pieces(ln, lambda off, size: fn(src + off, dst + off, size))

        return carry

    lax.fori_loop(jlo_ref[k], jhi_ref[k], body, 0)


def _blocks(ref, first, n):
    start = first * D_ROWS if isinstance(first, int) else pl.multiple_of(first * D_ROWS, D_ROWS)
    return ref.at[pl.ds(start, n * D_ROWS), :]


def _moe_body(grp_ref, lo_ref, hi_ref, nval_ref, tcls_ref, r0_ref, jlo_ref, jhi_ref, rank0_ref, cnt_ref, lsrc_ref,
              h1_hbm, g_ffn_ref, w_r_ref, b_r_ref, wg_ref, wu_ref, wd_ref,
              h2_hbm, hbuf, obuf, gsem, ssem):
    k = pl.program_id(0)
    n = pl.num_programs(0)
    slot = lax.rem(k, 2)
    tables = (tcls_ref, r0_ref, nval_ref, jlo_ref, jhi_ref, rank0_ref, cnt_ref, lsrc_ref)

    def gather_tile(tile, to_slot):
        def copy(src, dst, size):
            pltpu.make_async_copy(_blocks(h1_hbm, src, size), _blocks(hbuf.at[to_slot], dst, size),
                                  gsem.at[to_slot]).start()
        _for_runs(tile, *tables, copy)

    def wait_rows(n_rows, make_copy):
        _for_pieces(n_rows, lambda off, size: make_copy(size).wait())

    @pl.when(k == 0)
    def _():
        hbuf[...] = jnp.zeros_like(hbuf)
        gather_tile(0, 0)

    @pl.when(jnp.logical_and(k >= 2, nval_ref[jnp.maximum(k - 2, 0)] > 0))
    def _():
        wait_rows(nval_ref[jnp.maximum(k - 2, 0)],
                  lambda size: pltpu.make_async_copy(_blocks(obuf.at[slot], 0, size), _blocks(h2_hbm, 0, size),
                                                     ssem.at[slot]))

    nxt = jnp.minimum(k + 1, n - 1)

    @pl.when(jnp.logical_and(k + 1 < n, nval_ref[nxt] > 0))
    def _():
        gather_tile(nxt, 1 - slot)

    @pl.when(nval_ref[k] > 0)
    def _():
        wait_rows(nval_ref[k],
                  lambda size: pltpu.make_async_copy(_blocks(h1_hbm, 0, size), _blocks(hbuf.at[slot], 0, size),
                                                     gsem.at[slot]))
        e_lo = lo_ref[k]
        e_hi = hi_ref[k]

        def experts(m):
            h = _load_token_blocks(hbuf.at[slot], m)
            c = _rms(h, g_ffn_ref[...]).astype(BF16)
            logits = jnp.dot(c, w_r_ref[...], preferred_element_type=F32) + b_r_ref[...]
            wl, wh = _combine_weights(logits, grp_ref[k], e_lo, e_hi)

            def hidden(e):
                gate = jnp.dot(c, wg_ref[e], preferred_element_type=F32)
                return (jax.nn.silu(gate) * jnp.dot(c, wu_ref[e], preferred_element_type=F32)).astype(BF16)

            yl = jnp.dot(hidden(e_lo), wd_ref[e_lo], preferred_element_type=F32)
            yh = jnp.dot(hidden(e_hi), wd_ref[e_hi], preferred_element_type=F32)
            _store_token_blocks(obuf.at[slot], h + (wl * yl + wh * yh), m)

        @pl.when(nval_ref[k] > TM // 2)
        def _():
            experts(TM)

        @pl.when(nval_ref[k] <= TM // 2)
        def _():
            experts(TM // 2)

        def copy_back(src, dst, size):
            pltpu.make_async_copy(_blocks(obuf.at[slot], dst, size), _blocks(h2_hbm, src, size),
                                  ssem.at[slot]).start()
        _for_runs(k, *tables, copy_back)


def _ple_body(lp_ref, h2_ref, p_ref, g_ple_ref, w_gate_ref, w_proj_ref, g_fin_ref, y_ref, stage_h, *, tile0):
    ng = TP // SUBLANES
    first = lax.rem(tile0 + pl.program_id(0), WIN_TILES) * TP

    def group(g, carry):
        for k in range(SUBLANES):
            src = pl.multiple_of(lp_ref[0, 0, first + g * SUBLANES + k] * D_ROWS, D_ROWS)
            stage_h[g, pl.ds(k, D_ROWS, stride=SUBLANES), :] = h2_ref[pl.ds(src, D_ROWS), :]
        return carry

    lax.fori_loop(0, ng, group, 0, unroll=2)
    h = jnp.concatenate([stage_h[:, s * SUBLANES:(s + 1) * SUBLANES, :].reshape(TP, LANES) for s in range(D_ROWS)],
                        axis=1)
    c = _rms(h, g_ple_ref[...]).astype(BF16)
    gate = jax.nn.sigmoid(jnp.dot(c, w_gate_ref[...], preferred_element_type=F32))
    proj = jnp.dot(p_ref[...].astype(BF16), w_proj_ref[...], preferred_element_type=F32)
    h = h + proj * gate
    y_ref[...] = _rms(h, g_fin_ref[...])


def _full(shape, single=False):
    nd = len(shape)
    if single:
        return pl.BlockSpec(shape, lambda *_: (0,) * nd, pipeline_mode=pl.Buffered(1))
    return pl.BlockSpec(shape, lambda *_: (0,) * nd)


def _mixer(x_prompt, x_tm, hist_tm, weights, tri, expert_w):
    batch, seq, _ = x_prompt.shape
    dec_seq, dec_batch, _ = x_tm.shape
    nj = seq // TP
    npt = batch * nj
    sblk = TP // dec_seq
    nst = dec_batch // sblk
    n_tiles = npt + nst
    n_tail = n_tiles % WIN_TILES
    assert npt % WIN_TILES == 0 and n_tail <= nst
    n_windows = pl.cdiv(n_tiles, WIN_TILES)
    nh = CONV_WIDTH - 1
    w_specs = [_full(w.shape, single=True) for w in weights]
    body = functools.partial(_mixer_body, n_prompt_tiles=npt, nj=nj, dec_seq=dec_seq, sblk=sblk, n_tail_tiles=n_tail)

    def p_idx(i):
        return jnp.minimum(i, npt - 1)

    def s_idx(i):
        return jnp.maximum(i - npt, 0)

    assert npt % N_EXPERTS == 0
    parts = npt // N_EXPERTS
    e_specs = [pl.BlockSpec((1, w.shape[1] // parts, w.shape[2]), lambda i: (p_idx(i) // parts, p_idx(i) % parts, 0))
               for w in expert_w]
    e_shapes = [jax.ShapeDtypeStruct(w.shape, BF16) for w in expert_w]

    return pl.pallas_call(
        body,
        grid=(n_tiles,),
        in_specs=[pl.BlockSpec((1, TP, D_MODEL), lambda i: (p_idx(i) // nj, p_idx(i) % nj, 0)),
                  pl.BlockSpec((dec_seq, sblk, D_MODEL), lambda i: (0, s_idx(i), 0), pipeline_mode=pl.Buffered(1)),
                  pl.BlockSpec((nh, sblk, C_CONV), lambda i: (0, s_idx(i), 0), pipeline_mode=pl.Buffered(1))]
        + w_specs + [_full(tri.shape, single=True)] + e_specs,
        out_specs=[
            pl.BlockSpec((WIN * D_ROWS, LANES), lambda i: (i // WIN_TILES, 0), pipeline_mode=pl.Buffered(1)),
            pl.BlockSpec((1, SUBLANES, WIN), lambda i: (i // WIN_TILES, 0, 0)),
            pl.BlockSpec((1, CLS_ROWS, LANES), lambda i: (i // WIN_TILES, 0, 0)),
            pl.BlockSpec((1, HIST, C_CONV), lambda i: (p_idx(i) // nj, 0, 0)),
            pl.BlockSpec((nh, sblk, C_CONV), lambda i: (0, s_idx(i), 0), pipeline_mode=pl.Buffered(1)),
            pl.BlockSpec((dec_seq, sblk, C_GMLP), lambda i: (0, s_idx(i), 0), pipeline_mode=pl.Buffered(1)),
        ] + e_specs,
        out_shape=[
            jax.ShapeDtypeStruct((n_tiles * TP * D_ROWS, LANES), F32),
            jax.ShapeDtypeStruct((n_windows, SUBLANES, WIN), I32),
            jax.ShapeDtypeStruct((n_windows, CLS_ROWS, LANES), F32),
            jax.ShapeDtypeStruct((batch, HIST, C_CONV), F32),
            jax.ShapeDtypeStruct((nh, dec_batch, C_CONV), F32),
            jax.ShapeDtypeStruct((dec_seq, dec_batch, C_GMLP), F32),
        ] + e_shapes,
        scratch_shapes=[
            pltpu.VMEM((C_CONV // LANES, CONV_PITCH * (HIST + TP), LANES), F32),
            pltpu.VMEM((nh + dec_seq, sblk, C_CONV), F32),
            pltpu.VMEM((TP, D_MODEL), BF16),
            pltpu.VMEM((CLS_ROWS, LANES), F32),
            pltpu.VMEM((WIN_TILES, TP // SUBLANES, GROUP_ROWS, LANES), F32),
            pltpu.VMEM((WIN_TILES, SUBLANES, TP), I32),
            pltpu.VMEM((SUBLANES, WIN), I32),
            pltpu.SMEM((1, WIN), I32),
        ],
        compiler_params=pltpu.CompilerParams(
            dimension_semantics=("arbitrary",), vmem_limit_bytes=VMEM_LIMIT),
        name="mixer",
    )(x_prompt, x_tm, hist_tm, *weights, tri, *expert_w)


def _moe(tile_tabs, run_tabs, h1, g_ffn, w_r, b_r, w_g, w_u, w_d):
    n_tiles = tile_tabs[0].shape[0]
    n_prefetch = len(tile_tabs) + len(run_tabs)
    grid_spec = pltpu.PrefetchScalarGridSpec(
        num_scalar_prefetch=n_prefetch,
        grid=(n_tiles,),
        in_specs=[
            pl.BlockSpec(memory_space=pl.ANY),
            pl.BlockSpec((1, D_MODEL), lambda i, *_: (0, 0)),
            pl.BlockSpec((D_MODEL, LANES), lambda i, *_: (0, 0)),
            pl.BlockSpec((1, LANES), lambda i, *_: (0, 0)),
            pl.BlockSpec((EPG, D_MODEL, D_EXPERT), lambda i, grp, *_: (grp[i], 0, 0)),
            pl.BlockSpec((EPG, D_MODEL, D_EXPERT), lambda i, grp, *_: (grp[i], 0, 0)),
            pl.BlockSpec((EPG, D_EXPERT, D_MODEL), lambda i, grp, *_: (grp[i], 0, 0)),
        ],
        out_specs=pl.BlockSpec(memory_space=pl.ANY),
        scratch_shapes=[pltpu.VMEM((2, TM * D_ROWS, LANES), F32),
                        pltpu.VMEM((2, TM * D_ROWS, LANES), F32),
                        pltpu.SemaphoreType.DMA((2,)),
                        pltpu.SemaphoreType.DMA((2,))],
    )
    return pl.pallas_call(
        _moe_body,
        grid_spec=grid_spec,
        out_shape=jax.ShapeDtypeStruct(h1.shape, F32),
        compiler_params=pltpu.CompilerParams(
            dimension_semantics=("arbitrary",), vmem_limit_bytes=VMEM_LIMIT),
        name="moe",
    )(*tile_tabs, *run_tabs, h1, g_ffn, w_r, b_r, w_g, w_u, w_d)


def _ple(lp, h2, tile0, p, g_ple, w_gate, w_proj, g_fin, name):
    n_tiles = p.shape[0] // TP
    return pl.pallas_call(
        functools.partial(_ple_body, tile0=tile0),
        grid=(n_tiles,),
        in_specs=[
            pl.BlockSpec((1, 1, WIN), lambda i: ((tile0 + i) // WIN_TILES, 0, 0), memory_space=pltpu.SMEM),
            pl.BlockSpec((WIN * D_ROWS, LANES), lambda i: ((tile0 + i) // WIN_TILES, 0)),
            pl.BlockSpec((TP, PLE_DIM), lambda i: (i, 0)),
            _full(g_ple.shape), _full(w_gate.shape), _full(w_proj.shape), _full(g_fin.shape),
        ],
        out_specs=pl.BlockSpec((TP, D_MODEL), lambda i: (i, 0)),
        out_shape=jax.ShapeDtypeStruct((n_tiles * TP, D_MODEL), F32),
        scratch_shapes=[pltpu.VMEM((TP // SUBLANES, GROUP_ROWS, LANES), F32)],
        compiler_params=pltpu.CompilerParams(
            dimension_semantics=("arbitrary",), vmem_limit_bytes=VMEM_LIMIT),
        name=name,
    )(lp, h2, p, g_ple, w_gate, w_proj, g_fin)


def kernel(x_prompt, x_sample, p_prompt, p_sample, state_conv, norm_mix_g, w_in, conv_w, conv_b, conv_ln_g, conv_ln_b, gmlp_ln_g, gmlp_ln_b, w_s, b_s, w_out, norm_ffn_g, w_router_group, b_router_group, w_router_expert, b_router_expert, w_exp_gate, w_exp_up, w_exp_down, norm_ple_g, w_ple_gate, w_ple_proj, norm_final_g):
    depth = w_in.shape[0]
    assert depth == 1, "single-layer pipeline"
    batch, seq, _ = x_prompt.shape
    dec_batch, dec_seq, _ = x_sample.shape
    assert seq % TP == 0 and TP % CHUNK == 0 and TP % dec_seq == 0 and dec_batch % (TP // dec_seq) == 0
    sblk = TP // dec_seq
    n_prompt = batch * seq
    n_sample = dec_batch * dec_seq
    n_tok = n_prompt + n_sample

    row = lambda a: a.reshape(1, -1)
    w_in_b = w_in[0].astype(BF16)
    w_out_b = w_out[0].astype(BF16)
    cw = jnp.concatenate([conv_w[0], jnp.zeros((1, C_CONV), F32)], axis=0)
    tril = jnp.asarray(np.tril(np.ones((CHUNK, CHUNK), bool)))
    ws_m = jnp.where(tril[None], w_s[0], 0.0)
    ws_cat = jnp.concatenate([ws_m[0::2], ws_m[1::2]], axis=2).astype(BF16)
    bs_lane = jnp.repeat(jnp.transpose(b_s[0]), HEAD_DIM, axis=1)
    w_rt = jnp.zeros((CLS_ROWS, D_MODEL), F32)
    w_rt = w_rt.at[0:N_GROUPS].set(jnp.transpose(w_router_group[0]))
    w_rt = w_rt.at[N_GROUPS:N_GROUPS + N_EXPERTS].set(jnp.transpose(w_router_expert[0])).astype(BF16)
    b_r = jnp.zeros((CLS_ROWS, 1), F32)
    b_r = b_r.at[0:N_GROUPS, 0].set(b_router_group[0]).at[N_GROUPS:N_GROUPS + N_EXPERTS, 0].set(b_router_expert[0])
    n_logit = N_GROUPS + N_EXPERTS
    w_rn = jnp.concatenate([w_router_group[0], w_router_expert[0], jnp.zeros((D_MODEL, LANES - n_logit), F32)],
                           axis=1).astype(BF16)
    b_rn = jnp.concatenate([b_router_group[0], b_router_expert[0], jnp.zeros((LANES - n_logit,), F32)]).reshape(1, LANES)
    tri = jnp.asarray(np.triu(np.ones((TP, TP), np.float32)), dtype=BF16)
    wsl = jnp.where(jnp.asarray(np.tril(np.ones((dec_seq, dec_seq), bool)))[None], w_s[0][:, :dec_seq, :dec_seq], 0.0)
    wsl = jnp.repeat(jnp.transpose(wsl, (1, 2, 0)).reshape(dec_seq * dec_seq, N_HEADS), HEAD_DIM, axis=1)
    bsl = jnp.repeat(jnp.transpose(b_s[0][:, :dec_seq]), HEAD_DIM, axis=1)

    weights = (row(norm_mix_g[0]), w_in_b, cw, row(conv_b[0]), row(conv_ln_g[0]), row(conv_ln_b[0]),
               row(gmlp_ln_g[0]), row(gmlp_ln_b[0]), ws_cat, bs_lane, wsl, bsl,
               w_out_b, row(norm_ffn_g[0]), w_rt, b_r)

    x_tm = jnp.transpose(x_sample, (1, 0, 2))
    hist_tm = jnp.transpose(state_conv[0], (1, 0, 2))
    h1, lp, tab, cst_p, cst_s, v_s, w_g, w_u, w_d = _mixer(
        x_prompt, x_tm, hist_tm, weights, tri, (w_exp_gate[0], w_exp_up[0], w_exp_down[0]))

    rank0 = tab[:, :N_CLASSES, 0].astype(I32)
    cnt = tab[:, :N_CLASSES, 1].astype(I32)
    lsrc = tab[:, :N_CLASSES, 2].astype(I32) + (jnp.arange(tab.shape[0], dtype=I32) * WIN)[:, None]
    total = rank0[-1] + cnt[-1]
    padded = ((total + TM - 1) // TM) * TM
    ids = jnp.arange(N_CLASSES, dtype=I32)
    ends = jnp.sum(jnp.where(ids[None, :] <= ids[:, None], padded[None, :], 0), axis=1)
    offs = ends - padded
    all_rows = jnp.sum(padded)
    n_tiles = (n_tok + N_CLASSES * (TM - 1)) // TM + 2
    tile_start = jnp.arange(n_tiles, dtype=I32) * TM
    used = tile_start < all_rows
    last_cls = jnp.max(jnp.where(padded > 0, ids, 0))
    tile_cls = jnp.sum((tile_start[:, None] >= ends[None, :]).astype(I32), axis=1)
    tile_cls = jnp.where(used, tile_cls, last_cls)
    of_tile = tile_cls[:, None] == ids[None, :]
    tile_r0 = tile_start - jnp.sum(jnp.where(of_tile, offs[None, :], 0), axis=1)
    tile_total = jnp.sum(jnp.where(of_tile, total[None, :], 0), axis=1)
    tile_nval = jnp.where(used, jnp.clip(tile_total - tile_r0, 0, TM), 0).astype(I32)
    pair = tile_cls % 6
    assert _PAIRS == ((0, 1), (0, 2), (0, 3), (1, 2), (1, 3), (2, 3))
    tile_lo = jnp.where(pair < 3, 0, jnp.where(pair < 5, 1, 2)).astype(I32)
    tile_hi = jnp.where(pair == 0, 1, jnp.where(jnp.logical_or(pair == 1, pair == 3), 2, 3)).astype(I32)
    run_beg = jnp.sum(jnp.where(of_tile[None], rank0[:, None, :], 0), axis=2)
    run_end = run_beg + jnp.sum(jnp.where(of_tile[None], cnt[:, None, :], 0), axis=2)
    tile_jlo = jnp.sum((run_end <= tile_r0[None, :]).astype(I32), axis=0)
    tile_jhi = jnp.sum((run_beg < (tile_r0 + tile_nval)[None, :]).astype(I32), axis=0)
    tile_jhi = jnp.maximum(tile_jhi, tile_jlo)

    h2 = _moe((tile_cls // 6, tile_lo, tile_hi, tile_nval, tile_cls, tile_r0, tile_jlo, tile_jhi),
              (rank0.reshape(-1), cnt.reshape(-1), lsrc.reshape(-1)),
              h1, row(norm_ffn_g[0]), w_rn, b_rn, w_g, w_u, w_d)

    lp = lp[:, 0:1, :]
    ple_w = (row(norm_ple_g[0]), w_ple_gate[0].astype(BF16), w_ple_proj[0].astype(BF16), row(norm_final_g))
    y_p = _ple(lp, h2, 0, p_prompt[0].reshape(n_prompt, PLE_DIM), *ple_w, name="ple_prompt")
    p_s_tm = jnp.transpose(p_sample[0].reshape(dec_batch // sblk, sblk, dec_seq, PLE_DIM), (0, 2, 1, 3))
    y_s = _ple(lp, h2, n_prompt // TP, p_s_tm.reshape(n_sample, PLE_DIM), *ple_w, name="ple_sample")

    y_prompt = y_p.reshape(batch, seq, D_MODEL)
    y_sample = jnp.transpose(y_s.reshape(dec_batch // sblk, dec_seq, sblk, D_MODEL), (0, 2, 1, 3))
    y_sample = y_sample.reshape(dec_batch, dec_seq, D_MODEL)
    state_conv_prompt = cst_p[:, HIST_OFF:, :][None]
    state_conv_sample = jnp.transpose(cst_s, (1, 0, 2))[None]
    state_gmlp_v_sample = jnp.transpose(v_s, (1, 0, 2))[None]
    return (y_prompt, y_sample, state_conv_prompt, state_conv_sample, state_gmlp_v_sample)
```

```python
import functools

import numpy as np
import jax
import jax.numpy as jnp
from jax import lax
from jax.experimental import pallas as pl
from jax.experimental.pallas import tpu as pltpu

F32 = jnp.float32
BF16 = jnp.bfloat16
I32 = jnp.int32

D_MODEL = 1024
C_CONV = 512
C_GMLP = 512
N_HEADS = 8
HEAD_DIM = 64
CONV_WIDTH = 31
CHUNK = 128
PLE_DIM = 256
N_GROUPS = 4
EPG = 4
N_EXPERTS = 16
D_EXPERT = 512
EPS = 1e-6
LANES = 128
SUBLANES = 8

N_CLASSES = N_GROUPS * 6
CLS_ROWS = 32
HIST = 32
HIST_OFF = HIST - (CONV_WIDTH - 1)
D_ROWS = D_MODEL // LANES
assert D_ROWS == SUBLANES
GROUP_ROWS = D_ROWS * SUBLANES

TP = 512
WIN_TILES = 4
WIN = WIN_TILES * TP
TM = 256
CONV_ROWS = 64
CONV_PITCH = 2
PIECES = tuple(TM >> b for b in range(TM.bit_length()))
LARGE_PIECE = 32

VMEM_LIMIT = 58 * 1024 * 1024

_PAIRS = ((0, 1), (0, 2), (0, 3), (1, 2), (1, 3), (2, 3))


def _rms(x, g):
    ms = jnp.mean(x * x, axis=-1, keepdims=True)
    return x * lax.rsqrt(ms + EPS) * g


def _ln(x, g, b):
    mu = jnp.mean(x, axis=-1, keepdims=True)
    xc = x - mu
    var = jnp.mean(xc * xc, axis=-1, keepdims=True)
    return xc * lax.rsqrt(var + EPS) * g + b


def _max4(v):
    return jnp.maximum(jnp.maximum(v[0], v[1]), jnp.maximum(v[2], v[3]))


def _first4(v, m):
    return jnp.where(v[0] == m, 0, jnp.where(v[1] == m, 1, jnp.where(v[2] == m, 2, 3))).astype(I32)


def _route(lt):
    gl = [lt[i:i + 1, :] for i in range(N_GROUPS)]
    g = _first4(gl, _max4(gl))
    a = []
    for j in range(EPG):
        rows = [lt[N_GROUPS + EPG * q + j:N_GROUPS + EPG * q + j + 1, :] for q in range(N_GROUPS)]
        a.append(jnp.where(g == 0, rows[0], jnp.where(g == 1, rows[1], jnp.where(g == 2, rows[2], rows[3]))))
    v1 = _max4(a)
    i1 = _first4(a, v1)
    a2 = [jnp.where(i1 == j, -jnp.inf, a[j]) for j in range(EPG)]
    i2 = _first4(a2, _max4(a2))
    lo = jnp.minimum(i1, i2)
    hi = jnp.maximum(i1, i2)
    pair = jnp.where(lo == 0, hi - 1, jnp.where(lo == 1, hi + 1, 5))
    return g * 6 + pair


def _combine_weights(logits, grp, e_lo, e_hi):
    lane = lax.broadcasted_iota(I32, logits.shape, 1)
    is_group = lane < N_GROUPS
    m = jnp.max(jnp.where(is_group, logits, -jnp.inf), axis=1, keepdims=True)
    den = jnp.sum(jnp.where(is_group, jnp.exp(logits - m), 0.0), axis=1, keepdims=True)
    base = N_GROUPS + EPG * grp
    v_lo = jnp.sum(jnp.where(lane == base + e_lo, logits, 0.0), axis=1, keepdims=True)
    v_hi = jnp.sum(jnp.where(lane == base + e_hi, logits, 0.0), axis=1, keepdims=True)
    top = jnp.maximum(v_lo, v_hi)
    p_lo = jnp.exp(v_lo - top)
    p_hi = jnp.exp(v_hi - top)
    scale = 1.0 / (den * (p_lo + p_hi))
    return p_lo * scale, p_hi * scale


def _load_token_blocks(ref, n_tok):
    return jnp.concatenate([ref[pl.ds(s, n_tok, stride=D_ROWS), :] for s in range(D_ROWS)], axis=1)


def _store_token_blocks(ref, x, n_tok):
    for s in range(D_ROWS):
        ref[pl.ds(s, n_tok, stride=D_ROWS), :] = x[:, s * LANES:(s + 1) * LANES]


def _route_tile(h, g_ffn_ref, w_rt_ref, b_r_ref, stage_h, cls_scr, sub):
    c = _rms(h, g_ffn_ref[...]).astype(BF16)
    lt = lax.dot_general(w_rt_ref[...], c, (((1,), (1,)), ((), ())), preferred_element_type=F32)
    cls_scr[sub] = jnp.broadcast_to(_route(lt + b_r_ref[...]), (SUBLANES, TP))
    ng = TP // SUBLANES
    for s in range(D_ROWS):
        stage_h[sub, :, s * SUBLANES:(s + 1) * SUBLANES, :] = h[:, s * LANES:(s + 1) * LANES].reshape(
            ng, SUBLANES, LANES)


def _sort_window(tri_ref, carry_ref, h1_hbm, h1_buf, h1_sem, lp_ref, tab_ref, stage_h, cls_scr, lp_vmem, lp_smem,
                 w, is_last_step, n_sub):
    rows = lax.broadcasted_iota(I32, (CLS_ROWS, TP), 0)
    onehot, counts = [], []
    lstart = jnp.zeros((CLS_ROWS, 1), F32)
    if n_sub < WIN_TILES:
        lp_ref[0, :, n_sub * TP:] = jnp.zeros((SUBLANES, (WIN_TILES - n_sub) * TP), I32)
        lp_vmem[:, n_sub * TP:] = jnp.zeros((SUBLANES, (WIN_TILES - n_sub) * TP), I32)
    for s in range(n_sub):
        cls = cls_scr[s, 0:1, :]
        onehot.append(jnp.where(rows == cls, 1.0, 0.0).astype(F32))
        counts.append(jnp.sum(onehot[s], axis=1, keepdims=True))
        lstart = lstart + jnp.sum(jnp.where(cls < rows, 1.0, 0.0).astype(F32), axis=1, keepdims=True)
    base = lstart
    for s in range(n_sub):
        pre = jnp.dot(onehot[s].astype(BF16), tri_ref[...], preferred_element_type=F32)
        lpos = jnp.sum(onehot[s] * (pre - 1.0 + base), axis=0, keepdims=True).astype(I32)
        lp_ref[0, :, s * TP:(s + 1) * TP] = jnp.broadcast_to(lpos, (SUBLANES, TP))
        lp_vmem[:, s * TP:(s + 1) * TP] = jnp.broadcast_to(lpos, (SUBLANES, TP))
        base = base + counts[s]
    tot = base - lstart
    carry = carry_ref[:, 0:1]
    lane = lax.broadcasted_iota(I32, (CLS_ROWS, LANES), 1)
    tab_ref[0] = jnp.where(lane == 0, carry, jnp.where(lane == 1, tot, jnp.where(lane == 2, lstart, 0.0)))
    carry_ref[...] = jnp.broadcast_to(carry + tot, (CLS_ROWS, LANES))
    pltpu.sync_copy(lp_vmem.at[pl.ds(0, 1)], lp_smem)

    def window_copy(win, n_tiles):
        rows = n_tiles * TP * D_ROWS
        start = win * (WIN * D_ROWS)
        start = start if isinstance(start, int) else pl.multiple_of(start, D_ROWS)
        return pltpu.make_async_copy(h1_buf.at[pl.ds(0, rows), :], h1_hbm.at[pl.ds(start, rows), :], h1_sem)

    @pl.when(w > 0)
    def _():
        window_copy(0, WIN_TILES).wait()

    for s in range(n_sub):
        def group(g, carry_, s=s):
            for k in range(SUBLANES):
                dst = pl.multiple_of(lp_smem[0, s * TP + g * SUBLANES + k] * D_ROWS, D_ROWS)
                h1_buf[pl.ds(dst, D_ROWS), :] = stage_h[s, g, pl.ds(k, D_ROWS, stride=SUBLANES), :]
            return carry_

        lax.fori_loop(0, TP // SUBLANES, group, 0, unroll=2)

    window_copy(w, n_sub).start()

    @pl.when(is_last_step)
    def _():
        window_copy(w, n_sub).wait()


def _mixer_front(x, g_mix_ref, w_in_ref):
    a = _rms(x, g_mix_ref[...]).astype(BF16)
    z = jnp.dot(a, w_in_ref[...], preferred_element_type=F32)
    glu = z[:, 0:C_CONV] * jax.nn.sigmoid(z[:, C_CONV:2 * C_CONV])
    u = jax.nn.gelu(z[:, 2 * C_CONV:2 * C_CONV + C_GMLP])
    gv = jax.nn.gelu(z[:, 2 * C_CONV + C_GMLP:])
    return glu, u, gv


def _mixer_body(xp_ref, xs_ref, hist_ref, g_mix_ref, w_in_ref, cw_ref, cb_ref, clg_ref, clb_ref, glg_ref, glb_ref,
                ws_ref, bs_ref, wsl_ref, bsl_ref, w_out_ref, g_ffn_ref, w_rt_ref, b_r_ref, tri_ref,
                eg_ref, eu_ref, ed_ref,
                h1_hbm, lp_ref, tab_ref, cstp_ref, csts_ref, vs_ref, egb_ref, eub_ref, edb_ref,
                glu_scr, xs_scr, cat_scr, carry_ref, stage_h, cls_scr, lp_vmem, lp_smem, h1_buf, h1_sem,
                *, n_prompt_tiles, nj, dec_seq, sblk, n_tail_tiles):
    i = pl.program_id(0)
    sub = lax.rem(i, WIN_TILES)

    egb_ref[...] = eg_ref[...].astype(BF16)
    eub_ref[...] = eu_ref[...].astype(BF16)
    edb_ref[...] = ed_ref[...].astype(BF16)

    @pl.when(i == 0)
    def _():
        carry_ref[...] = jnp.zeros_like(carry_ref)

    def tail(x):
        h = x + jnp.dot(cat_scr[...], w_out_ref[...], preferred_element_type=F32)
        _route_tile(h, g_ffn_ref, w_rt_ref, b_r_ref, stage_h, cls_scr, sub)

    @pl.when(i < n_prompt_tiles)
    def _prompt():
        j = lax.rem(i, nj)
        x = xp_ref[0]
        glu, u, gv = _mixer_front(x, g_mix_ref, w_in_ref)

        def time_rows(t0, n):
            return pl.ds(CONV_PITCH * t0, n, stride=CONV_PITCH)

        @pl.when(j == 0)
        def _():
            for s in range(C_CONV // LANES):
                glu_scr[s, time_rows(0, HIST), :] = jnp.zeros((HIST, LANES), F32)

        @pl.when(j > 0)
        def _():
            for s in range(C_CONV // LANES):
                glu_scr[s, time_rows(0, HIST), :] = glu_scr[s, time_rows(TP, HIST), :]

        for s in range(C_CONV // LANES):
            glu_scr[s, time_rows(HIST, TP), :] = glu[:, s * LANES:(s + 1) * LANES]
        cstp_ref[0] = jnp.concatenate([glu_scr[s, time_rows(TP, HIST), :] for s in range(C_CONV // LANES)], axis=1)

        cb = cb_ref[...]
        for r0 in range(0, TP, CONV_ROWS):
            slabs = []
            for s in range(C_CONV // LANES):
                acc = None
                for k in range(CONV_WIDTH):
                    term = (cw_ref[k:k + 1, s * LANES:(s + 1) * LANES]
                            * glu_scr[s, time_rows(r0 + HIST_OFF + k, CONV_ROWS), :])
                    acc = term if acc is None else acc + term
                slabs.append(acc)
            ya = jax.nn.silu(_ln(jnp.concatenate(slabs, axis=1) + cb, clg_ref[...], clb_ref[...]))
            cat_scr[r0:r0 + CONV_ROWS, 0:C_CONV] = ya.astype(BF16)

        v = _ln(gv, glg_ref[...], glb_ref[...])
        vb = v.astype(BF16)
        lane = lax.broadcasted_iota(I32, (CHUNK, LANES), 1)
        is_lo = lane < HEAD_DIM
        zero = jnp.zeros((CHUNK, LANES), BF16)
        for c in range(TP // CHUNK):
            for p in range(N_HEADS // 2):
                blk = vb[c * CHUNK:(c + 1) * CHUNK, p * LANES:(p + 1) * LANES]
                rhs = jnp.concatenate([jnp.where(is_lo, blk, zero), jnp.where(is_lo, zero, blk)], axis=0)
                mixed = jnp.dot(ws_ref[p], rhs, preferred_element_type=F32) + bs_ref[:, p * LANES:(p + 1) * LANES]
                yb = u[c * CHUNK:(c + 1) * CHUNK, p * LANES:(p + 1) * LANES] * mixed
                cat_scr[c * CHUNK:(c + 1) * CHUNK, C_CONV + p * LANES:C_CONV + (p + 1) * LANES] = yb.astype(BF16)
        tail(x)

    @pl.when(i >= n_prompt_tiles)
    def _sample():
        x = xs_ref[...].reshape(TP, D_MODEL)
        glu, u, gv = _mixer_front(x, g_mix_ref, w_in_ref)

        nh = CONV_WIDTH - 1
        xs_scr[0:nh] = hist_ref[...]
        xs_scr[nh:nh + dec_seq] = glu.reshape(dec_seq, sblk, C_CONV)
        csts_ref[...] = xs_scr[dec_seq:dec_seq + nh]

        cb = cb_ref[...]
        for t in range(dec_seq):
            acc = jnp.zeros((sblk, C_CONV), F32)
            for k in range(CONV_WIDTH):
                acc = acc + cw_ref[k:k + 1, :] * xs_scr[t + k]
            ya = jax.nn.silu(_ln(acc + cb, clg_ref[...], clb_ref[...]))
            cat_scr[t * sblk:(t + 1) * sblk, 0:C_CONV] = ya.astype(BF16)

        v = _ln(gv, glg_ref[...], glb_ref[...])
        vs_ref[...] = v.reshape(dec_seq, sblk, C_GMLP)
        for t in range(dec_seq):
            mixed = jnp.broadcast_to(bsl_ref[t:t + 1, :], (sblk, C_GMLP))
            for tp in range(t + 1):
                r = t * dec_seq + tp
                mixed = mixed + wsl_ref[r:r + 1, :] * v[tp * sblk:(tp + 1) * sblk, :]
            yb = u[t * sblk:(t + 1) * sblk, :] * mixed
            cat_scr[t * sblk:(t + 1) * sblk, C_CONV:] = yb.astype(BF16)
        tail(x)

    is_last_step = i == pl.num_programs(0) - 1
    sort_args = (tri_ref, carry_ref, h1_hbm, h1_buf, h1_sem, lp_ref, tab_ref, stage_h, cls_scr, lp_vmem, lp_smem,
                 i // WIN_TILES, is_last_step)

    @pl.when(sub == WIN_TILES - 1)
    def _():
        _sort_window(*sort_args, WIN_TILES)

    if n_tail_tiles:
        @pl.when(is_last_step)
        def _():
            _sort_window(*sort_args, n_tail_tiles)


def _for_pieces(ln, fn):
    def emit(sizes):
        for size in sizes:
            shift = size.bit_length()
            off = lax.shift_left(lax.shift_right_logical(ln, shift), shift)

            @pl.when((ln & size) != 0)
            def _(off=off, size=size):
                fn(off, size)

    large = tuple(s for s in PIECES if s >= LARGE_PIECE)

    @pl.when(ln >= LARGE_PIECE)
    def _():
        emit(large)

    emit(tuple(s for s in PIECES if s < LARGE_PIECE))


def _for_runs(k, tcls_ref, r0_ref, nval_ref, jlo_ref, jhi_ref, rank0_ref, cnt_ref, lsrc_ref, fn):
    c = tcls_ref[k]
    ra = r0_ref[k]
    rb = ra + nval_ref[k]

    def body(j, carry):
        idx = j * N_CLASSES + c
        s = rank0_ref[idx]
        lo = jnp.maximum(s, ra)
        hi = jnp.minimum(s + cnt_ref[idx], rb)
        ln = hi - lo

        @pl.when(ln > 0)
        def _():
            src = lsrc_ref[idx] + (lo - s)
            dst = lo - ra
            _for_pieces(ln, lambda off, size: fn(src + off, dst + off, size))

        return carry

    lax.fori_loop(jlo_ref[k], jhi_ref[k], body, 0)


def _blocks(ref, first, n):
    start = first * D_ROWS if isinstance(first, int) else pl.multiple_of(first * D_ROWS, D_ROWS)
    return ref.at[pl.ds(start, n * D_ROWS), :]


def _moe_body(grp_ref, lo_ref, hi_ref, nval_ref, tcls_ref, r0_ref, jlo_ref, jhi_ref, rank0_ref, cnt_ref, lsrc_ref,
              h1_hbm, g_ffn_ref, w_r_ref, b_r_ref, wg_ref, wu_ref, wd_ref,
              h2_hbm, hbuf, obuf, gsem, ssem):
    k = pl.program_id(0)
    n = pl.num_programs(0)
    slot = lax.rem(k, 2)
    tables = (tcls_ref, r0_ref, nval_ref, jlo_ref, jhi_ref, rank0_ref, cnt_ref, lsrc_ref)

    def gather_tile(tile, to_slot):
        def copy(src, dst, size):
            pltpu.make_async_copy(_blocks(h1_hbm, src, size), _blocks(hbuf.at[to_slot], dst, size),
                                  gsem.at[to_slot]).start()
        _for_runs(tile, *tables, copy)

    def wait_rows(n_rows, make_copy):
        _for_pieces(n_rows, lambda off, size: make_copy(size).wait())

    @pl.when(k == 0)
    def _():
        hbuf[...] = jnp.zeros_like(hbuf)
        gather_tile(0, 0)

    @pl.when(jnp.logical_and(k >= 2, nval_ref[jnp.maximum(k - 2, 0)] > 0))
    def _():
        wait_rows(nval_ref[jnp.maximum(k - 2, 0)],
                  lambda size: pltpu.make_async_copy(_blocks(obuf.at[slot], 0, size), _blocks(h2_hbm, 0, size),
                                                     ssem.at[slot]))

    nxt = jnp.minimum(k + 1, n - 1)

    @pl.when(jnp.logical_and(k + 1 < n, nval_ref[nxt] > 0))
    def _():
        gather_tile(nxt, 1 - slot)

    @pl.when(nval_ref[k] > 0)
    def _():
        wait_rows(nval_ref[k],
                  lambda size: pltpu.make_async_copy(_blocks(h1_hbm, 0, size), _blocks(hbuf.at[slot], 0, size),
                                                     gsem.at[slot]))
        e_lo = lo_ref[k]
        e_hi = hi_ref[k]

        def experts(m):
            h = _load_token_blocks(hbuf.at[slot], m)
            c = _rms(h, g_ffn_ref[...]).astype(BF16)
            logits = jnp.dot(c, w_r_ref[...], preferred_element_type=F32) + b_r_ref[...]
            wl, wh = _combine_weights(logits, grp_ref[k], e_lo, e_hi)

            def hidden(e):
                gate = jnp.dot(c, wg_ref[e], preferred_element_type=F32)
                return (jax.nn.silu(gate) * jnp.dot(c, wu_ref[e], preferred_element_type=F32)).astype(BF16)

            yl = jnp.dot(hidden(e_lo), wd_ref[e_lo], preferred_element_type=F32)
            yh = jnp.dot(hidden(e_hi), wd_ref[e_hi], preferred_element_type=F32)
            _store_token_blocks(obuf.at[slot], h + (wl * yl + wh * yh), m)

        @pl.when(nval_ref[k] > TM // 2)
        def _():
            experts(TM)

        @pl.when(nval_ref[k] <= TM // 2)
        def _():
            experts(TM // 2)

        def copy_back(src, dst, size):
            pltpu.make_async_copy(_blocks(obuf.at[slot], dst, size), _blocks(h2_hbm, src, size),
                                  ssem.at[slot]).start()
        _for_runs(k, *tables, copy_back)


def _ple_body(lp_ref, h2_hbm, p_ref, g_ple_ref, w_gate_ref, w_proj_ref, g_fin_ref, y_ref, stage_h, win_buf, win_sem,
              *, tile0, win_tiles):
    ng = TP // SUBLANES
    i = pl.program_id(0)
    n = pl.num_programs(0)
    sub = lax.rem(i, win_tiles)
    w_local = i // win_tiles
    slot = lax.rem(w_local, 2)
    first = sub * TP

    def window_copy(w, to_slot):
        rows = win_tiles * TP * D_ROWS
        start = pl.multiple_of((tile0 // WIN_TILES + w) * (WIN * D_ROWS), D_ROWS)
        return pltpu.make_async_copy(h2_hbm.at[pl.ds(start, rows), :], win_buf.at[to_slot, pl.ds(0, rows), :],
                                     win_sem.at[to_slot])

    @pl.when(i == 0)
    def _():
        window_copy(0, 0).start()

    @pl.when(sub == 0)
    def _():
        window_copy(w_local, slot).wait()

        @pl.when(i + win_tiles < n)
        def _():
            window_copy(w_local + 1, 1 - slot).start()

    def group(g, carry):
        for k in range(SUBLANES):
            src = pl.multiple_of(lp_ref[0, 0, first + g * SUBLANES + k] * D_ROWS, D_ROWS)
            stage_h[g, pl.ds(k, D_ROWS, stride=SUBLANES), :] = win_buf[slot, pl.ds(src, D_ROWS), :]
        return carry

    lax.fori_loop(0, ng, group, 0, unroll=2)
    h = jnp.concatenate([stage_h[:, s * SUBLANES:(s + 1) * SUBLANES, :].reshape(TP, LANES) for s in range(D_ROWS)],
                        axis=1)
    c = _rms(h, g_ple_ref[...]).astype(BF16)
    gate = jax.nn.sigmoid(jnp.dot(c, w_gate_ref[...], preferred_element_type=F32))
    proj = jnp.dot(p_ref[...].astype(BF16), w_proj_ref[...], preferred_element_type=F32)
    h = h + proj * gate
    y_ref[...] = _rms(h, g_fin_ref[...])


def _full(shape, single=False):
    nd = len(shape)
    if single:
        return pl.BlockSpec(shape, lambda *_: (0,) * nd, pipeline_mode=pl.Buffered(1))
    return pl.BlockSpec(shape, lambda *_: (0,) * nd)


def _mixer(x_prompt, x_tm, hist_tm, weights, tri, expert_w):
    batch, seq, _ = x_prompt.shape
    dec_seq, dec_batch, _ = x_tm.shape
    nj = seq // TP
    npt = batch * nj
    sblk = TP // dec_seq
    nst = dec_batch // sblk
    n_tiles = npt + nst
    n_tail = n_tiles % WIN_TILES
    assert npt % WIN_TILES == 0 and n_tail <= nst
    n_windows = pl.cdiv(n_tiles, WIN_TILES)
    nh = CONV_WIDTH - 1
    w_specs = [_full(w.shape, single=True) for w in weights]
    body = functools.partial(_mixer_body, n_prompt_tiles=npt, nj=nj, dec_seq=dec_seq, sblk=sblk, n_tail_tiles=n_tail)

    def p_idx(i):
        return jnp.minimum(i, npt - 1)

    def s_idx(i):
        return jnp.maximum(i - npt, 0)

    assert npt % N_EXPERTS == 0
    parts = npt // N_EXPERTS
    e_specs = [pl.BlockSpec((1, w.shape[1] // parts, w.shape[2]), lambda i: (p_idx(i) // parts, p_idx(i) % parts, 0))
               for w in expert_w]
    e_shapes = [jax.ShapeDtypeStruct(w.shape, BF16) for w in expert_w]

    return pl.pallas_call(
        body,
        grid=(n_tiles,),
        in_specs=[pl.BlockSpec((1, TP, D_MODEL), lambda i: (p_idx(i) // nj, p_idx(i) % nj, 0)),
                  pl.BlockSpec((dec_seq, sblk, D_MODEL), lambda i: (0, s_idx(i), 0), pipeline_mode=pl.Buffered(1)),
                  pl.BlockSpec((nh, sblk, C_CONV), lambda i: (0, s_idx(i), 0), pipeline_mode=pl.Buffered(1))]
        + w_specs + [_full(tri.shape, single=True)] + e_specs,
        out_specs=[
            pl.BlockSpec(memory_space=pl.ANY),
            pl.BlockSpec((1, SUBLANES, WIN), lambda i: (i // WIN_TILES, 0, 0)),
            pl.BlockSpec((1, CLS_ROWS, LANES), lambda i: (i // WIN_TILES, 0, 0)),
            pl.BlockSpec((1, HIST, C_CONV), lambda i: (p_idx(i) // nj, 0, 0)),
            pl.BlockSpec((nh, sblk, C_CONV), lambda i: (0, s_idx(i), 0), pipeline_mode=pl.Buffered(1)),
            pl.BlockSpec((dec_seq, sblk, C_GMLP), lambda i: (0, s_idx(i), 0), pipeline_mode=pl.Buffered(1)),
        ] + e_specs,
        out_shape=[
            jax.ShapeDtypeStruct((n_tiles * TP * D_ROWS, LANES), F32),
            jax.ShapeDtypeStruct((n_windows, SUBLANES, WIN), I32),
            jax.ShapeDtypeStruct((n_windows, CLS_ROWS, LANES), F32),
            jax.ShapeDtypeStruct((batch, HIST, C_CONV), F32),
            jax.ShapeDtypeStruct((nh, dec_batch, C_CONV), F32),
            jax.ShapeDtypeStruct((dec_seq, dec_batch, C_GMLP), F32),
        ] + e_shapes,
        scratch_shapes=[
            pltpu.VMEM((C_CONV // LANES, CONV_PITCH * (HIST + TP), LANES), F32),
            pltpu.VMEM((nh + dec_seq, sblk, C_CONV), F32),
            pltpu.VMEM((TP, D_MODEL), BF16),
            pltpu.VMEM((CLS_ROWS, LANES), F32),
            pltpu.VMEM((WIN_TILES, TP // SUBLANES, GROUP_ROWS, LANES), F32),
            pltpu.VMEM((WIN_TILES, SUBLANES, TP), I32),
            pltpu.VMEM((SUBLANES, WIN), I32),
            pltpu.SMEM((1, WIN), I32),
            pltpu.VMEM((WIN * D_ROWS, LANES), F32),
            pltpu.SemaphoreType.DMA(()),
        ],
        compiler_params=pltpu.CompilerParams(
            dimension_semantics=("arbitrary",), vmem_limit_bytes=VMEM_LIMIT),
        name="mixer",
    )(x_prompt, x_tm, hist_tm, *weights, tri, *expert_w)


def _moe(tile_tabs, run_tabs, h1, g_ffn, w_r, b_r, w_g, w_u, w_d):
    n_tiles = tile_tabs[0].shape[0]
    n_prefetch = len(tile_tabs) + len(run_tabs)
    grid_spec = pltpu.PrefetchScalarGridSpec(
        num_scalar_prefetch=n_prefetch,
        grid=(n_tiles,),
        in_specs=[
            pl.BlockSpec(memory_space=pl.ANY),
            pl.BlockSpec((1, D_MODEL), lambda i, *_: (0, 0)),
            pl.BlockSpec((D_MODEL, LANES), lambda i, *_: (0, 0)),
            pl.BlockSpec((1, LANES), lambda i, *_: (0, 0)),
            pl.BlockSpec((EPG, D_MODEL, D_EXPERT), lambda i, grp, *_: (grp[i], 0, 0)),
            pl.BlockSpec((EPG, D_MODEL, D_EXPERT), lambda i, grp, *_: (grp[i], 0, 0)),
            pl.BlockSpec((EPG, D_EXPERT, D_MODEL), lambda i, grp, *_: (grp[i], 0, 0)),
        ],
        out_specs=pl.BlockSpec(memory_space=pl.ANY),
        scratch_shapes=[pltpu.VMEM((2, TM * D_ROWS, LANES), F32),
                        pltpu.VMEM((2, TM * D_ROWS, LANES), F32),
                        pltpu.SemaphoreType.DMA((2,)),
                        pltpu.SemaphoreType.DMA((2,))],
    )
    return pl.pallas_call(
        _moe_body,
        grid_spec=grid_spec,
        out_shape=jax.ShapeDtypeStruct(h1.shape, F32),
        compiler_params=pltpu.CompilerParams(
            dimension_semantics=("arbitrary",), vmem_limit_bytes=VMEM_LIMIT),
        name="moe",
    )(*tile_tabs, *run_tabs, h1, g_ffn, w_r, b_r, w_g, w_u, w_d)


def _ple(lp, h2, tile0, p, g_ple, w_gate, w_proj, g_fin, name):
    n_tiles = p.shape[0] // TP
    win_tiles = min(WIN_TILES, n_tiles)
    assert tile0 % WIN_TILES == 0 and n_tiles % win_tiles == 0
    return pl.pallas_call(
        functools.partial(_ple_body, tile0=tile0, win_tiles=win_tiles),
        grid=(n_tiles,),
        in_specs=[
            pl.BlockSpec((1, 1, WIN), lambda i: ((tile0 + i) // WIN_TILES, 0, 0), memory_space=pltpu.SMEM),
            pl.BlockSpec(memory_space=pl.ANY),
            pl.BlockSpec((TP, PLE_DIM), lambda i: (i, 0)),
            _full(g_ple.shape), _full(w_gate.shape), _full(w_proj.shape), _full(g_fin.shape),
        ],
        out_specs=pl.BlockSpec((TP, D_MODEL), lambda i: (i, 0)),
        out_shape=jax.ShapeDtypeStruct((n_tiles * TP, D_MODEL), F32),
        scratch_shapes=[pltpu.VMEM((TP // SUBLANES, GROUP_ROWS, LANES), F32),
                        pltpu.VMEM((2, win_tiles * TP * D_ROWS, LANES), F32),
                        pltpu.SemaphoreType.DMA((2,))],
        compiler_params=pltpu.CompilerParams(
            dimension_semantics=("arbitrary",), vmem_limit_bytes=VMEM_LIMIT),
        name=name,
    )(lp, h2, p, g_ple, w_gate, w_proj, g_fin)


def kernel(x_prompt, x_sample, p_prompt, p_sample, state_conv, norm_mix_g, w_in, conv_w, conv_b, conv_ln_g, conv_ln_b, gmlp_ln_g, gmlp_ln_b, w_s, b_s, w_out, norm_ffn_g, w_router_group, b_router_group, w_router_expert, b_router_expert, w_exp_gate, w_exp_up, w_exp_down, norm_ple_g, w_ple_gate, w_ple_proj, norm_final_g):
    depth = w_in.shape[0]
    assert depth == 1, "single-layer pipeline"
    batch, seq, _ = x_prompt.shape
    dec_batch, dec_seq, _ = x_sample.shape
    assert seq % TP == 0 and TP % CHUNK == 0 and TP % dec_seq == 0 and dec_batch % (TP // dec_seq) == 0
    sblk = TP // dec_seq
    n_prompt = batch * seq
    n_sample = dec_batch * dec_seq
    n_tok = n_prompt + n_sample

    row = lambda a: a.reshape(1, -1)
    w_in_b = w_in[0].astype(BF16)
    w_out_b = w_out[0].astype(BF16)
    cw = jnp.concatenate([conv_w[0], jnp.zeros((1, C_CONV), F32)], axis=0)
    tril = jnp.asarray(np.tril(np.ones((CHUNK, CHUNK), bool)))
    ws_m = jnp.where(tril[None], w_s[0], 0.0)
    ws_cat = jnp.concatenate([ws_m[0::2], ws_m[1::2]], axis=2).astype(BF16)
    bs_lane = jnp.repeat(jnp.transpose(b_s[0]), HEAD_DIM, axis=1)
    w_rt = jnp.zeros((CLS_ROWS, D_MODEL), F32)
    w_rt = w_rt.at[0:N_GROUPS].set(jnp.transpose(w_router_group[0]))
    w_rt = w_rt.at[N_GROUPS:N_GROUPS + N_EXPERTS].set(jnp.transpose(w_router_expert[0])).astype(BF16)
    b_r = jnp.zeros((CLS_ROWS, 1), F32)
    b_r = b_r.at[0:N_GROUPS, 0].set(b_router_group[0]).at[N_GROUPS:N_GROUPS + N_EXPERTS, 0].set(b_router_expert[0])
    n_logit = N_GROUPS + N_EXPERTS
    w_rn = jnp.concatenate([w_router_group[0], w_router_expert[0], jnp.zeros((D_MODEL, LANES - n_logit), F32)],
                           axis=1).astype(BF16)
    b_rn = jnp.concatenate([b_router_group[0], b_router_expert[0], jnp.zeros((LANES - n_logit,), F32)]).reshape(1, LANES)
    tri = jnp.asarray(np.triu(np.ones((TP, TP), np.float32)), dtype=BF16)
    wsl = jnp.where(jnp.asarray(np.tril(np.ones((dec_seq, dec_seq), bool)))[None], w_s[0][:, :dec_seq, :dec_seq], 0.0)
    wsl = jnp.repeat(jnp.transpose(wsl, (1, 2, 0)).reshape(dec_seq * dec_seq, N_HEADS), HEAD_DIM, axis=1)
    bsl = jnp.repeat(jnp.transpose(b_s[0][:, :dec_seq]), HEAD_DIM, axis=1)

    weights = (row(norm_mix_g[0]), w_in_b, cw, row(conv_b[0]), row(conv_ln_g[0]), row(conv_ln_b[0]),
               row(gmlp_ln_g[0]), row(gmlp_ln_b[0]), ws_cat, bs_lane, wsl, bsl,
               w_out_b, row(norm_ffn_g[0]), w_rt, b_r)

    x_tm = jnp.transpose(x_sample, (1, 0, 2))
    hist_tm = jnp.transpose(state_conv[0], (1, 0, 2))
    h1, lp, tab, cst_p, cst_s, v_s, w_g, w_u, w_d = _mixer(
        x_prompt, x_tm, hist_tm, weights, tri, (w_exp_gate[0], w_exp_up[0], w_exp_down[0]))

    rank0 = tab[:, :N_CLASSES, 0].astype(I32)
    cnt = tab[:, :N_CLASSES, 1].astype(I32)
    lsrc = tab[:, :N_CLASSES, 2].astype(I32) + (jnp.arange(tab.shape[0], dtype=I32) * WIN)[:, None]
    total = rank0[-1] + cnt[-1]
    padded = ((total + TM - 1) // TM) * TM
    ids = jnp.arange(N_CLASSES, dtype=I32)
    ends = jnp.sum(jnp.where(ids[None, :] <= ids[:, None], padded[None, :], 0), axis=1)
    offs = ends - padded
    all_rows = jnp.sum(padded)
    n_tiles = (n_tok + N_CLASSES * (TM - 1)) // TM + 2
    tile_start = jnp.arange(n_tiles, dtype=I32) * TM
    used = tile_start < all_rows
    last_cls = jnp.max(jnp.where(padded > 0, ids, 0))
    tile_cls = jnp.sum((tile_start[:, None] >= ends[None, :]).astype(I32), axis=1)
    tile_cls = jnp.where(used, tile_cls, last_cls)
    of_tile = tile_cls[:, None] == ids[None, :]
    tile_r0 = tile_start - jnp.sum(jnp.where(of_tile, offs[None, :], 0), axis=1)
    tile_total = jnp.sum(jnp.where(of_tile, total[None, :], 0), axis=1)
    tile_nval = jnp.where(used, jnp.clip(tile_total - tile_r0, 0, TM), 0).astype(I32)
    pair = tile_cls % 6
    assert _PAIRS == ((0, 1), (0, 2), (0, 3), (1, 2), (1, 3), (2, 3))
    tile_lo = jnp.where(pair < 3, 0, jnp.where(pair < 5, 1, 2)).astype(I32)
    tile_hi = jnp.where(pair == 0, 1, jnp.where(jnp.logical_or(pair == 1, pair == 3), 2, 3)).astype(I32)
    run_beg = jnp.sum(jnp.where(of_tile[None], rank0[:, None, :], 0), axis=2)
    run_end = run_beg + jnp.sum(jnp.where(of_tile[None], cnt[:, None, :], 0), axis=2)
    tile_jlo = jnp.sum((run_end <= tile_r0[None, :]).astype(I32), axis=0)
    tile_jhi = jnp.sum((run_beg < (tile_r0 + tile_nval)[None, :]).astype(I32), axis=0)
    tile_jhi = jnp.maximum(tile_jhi, tile_jlo)

    h2 = _moe((tile_cls // 6, tile_lo, tile_hi, tile_nval, tile_cls, tile_r0, tile_jlo, tile_jhi),
              (rank0.reshape(-1), cnt.reshape(-1), lsrc.reshape(-1)),
              h1, row(norm_ffn_g[0]), w_rn, b_rn, w_g, w_u, w_d)

    lp = lp[:, 0:1, :]
    ple_w = (row(norm_ple_g[0]), w_ple_gate[0].astype(BF16), w_ple_proj[0].astype(BF16), row(norm_final_g))
    y_p = _ple(lp, h2, 0, p_prompt[0].reshape(n_prompt, PLE_DIM), *ple_w, name="ple_prompt")
    p_s_tm = jnp.transpose(p_sample[0].reshape(dec_batch // sblk, sblk, dec_seq, PLE_DIM), (0, 2, 1, 3))
    y_s = _ple(lp, h2, n_prompt // TP, p_s_tm.reshape(n_sample, PLE_DIM), *ple_w, name="ple_sample")

    y_prompt = y_p.reshape(batch, seq, D_MODEL)
    y_sample = jnp.transpose(y_s.reshape(dec_batch // sblk, dec_seq, sblk, D_MODEL), (0, 2, 1, 3))
    y_sample = y_sample.reshape(dec_batch, dec_seq, D_MODEL)
    state_conv_prompt = cst_p[:, HIST_OFF:, :][None]
    state_conv_sample = jnp.transpose(cst_s, (1, 0, 2))[None]
    state_gmlp_v_sample = jnp.transpose(v_s, (1, 0, 2))[None]
    return (y_prompt, y_sample, state_conv_prompt, state_conv_sample, state_gmlp_v_sample)
```

```python
import functools

import numpy as np
import jax
import jax.numpy as jnp
from jax import lax
from jax.experimental import pallas as pl
from jax.experimental.pallas import tpu as pltpu

F32 = jnp.float32
BF16 = jnp.bfloat16
I32 = jnp.int32

D_MODEL = 1024
C_CONV = 512
C_GMLP = 512
N_HEADS = 8
HEAD_DIM = 64
CONV_WIDTH = 31
CHUNK = 128
PLE_DIM = 256
N_GROUPS = 4
EPG = 4
N_EXPERTS = 16
D_EXPERT = 512
EPS = 1e-6
LANES = 128
SUBLANES = 8

N_CLASSES = N_GROUPS * 6
CLS_ROWS = 32
HIST = 32
HIST_OFF = HIST - (CONV_WIDTH - 1)
D_ROWS = D_MODEL // LANES
assert D_ROWS == SUBLANES
GROUP_ROWS = D_ROWS * SUBLANES

TP = 512
WIN_TILES = 4
WIN = WIN_TILES * TP
TM = 256
CONV_ROWS = 64
CONV_PITCH = 2
PIECES = tuple(TM >> b for b in range(TM.bit_length()))
LARGE_PIECE = 32

VMEM_LIMIT = 58 * 1024 * 1024

_PAIRS = ((0, 1), (0, 2), (0, 3), (1, 2), (1, 3), (2, 3))


def _rms(x, g):
    ms = jnp.mean(x * x, axis=-1, keepdims=True)
    return x * lax.rsqrt(ms + EPS) * g


def _ln(x, g, b):
    mu = jnp.mean(x, axis=-1, keepdims=True)
    xc = x - mu
    var = jnp.mean(xc * xc, axis=-1, keepdims=True)
    return xc * lax.rsqrt(var + EPS) * g + b


def _max4(v):
    return jnp.maximum(jnp.maximum(v[0], v[1]), jnp.maximum(v[2], v[3]))


def _first4(v, m):
    return jnp.where(v[0] == m, 0, jnp.where(v[1] == m, 1, jnp.where(v[2] == m, 2, 3))).astype(I32)


def _route(lt):
    gl = [lt[i:i + 1, :] for i in range(N_GROUPS)]
    g = _first4(gl, _max4(gl))
    a = []
    for j in range(EPG):
        rows = [lt[N_GROUPS + EPG * q + j:N_GROUPS + EPG * q + j + 1, :] for q in range(N_GROUPS)]
        a.append(jnp.where(g == 0, rows[0], jnp.where(g == 1, rows[1], jnp.where(g == 2, rows[2], rows[3]))))
    v1 = _max4(a)
    i1 = _first4(a, v1)
    a2 = [jnp.where(i1 == j, -jnp.inf, a[j]) for j in range(EPG)]
    i2 = _first4(a2, _max4(a2))
    lo = jnp.minimum(i1, i2)
    hi = jnp.maximum(i1, i2)
    pair = jnp.where(lo == 0, hi - 1, jnp.where(lo == 1, hi + 1, 5))
    return g * 6 + pair


def _combine_weights(logits, grp, e_lo, e_hi):
    lane = lax.broadcasted_iota(I32, logits.shape, 1)
    is_group = lane < N_GROUPS
    m = jnp.max(jnp.where(is_group, logits, -jnp.inf), axis=1, keepdims=True)
    den = jnp.sum(jnp.where(is_group, jnp.exp(logits - m), 0.0), axis=1, keepdims=True)
    base = N_GROUPS + EPG * grp
    v_lo = jnp.sum(jnp.where(lane == base + e_lo, logits, 0.0), axis=1, keepdims=True)
    v_hi = jnp.sum(jnp.where(lane == base + e_hi, logits, 0.0), axis=1, keepdims=True)
    top = jnp.maximum(v_lo, v_hi)
    p_lo = jnp.exp(v_lo - top)
    p_hi = jnp.exp(v_hi - top)
    scale = 1.0 / (den * (p_lo + p_hi))
    return p_lo * scale, p_hi * scale


def _load_token_blocks(ref, n_tok):
    return jnp.concatenate([ref[pl.ds(s, n_tok, stride=D_ROWS), :] for s in range(D_ROWS)], axis=1)


def _store_token_blocks(ref, x, n_tok):
    for s in range(D_ROWS):
        ref[pl.ds(s, n_tok, stride=D_ROWS), :] = x[:, s * LANES:(s + 1) * LANES]


def _route_tile(h, g_ffn_ref, w_rt_ref, b_r_ref, stage_h, cls_scr, sub):
    c = _rms(h, g_ffn_ref[...]).astype(BF16)
    lt = lax.dot_general(w_rt_ref[...], c, (((1,), (1,)), ((), ())), preferred_element_type=F32)
    cls_scr[sub] = jnp.broadcast_to(_route(lt + b_r_ref[...]), (SUBLANES, TP))
    ng = TP // SUBLANES
    for s in range(D_ROWS):
        stage_h[sub, :, s * SUBLANES:(s + 1) * SUBLANES, :] = h[:, s * LANES:(s + 1) * LANES].reshape(
            ng, SUBLANES, LANES)


def _sort_window(tri_ref, carry_ref, h1_hbm, h1_buf, h1_sem, lp_ref, tab_ref, stage_h, cls_scr, lp_vmem, lp_smem,
                 w, is_last_step, n_sub):
    rows = lax.broadcasted_iota(I32, (CLS_ROWS, TP), 0)
    onehot, counts = [], []
    lstart = jnp.zeros((CLS_ROWS, 1), F32)
    if n_sub < WIN_TILES:
        lp_ref[0, :, n_sub * TP:] = jnp.zeros((SUBLANES, (WIN_TILES - n_sub) * TP), I32)
        lp_vmem[:, n_sub * TP:] = jnp.zeros((SUBLANES, (WIN_TILES - n_sub) * TP), I32)
    for s in range(n_sub):
        cls = cls_scr[s, 0:1, :]
        onehot.append(jnp.where(rows == cls, 1.0, 0.0).astype(F32))
        counts.append(jnp.sum(onehot[s], axis=1, keepdims=True))
        lstart = lstart + jnp.sum(jnp.where(cls < rows, 1.0, 0.0).astype(F32), axis=1, keepdims=True)
    base = lstart
    for s in range(n_sub):
        pre = jnp.dot(onehot[s].astype(BF16), tri_ref[...], preferred_element_type=F32)
        lpos = jnp.sum(onehot[s] * (pre - 1.0 + base), axis=0, keepdims=True).astype(I32)
        lpos = lpos * D_ROWS
        lp_ref[0, :, s * TP:(s + 1) * TP] = jnp.broadcast_to(lpos, (SUBLANES, TP))
        lp_vmem[:, s * TP:(s + 1) * TP] = jnp.broadcast_to(lpos, (SUBLANES, TP))
        base = base + counts[s]
    tot = base - lstart
    carry = carry_ref[:, 0:1]
    lane = lax.broadcasted_iota(I32, (CLS_ROWS, LANES), 1)
    tab_ref[0] = jnp.where(lane == 0, carry, jnp.where(lane == 1, tot, jnp.where(lane == 2, lstart, 0.0)))
    carry_ref[...] = jnp.broadcast_to(carry + tot, (CLS_ROWS, LANES))
    pltpu.sync_copy(lp_vmem.at[pl.ds(0, 1)], lp_smem)

    def window_copy(win, n_tiles):
        rows = n_tiles * TP * D_ROWS
        start = win * (WIN * D_ROWS)
        start = start if isinstance(start, int) else pl.multiple_of(start, D_ROWS)
        return pltpu.make_async_copy(h1_buf.at[pl.ds(0, rows), :], h1_hbm.at[pl.ds(start, rows), :], h1_sem)

    @pl.when(w > 0)
    def _():
        window_copy(0, WIN_TILES).wait()

    for s in range(n_sub):
        def group(g, carry_, s=s):
            for k in range(SUBLANES):
                dst = pl.multiple_of(lp_smem[0, s * TP + g * SUBLANES + k], D_ROWS)
                h1_buf[pl.ds(dst, D_ROWS), :] = stage_h[s, g, pl.ds(k, D_ROWS, stride=SUBLANES), :]
            return carry_

        lax.fori_loop(0, TP // SUBLANES, group, 0, unroll=2)

    window_copy(w, n_sub).start()

    @pl.when(is_last_step)
    def _():
        window_copy(w, n_sub).wait()


def _mixer_front(x, g_mix_ref, w_in_ref):
    a = _rms(x, g_mix_ref[...]).astype(BF16)
    z = jnp.dot(a, w_in_ref[...], preferred_element_type=F32)
    glu = z[:, 0:C_CONV] * jax.nn.sigmoid(z[:, C_CONV:2 * C_CONV])
    u = jax.nn.gelu(z[:, 2 * C_CONV:2 * C_CONV + C_GMLP])
    gv = jax.nn.gelu(z[:, 2 * C_CONV + C_GMLP:])
    return glu, u, gv


def _mixer_body(xp_ref, xs_ref, hist_ref, g_mix_ref, w_in_ref, cw_ref, cb_ref, clg_ref, clb_ref, glg_ref, glb_ref,
                ws_ref, bs_ref, wsl_ref, bsl_ref, w_out_ref, g_ffn_ref, w_rt_ref, b_r_ref, tri_ref,
                eg_ref, eu_ref, ed_ref,
                h1_hbm, lp_ref, tab_ref, cstp_ref, csts_ref, vs_ref, egb_ref, eub_ref, edb_ref,
                glu_scr, xs_scr, cat_scr, carry_ref, stage_h, cls_scr, lp_vmem, lp_smem, h1_buf, h1_sem,
                *, n_prompt_tiles, nj, dec_seq, sblk, n_tail_tiles):
    i = pl.program_id(0)
    sub = lax.rem(i, WIN_TILES)

    egb_ref[...] = eg_ref[...].astype(BF16)
    eub_ref[...] = eu_ref[...].astype(BF16)
    edb_ref[...] = ed_ref[...].astype(BF16)

    @pl.when(i == 0)
    def _():
        carry_ref[...] = jnp.zeros_like(carry_ref)

    def tail(x):
        h = x + jnp.dot(cat_scr[...], w_out_ref[...], preferred_element_type=F32)
        _route_tile(h, g_ffn_ref, w_rt_ref, b_r_ref, stage_h, cls_scr, sub)

    @pl.when(i < n_prompt_tiles)
    def _prompt():
        j = lax.rem(i, nj)
        x = xp_ref[0]
        glu, u, gv = _mixer_front(x, g_mix_ref, w_in_ref)

        def time_rows(t0, n):
            return pl.ds(CONV_PITCH * t0, n, stride=CONV_PITCH)

        @pl.when(j == 0)
        def _():
            for s in range(C_CONV // LANES):
                glu_scr[s, time_rows(0, HIST), :] = jnp.zeros((HIST, LANES), F32)

        @pl.when(j > 0)
        def _():
            for s in range(C_CONV // LANES):
                glu_scr[s, time_rows(0, HIST), :] = glu_scr[s, time_rows(TP, HIST), :]

        for s in range(C_CONV // LANES):
            glu_scr[s, time_rows(HIST, TP), :] = glu[:, s * LANES:(s + 1) * LANES]
        cstp_ref[0] = jnp.concatenate([glu_scr[s, time_rows(TP, HIST), :] for s in range(C_CONV // LANES)], axis=1)

        cb = cb_ref[...]
        for r0 in range(0, TP, CONV_ROWS):
            slabs = []
            for s in range(C_CONV // LANES):
                acc = None
                for k in range(CONV_WIDTH):
                    term = (cw_ref[k:k + 1, s * LANES:(s + 1) * LANES]
                            * glu_scr[s, time_rows(r0 + HIST_OFF + k, CONV_ROWS), :])
                    acc = term if acc is None else acc + term
                slabs.append(acc)
            ya = jax.nn.silu(_ln(jnp.concatenate(slabs, axis=1) + cb, clg_ref[...], clb_ref[...]))
            cat_scr[r0:r0 + CONV_ROWS, 0:C_CONV] = ya.astype(BF16)

        v = _ln(gv, glg_ref[...], glb_ref[...])
        vb = v.astype(BF16)
        lane = lax.broadcasted_iota(I32, (CHUNK, LANES), 1)
        is_lo = lane < HEAD_DIM
        zero = jnp.zeros((CHUNK, LANES), BF16)
        for c in range(TP // CHUNK):
            for p in range(N_HEADS // 2):
                blk = vb[c * CHUNK:(c + 1) * CHUNK, p * LANES:(p + 1) * LANES]
                rhs = jnp.concatenate([jnp.where(is_lo, blk, zero), jnp.where(is_lo, zero, blk)], axis=0)
                mixed = jnp.dot(ws_ref[p], rhs, preferred_element_type=F32) + bs_ref[:, p * LANES:(p + 1) * LANES]
                yb = u[c * CHUNK:(c + 1) * CHUNK, p * LANES:(p + 1) * LANES] * mixed
                cat_scr[c * CHUNK:(c + 1) * CHUNK, C_CONV + p * LANES:C_CONV + (p + 1) * LANES] = yb.astype(BF16)
        tail(x)

    @pl.when(i >= n_prompt_tiles)
    def _sample():
        x_nat = xs_ref[...].reshape(TP, D_MODEL)
        slabs = []
        for s in range(D_ROWS):
            half, row0 = s // 2, (s % 2) * TP
            glu_scr[half, row0:row0 + TP, :] = x_nat[:, s * LANES:(s + 1) * LANES]
            slabs.append(jnp.concatenate(
                [glu_scr[half, pl.ds(row0 + t, sblk, stride=dec_seq), :] for t in range(dec_seq)], axis=0))
        x = jnp.concatenate(slabs, axis=1)
        glu, u, gv = _mixer_front(x, g_mix_ref, w_in_ref)

        nh = CONV_WIDTH - 1
        xs_scr[0:nh] = hist_ref[...]
        xs_scr[nh:nh + dec_seq] = glu.reshape(dec_seq, sblk, C_CONV)
        csts_ref[...] = xs_scr[dec_seq:dec_seq + nh]

        cb = cb_ref[...]
        for t in range(dec_seq):
            acc = jnp.zeros((sblk, C_CONV), F32)
            for k in range(CONV_WIDTH):
                acc = acc + cw_ref[k:k + 1, :] * xs_scr[t + k]
            ya = jax.nn.silu(_ln(acc + cb, clg_ref[...], clb_ref[...]))
            cat_scr[t * sblk:(t + 1) * sblk, 0:C_CONV] = ya.astype(BF16)

        v = _ln(gv, glg_ref[...], glb_ref[...])
        vs_ref[...] = v.reshape(dec_seq, sblk, C_GMLP)
        for t in range(dec_seq):
            mixed = jnp.broadcast_to(bsl_ref[t:t + 1, :], (sblk, C_GMLP))
            for tp in range(t + 1):
                r = t * dec_seq + tp
                mixed = mixed + wsl_ref[r:r + 1, :] * v[tp * sblk:(tp + 1) * sblk, :]
            yb = u[t * sblk:(t + 1) * sblk, :] * mixed
            cat_scr[t * sblk:(t + 1) * sblk, C_CONV:] = yb.astype(BF16)
        tail(x)

    is_last_step = i == pl.num_programs(0) - 1
    sort_args = (tri_ref, carry_ref, h1_hbm, h1_buf, h1_sem, lp_ref, tab_ref, stage_h, cls_scr, lp_vmem, lp_smem,
                 i // WIN_TILES, is_last_step)

    @pl.when(sub == WIN_TILES - 1)
    def _():
        _sort_window(*sort_args, WIN_TILES)

    if n_tail_tiles:
        @pl.when(is_last_step)
        def _():
            _sort_window(*sort_args, n_tail_tiles)


def _for_pieces(ln, fn):
    def emit(sizes):
        for size in sizes:
            shift = size.bit_length()
            off = lax.shift_left(lax.shift_right_logical(ln, shift), shift)

            @pl.when((ln & size) != 0)
            def _(off=off, size=size):
                fn(off, size)

    large = tuple(s for s in PIECES if s >= LARGE_PIECE)

    @pl.when(ln >= LARGE_PIECE)
    def _():
        emit(large)

    emit(tuple(s for s in PIECES if s < LARGE_PIECE))


def _for_runs(k, tcls_ref, r0_ref, nval_ref, jlo_ref, jhi_ref, rank0_ref, cnt_ref, lsrc_ref, fn):
    c = tcls_ref[k]
    ra = r0_ref[k]
    rb = ra + nval_ref[k]

    def body(j, carry):
        idx = j * N_CLASSES + c
        s = rank0_ref[idx]
        lo = jnp.maximum(s, ra)
        hi = jnp.minimum(s + cnt_ref[idx], rb)
        ln = hi - lo

        @pl.when(ln > 0)
        def _():
            src = lsrc_ref[idx] + (lo - s)
            dst = lo - ra
            _for_pieces(ln, lambda off, size: fn(src + off, dst + off, size))

        return carry

    lax.fori_loop(jlo_ref[k], jhi_ref[k], body, 0)


def _blocks(ref, first, n):
    start = first * D_ROWS if isinstance(first, int) else pl.multiple_of(first * D_ROWS, D_ROWS)
    return ref.at[pl.ds(start, n * D_ROWS), :]


def _moe_body(grp_ref, lo_ref, hi_ref, nval_ref, tcls_ref, r0_ref, jlo_ref, jhi_ref, rank0_ref, cnt_ref, lsrc_ref,
              h1_hbm, g_ffn_ref, w_r_ref, b_r_ref, wg_ref, wu_ref, wd_ref,
              h2_hbm, hbuf, obuf, gsem, ssem):
    k = pl.program_id(0)
    n = pl.num_programs(0)
    slot = lax.rem(k, 2)
    tables = (tcls_ref, r0_ref, nval_ref, jlo_ref, jhi_ref, rank0_ref, cnt_ref, lsrc_ref)

    def gather_tile(tile, to_slot):
        def copy(src, dst, size):
            pltpu.make_async_copy(_blocks(h1_hbm, src, size), _blocks(hbuf.at[to_slot], dst, size),
                                  gsem.at[to_slot]).start()
        _for_runs(tile, *tables, copy)

    def wait_rows(n_rows, make_copy):
        _for_pieces(n_rows, lambda off, size: make_copy(size).wait())

    @pl.when(k == 0)
    def _():
        hbuf[...] = jnp.zeros_like(hbuf)
        gather_tile(0, 0)

    @pl.when(jnp.logical_and(k >= 2, nval_ref[jnp.maximum(k - 2, 0)] > 0))
    def _():
        wait_rows(nval_ref[jnp.maximum(k - 2, 0)],
                  lambda size: pltpu.make_async_copy(_blocks(obuf.at[slot], 0, size), _blocks(h2_hbm, 0, size),
                                                     ssem.at[slot]))

    nxt = jnp.minimum(k + 1, n - 1)

    @pl.when(jnp.logical_and(k + 1 < n, nval_ref[nxt] > 0))
    def _():
        gather_tile(nxt, 1 - slot)

    @pl.when(nval_ref[k] > 0)
    def _():
        wait_rows(nval_ref[k],
                  lambda size: pltpu.make_async_copy(_blocks(h1_hbm, 0, size), _blocks(hbuf.at[slot], 0, size),
                                                     gsem.at[slot]))
        e_lo = lo_ref[k]
        e_hi = hi_ref[k]

        def experts(m):
            h = _load_token_blocks(hbuf.at[slot], m)
            c = _rms(h, g_ffn_ref[...]).astype(BF16)
            logits = jnp.dot(c, w_r_ref[...], preferred_element_type=F32) + b_r_ref[...]
            wl, wh = _combine_weights(logits, grp_ref[k], e_lo, e_hi)

            def hidden(e):
                gate = jnp.dot(c, wg_ref[e], preferred_element_type=F32)
                return (jax.nn.silu(gate) * jnp.dot(c, wu_ref[e], preferred_element_type=F32)).astype(BF16)

            yl = jnp.dot(hidden(e_lo), wd_ref[e_lo], preferred_element_type=F32)
            yh = jnp.dot(hidden(e_hi), wd_ref[e_hi], preferred_element_type=F32)
            _store_token_blocks(obuf.at[slot], h + (wl * yl + wh * yh), m)

        @pl.when(nval_ref[k] > TM // 2)
        def _():
            experts(TM)

        @pl.when(nval_ref[k] <= TM // 2)
        def _():
            experts(TM // 2)

        def copy_back(src, dst, size):
            pltpu.make_async_copy(_blocks(obuf.at[slot], dst, size), _blocks(h2_hbm, src, size),
                                  ssem.at[slot]).start()
        _for_runs(k, *tables, copy_back)


def _ple_body(lp_ref, h2_hbm, p_ref, g_ple_ref, w_gate_ref, w_proj_ref, g_fin_ref, y_ref, stage_h, win_buf, win_sem,
              *rest, tile0, win_tiles, seq_rows):
    ng = TP // SUBLANES
    i = pl.program_id(0)
    n = pl.num_programs(0)
    sub = lax.rem(i, win_tiles)
    w_local = i // win_tiles
    slot = lax.rem(w_local, 2)
    first = sub * TP

    def window_copy(w, to_slot):
        rows = win_tiles * TP * D_ROWS
        start = pl.multiple_of((tile0 // WIN_TILES + w) * (WIN * D_ROWS), D_ROWS)
        return pltpu.make_async_copy(h2_hbm.at[pl.ds(start, rows), :], win_buf.at[to_slot, pl.ds(0, rows), :],
                                     win_sem.at[to_slot])

    @pl.when(i == 0)
    def _():
        window_copy(0, 0).start()

    @pl.when(sub == 0)
    def _():
        window_copy(w_local, slot).wait()

        @pl.when(i + win_tiles < n)
        def _():
            window_copy(w_local + 1, 1 - slot).start()

    def group(g, carry):
        for k in range(SUBLANES):
            src = pl.multiple_of(lp_ref[0, 0, first + g * SUBLANES + k], D_ROWS)
            stage_h[g, pl.ds(k, D_ROWS, stride=SUBLANES), :] = win_buf[slot, pl.ds(src, D_ROWS), :]
        return carry

    lax.fori_loop(0, ng, group, 0, unroll=2)
    h = jnp.concatenate([stage_h[:, s * SUBLANES:(s + 1) * SUBLANES, :].reshape(TP, LANES) for s in range(D_ROWS)],
                        axis=1)
    c = _rms(h, g_ple_ref[...]).astype(BF16)
    gate = jax.nn.sigmoid(jnp.dot(c, w_gate_ref[...], preferred_element_type=F32))
    proj = jnp.dot(p_ref[...].astype(BF16), w_proj_ref[...], preferred_element_type=F32)
    y = _rms(h + proj * gate, g_fin_ref[...])
    if seq_rows is None:
        y_ref[...] = y
    else:
        sblk, dec_seq = seq_rows
        out_scr = rest[0]
        for s in range(D_ROWS):
            for t in range(dec_seq):
                out_scr[s, pl.ds(t, sblk, stride=dec_seq), :] = y[t * sblk:(t + 1) * sblk, s * LANES:(s + 1) * LANES]
            y_ref[:, s * LANES:(s + 1) * LANES] = out_scr[s]


def _full(shape, single=False):
    nd = len(shape)
    if single:
        return pl.BlockSpec(shape, lambda *_: (0,) * nd, pipeline_mode=pl.Buffered(1))
    return pl.BlockSpec(shape, lambda *_: (0,) * nd)


def _mixer(x_prompt, x_sample, hist_tm, weights, tri, expert_w):
    batch, seq, _ = x_prompt.shape
    dec_batch, dec_seq, _ = x_sample.shape
    nj = seq // TP
    npt = batch * nj
    sblk = TP // dec_seq
    nst = dec_batch // sblk
    n_tiles = npt + nst
    n_tail = n_tiles % WIN_TILES
    assert npt % WIN_TILES == 0 and n_tail <= nst
    n_windows = pl.cdiv(n_tiles, WIN_TILES)
    nh = CONV_WIDTH - 1
    w_specs = [_full(w.shape, single=True) for w in weights]
    body = functools.partial(_mixer_body, n_prompt_tiles=npt, nj=nj, dec_seq=dec_seq, sblk=sblk, n_tail_tiles=n_tail)

    def p_idx(i):
        return jnp.minimum(i, npt - 1)

    def s_idx(i):
        return jnp.maximum(i - npt, 0)

    assert npt % N_EXPERTS == 0
    parts = npt // N_EXPERTS
    e_specs = [pl.BlockSpec((1, w.shape[1] // parts, w.shape[2]), lambda i: (p_idx(i) // parts, p_idx(i) % parts, 0))
               for w in expert_w]
    e_shapes = [jax.ShapeDtypeStruct(w.shape, BF16) for w in expert_w]

    return pl.pallas_call(
        body,
        grid=(n_tiles,),
        in_specs=[pl.BlockSpec((1, TP, D_MODEL), lambda i: (p_idx(i) // nj, p_idx(i) % nj, 0)),
                  pl.BlockSpec((sblk, dec_seq, D_MODEL), lambda i: (s_idx(i), 0, 0), pipeline_mode=pl.Buffered(1)),
                  pl.BlockSpec((nh, sblk, C_CONV), lambda i: (0, s_idx(i), 0), pipeline_mode=pl.Buffered(1))]
        + w_specs + [_full(tri.shape, single=True)] + e_specs,
        out_specs=[
            pl.BlockSpec(memory_space=pl.ANY),
            pl.BlockSpec((1, SUBLANES, WIN), lambda i: (i // WIN_TILES, 0, 0)),
            pl.BlockSpec((1, CLS_ROWS, LANES), lambda i: (i // WIN_TILES, 0, 0)),
            pl.BlockSpec((1, HIST, C_CONV), lambda i: (p_idx(i) // nj, 0, 0)),
            pl.BlockSpec((nh, sblk, C_CONV), lambda i: (0, s_idx(i), 0), pipeline_mode=pl.Buffered(1)),
            pl.BlockSpec((dec_seq, sblk, C_GMLP), lambda i: (0, s_idx(i), 0), pipeline_mode=pl.Buffered(1)),
        ] + e_specs,
        out_shape=[
            jax.ShapeDtypeStruct((n_tiles * TP * D_ROWS, LANES), F32),
            jax.ShapeDtypeStruct((n_windows, SUBLANES, WIN), I32),
            jax.ShapeDtypeStruct((n_windows, CLS_ROWS, LANES), F32),
            jax.ShapeDtypeStruct((batch, HIST, C_CONV), F32),
            jax.ShapeDtypeStruct((nh, dec_batch, C_CONV), F32),
            jax.ShapeDtypeStruct((dec_seq, dec_batch, C_GMLP), F32),
        ] + e_shapes,
        scratch_shapes=[
            pltpu.VMEM((C_CONV // LANES, CONV_PITCH * (HIST + TP), LANES), F32),
            pltpu.VMEM((nh + dec_seq, sblk, C_CONV), F32),
            pltpu.VMEM((TP, D_MODEL), BF16),
            pltpu.VMEM((CLS_ROWS, LANES), F32),
            pltpu.VMEM((WIN_TILES, TP // SUBLANES, GROUP_ROWS, LANES), F32),
            pltpu.VMEM((WIN_TILES, SUBLANES, TP), I32),
            pltpu.VMEM((SUBLANES, WIN), I32),
            pltpu.SMEM((1, WIN), I32),
            pltpu.VMEM((WIN * D_ROWS, LANES), F32),
            pltpu.SemaphoreType.DMA(()),
        ],
        compiler_params=pltpu.CompilerParams(
            dimension_semantics=("arbitrary",), vmem_limit_bytes=VMEM_LIMIT),
        name="mixer",
    )(x_prompt, x_sample, hist_tm, *weights, tri, *expert_w)


def _moe(tile_tabs, run_tabs, h1, g_ffn, w_r, b_r, w_g, w_u, w_d):
    n_tiles = tile_tabs[0].shape[0]
    n_prefetch = len(tile_tabs) + len(run_tabs)
    grid_spec = pltpu.PrefetchScalarGridSpec(
        num_scalar_prefetch=n_prefetch,
        grid=(n_tiles,),
        in_specs=[
            pl.BlockSpec(memory_space=pl.ANY),
            pl.BlockSpec((1, D_MODEL), lambda i, *_: (0, 0)),
            pl.BlockSpec((D_MODEL, LANES), lambda i, *_: (0, 0)),
            pl.BlockSpec((1, LANES), lambda i, *_: (0, 0)),
            pl.BlockSpec((EPG, D_MODEL, D_EXPERT), lambda i, grp, *_: (grp[i], 0, 0)),
            pl.BlockSpec((EPG, D_MODEL, D_EXPERT), lambda i, grp, *_: (grp[i], 0, 0)),
            pl.BlockSpec((EPG, D_EXPERT, D_MODEL), lambda i, grp, *_: (grp[i], 0, 0)),
        ],
        out_specs=pl.BlockSpec(memory_space=pl.ANY),
        scratch_shapes=[pltpu.VMEM((2, TM * D_ROWS, LANES), F32),
                        pltpu.VMEM((2, TM * D_ROWS, LANES), F32),
                        pltpu.SemaphoreType.DMA((2,)),
                        pltpu.SemaphoreType.DMA((2,))],
    )
    return pl.pallas_call(
        _moe_body,
        grid_spec=grid_spec,
        out_shape=jax.ShapeDtypeStruct(h1.shape, F32),
        compiler_params=pltpu.CompilerParams(
            dimension_semantics=("arbitrary",), vmem_limit_bytes=VMEM_LIMIT),
        name="moe",
    )(*tile_tabs, *run_tabs, h1, g_ffn, w_r, b_r, w_g, w_u, w_d)


def _ple(lp, h2, tile0, p, g_ple, w_gate, w_proj, g_fin, name, seq_rows=None):
    n_tiles = p.shape[0] // TP
    win_tiles = min(WIN_TILES, n_tiles)
    assert tile0 % WIN_TILES == 0 and n_tiles % win_tiles == 0
    extra = [] if seq_rows is None else [pltpu.VMEM((D_ROWS, TP, LANES), F32)]
    return pl.pallas_call(
        functools.partial(_ple_body, tile0=tile0, win_tiles=win_tiles, seq_rows=seq_rows),
        grid=(n_tiles,),
        in_specs=[
            pl.BlockSpec((1, 1, WIN), lambda i: ((tile0 + i) // WIN_TILES, 0, 0), memory_space=pltpu.SMEM),
            pl.BlockSpec(memory_space=pl.ANY),
            pl.BlockSpec((TP, PLE_DIM), lambda i: (i, 0)),
            _full(g_ple.shape), _full(w_gate.shape), _full(w_proj.shape), _full(g_fin.shape),
        ],
        out_specs=pl.BlockSpec((TP, D_MODEL), lambda i: (i, 0)),
        out_shape=jax.ShapeDtypeStruct((n_tiles * TP, D_MODEL), F32),
        scratch_shapes=[pltpu.VMEM((TP // SUBLANES, GROUP_ROWS, LANES), F32),
                        pltpu.VMEM((2, win_tiles * TP * D_ROWS, LANES), F32),
                        pltpu.SemaphoreType.DMA((2,))] + extra,
        compiler_params=pltpu.CompilerParams(
            dimension_semantics=("arbitrary",), vmem_limit_bytes=VMEM_LIMIT),
        name=name,
    )(lp, h2, p, g_ple, w_gate, w_proj, g_fin)


def kernel(x_prompt, x_sample, p_prompt, p_sample, state_conv, norm_mix_g, w_in, conv_w, conv_b, conv_ln_g, conv_ln_b, gmlp_ln_g, gmlp_ln_b, w_s, b_s, w_out, norm_ffn_g, w_router_group, b_router_group, w_router_expert, b_router_expert, w_exp_gate, w_exp_up, w_exp_down, norm_ple_g, w_ple_gate, w_ple_proj, norm_final_g):
    depth = w_in.shape[0]
    assert depth == 1, "single-layer pipeline"
    batch, seq, _ = x_prompt.shape
    dec_batch, dec_seq, _ = x_sample.shape
    assert seq % TP == 0 and TP % CHUNK == 0 and TP % dec_seq == 0 and dec_batch % (TP // dec_seq) == 0
    sblk = TP // dec_seq
    n_prompt = batch * seq
    n_sample = dec_batch * dec_seq
    n_tok = n_prompt + n_sample

    row = lambda a: a.reshape(1, -1)
    w_in_b = w_in[0].astype(BF16)
    w_out_b = w_out[0].astype(BF16)
    cw = jnp.concatenate([conv_w[0], jnp.zeros((1, C_CONV), F32)], axis=0)
    tril = jnp.asarray(np.tril(np.ones((CHUNK, CHUNK), bool)))
    ws_m = jnp.where(tril[None], w_s[0], 0.0)
    ws_cat = jnp.concatenate([ws_m[0::2], ws_m[1::2]], axis=2).astype(BF16)
    bs_lane = jnp.repeat(jnp.transpose(b_s[0]), HEAD_DIM, axis=1)
    w_rt = jnp.zeros((CLS_ROWS, D_MODEL), F32)
    w_rt = w_rt.at[0:N_GROUPS].set(jnp.transpose(w_router_group[0]))
    w_rt = w_rt.at[N_GROUPS:N_GROUPS + N_EXPERTS].set(jnp.transpose(w_router_expert[0])).astype(BF16)
    b_r = jnp.zeros((CLS_ROWS, 1), F32)
    b_r = b_r.at[0:N_GROUPS, 0].set(b_router_group[0]).at[N_GROUPS:N_GROUPS + N_EXPERTS, 0].set(b_router_expert[0])
    n_logit = N_GROUPS + N_EXPERTS
    w_rn = jnp.concatenate([w_router_group[0], w_router_expert[0], jnp.zeros((D_MODEL, LANES - n_logit), F32)],
                           axis=1).astype(BF16)
    b_rn = jnp.concatenate([b_router_group[0], b_router_expert[0], jnp.zeros((LANES - n_logit,), F32)]).reshape(1, LANES)
    tri = jnp.asarray(np.triu(np.ones((TP, TP), np.float32)), dtype=BF16)
    wsl = jnp.where(jnp.asarray(np.tril(np.ones((dec_seq, dec_seq), bool)))[None], w_s[0][:, :dec_seq, :dec_seq], 0.0)
    wsl = jnp.repeat(jnp.transpose(wsl, (1, 2, 0)).reshape(dec_seq * dec_seq, N_HEADS), HEAD_DIM, axis=1)
    bsl = jnp.repeat(jnp.transpose(b_s[0][:, :dec_seq]), HEAD_DIM, axis=1)

    weights = (row(norm_mix_g[0]), w_in_b, cw, row(conv_b[0]), row(conv_ln_g[0]), row(conv_ln_b[0]),
               row(gmlp_ln_g[0]), row(gmlp_ln_b[0]), ws_cat, bs_lane, wsl, bsl,
               w_out_b, row(norm_ffn_g[0]), w_rt, b_r)

    hist_tm = jnp.transpose(state_conv[0], (1, 0, 2))
    h1, lp, tab, cst_p, cst_s, v_s, w_g, w_u, w_d = _mixer(
        x_prompt, x_sample, hist_tm, weights, tri, (w_exp_gate[0], w_exp_up[0], w_exp_down[0]))

    rank0 = tab[:, :N_CLASSES, 0].astype(I32)
    cnt = tab[:, :N_CLASSES, 1].astype(I32)
    lsrc = tab[:, :N_CLASSES, 2].astype(I32) + (jnp.arange(tab.shape[0], dtype=I32) * WIN)[:, None]
    total = rank0[-1] + cnt[-1]
    padded = ((total + TM - 1) // TM) * TM
    ids = jnp.arange(N_CLASSES, dtype=I32)
    ends = jnp.sum(jnp.where(ids[None, :] <= ids[:, None], padded[None, :], 0), axis=1)
    offs = ends - padded
    all_rows = jnp.sum(padded)
    n_tiles = (n_tok + N_CLASSES * (TM - 1)) // TM + 2
    tile_start = jnp.arange(n_tiles, dtype=I32) * TM
    used = tile_start < all_rows
    last_cls = jnp.max(jnp.where(padded > 0, ids, 0))
    tile_cls = jnp.sum((tile_start[:, None] >= ends[None, :]).astype(I32), axis=1)
    tile_cls = jnp.where(used, tile_cls, last_cls)
    of_tile = tile_cls[:, None] == ids[None, :]
    tile_r0 = tile_start - jnp.sum(jnp.where(of_tile, offs[None, :], 0), axis=1)
    tile_total = jnp.sum(jnp.where(of_tile, total[None, :], 0), axis=1)
    tile_nval = jnp.where(used, jnp.clip(tile_total - tile_r0, 0, TM), 0).astype(I32)
    pair = tile_cls % 6
    assert _PAIRS == ((0, 1), (0, 2), (0, 3), (1, 2), (1, 3), (2, 3))
    tile_lo = jnp.where(pair < 3, 0, jnp.where(pair < 5, 1, 2)).astype(I32)
    tile_hi = jnp.where(pair == 0, 1, jnp.where(jnp.logical_or(pair == 1, pair == 3), 2, 3)).astype(I32)
    run_beg = jnp.sum(jnp.where(of_tile[None], rank0[:, None, :], 0), axis=2)
    run_end = run_beg + jnp.sum(jnp.where(of_tile[None], cnt[:, None, :], 0), axis=2)
    tile_jlo = jnp.sum((run_end <= tile_r0[None, :]).astype(I32), axis=0)
    tile_jhi = jnp.sum((run_beg < (tile_r0 + tile_nval)[None, :]).astype(I32), axis=0)
    tile_jhi = jnp.maximum(tile_jhi, tile_jlo)

    h2 = _moe((tile_cls // 6, tile_lo, tile_hi, tile_nval, tile_cls, tile_r0, tile_jlo, tile_jhi),
              (rank0.reshape(-1), cnt.reshape(-1), lsrc.reshape(-1)),
              h1, row(norm_ffn_g[0]), w_rn, b_rn, w_g, w_u, w_d)

    lp = lp[:, 0:1, :]
    ple_w = (row(norm_ple_g[0]), w_ple_gate[0].astype(BF16), w_ple_proj[0].astype(BF16), row(norm_final_g))
    y_p = _ple(lp, h2, 0, p_prompt[0].reshape(n_prompt, PLE_DIM), *ple_w, name="ple_prompt")
    p_s_tm = jnp.transpose(p_sample[0].reshape(dec_batch // sblk, sblk, dec_seq, PLE_DIM), (0, 2, 1, 3))
    y_s = _ple(lp, h2, n_prompt // TP, p_s_tm.reshape(n_sample, PLE_DIM), *ple_w, name="ple_sample",
               seq_rows=(sblk, dec_seq))

    y_prompt = y_p.reshape(batch, seq, D_MODEL)
    y_sample = y_s.reshape(dec_batch, dec_seq, D_MODEL)
    state_conv_prompt = cst_p[:, HIST_OFF:, :][None]
    state_conv_sample = jnp.transpose(cst_s, (1, 0, 2))[None]
    state_gmlp_v_sample = jnp.transpose(v_s, (1, 0, 2))[None]
    return (y_prompt, y_sample, state_conv_prompt, state_conv_sample, state_gmlp_v_sample)
```

```python
import functools

import numpy as np
import jax
import jax.numpy as jnp
from jax import lax
from jax.experimental import pallas as pl
from jax.experimental.pallas import tpu as pltpu

F32 = jnp.float32
BF16 = jnp.bfloat16
I32 = jnp.int32

D_MODEL = 1024
C_CONV = 512
C_GMLP = 512
N_HEADS = 8
HEAD_DIM = 64
CONV_WIDTH = 31
CHUNK = 128
PLE_DIM = 256
N_GROUPS = 4
EPG = 4
N_EXPERTS = 16
D_EXPERT = 512
EPS = 1e-6
LANES = 128
SUBLANES = 8

N_CLASSES = N_GROUPS * 6
CLS_ROWS = 32
HIST = 32
HIST_OFF = HIST - (CONV_WIDTH - 1)
D_ROWS = D_MODEL // LANES
assert D_ROWS == SUBLANES
GROUP_ROWS = D_ROWS * SUBLANES

TP = 512
WIN_TILES = 4
WIN = WIN_TILES * TP
TM = 256
CONV_ROWS = 64
CONV_PITCH = 2
PIECES = tuple(TM >> b for b in range(TM.bit_length()))
LARGE_PIECE = 32

VMEM_LIMIT = 58 * 1024 * 1024

_PAIRS = ((0, 1), (0, 2), (0, 3), (1, 2), (1, 3), (2, 3))


def _rms(x, g):
    ms = jnp.mean(x * x, axis=-1, keepdims=True)
    return x * lax.rsqrt(ms + EPS) * g


def _ln(x, g, b):
    mu = jnp.mean(x, axis=-1, keepdims=True)
    xc = x - mu
    var = jnp.mean(xc * xc, axis=-1, keepdims=True)
    return xc * lax.rsqrt(var + EPS) * g + b


def _max4(v):
    return jnp.maximum(jnp.maximum(v[0], v[1]), jnp.maximum(v[2], v[3]))


def _first4(v, m):
    return jnp.where(v[0] == m, 0, jnp.where(v[1] == m, 1, jnp.where(v[2] == m, 2, 3))).astype(I32)


def _route(lt):
    gl = [lt[i:i + 1, :] for i in range(N_GROUPS)]
    g = _first4(gl, _max4(gl))
    a = []
    for j in range(EPG):
        rows = [lt[N_GROUPS + EPG * q + j:N_GROUPS + EPG * q + j + 1, :] for q in range(N_GROUPS)]
        a.append(jnp.where(g == 0, rows[0], jnp.where(g == 1, rows[1], jnp.where(g == 2, rows[2], rows[3]))))
    v1 = _max4(a)
    i1 = _first4(a, v1)
    a2 = [jnp.where(i1 == j, -jnp.inf, a[j]) for j in range(EPG)]
    i2 = _first4(a2, _max4(a2))
    lo = jnp.minimum(i1, i2)
    hi = jnp.maximum(i1, i2)
    pair = jnp.where(lo == 0, hi - 1, jnp.where(lo == 1, hi + 1, 5))
    return g * 6 + pair


def _combine_weights(logits, grp, e_lo, e_hi):
    lane = lax.broadcasted_iota(I32, logits.shape, 1)
    is_group = lane < N_GROUPS
    m = jnp.max(jnp.where(is_group, logits, -jnp.inf), axis=1, keepdims=True)
    den = jnp.sum(jnp.where(is_group, jnp.exp(logits - m), 0.0), axis=1, keepdims=True)
    base = N_GROUPS + EPG * grp
    v_lo = jnp.sum(jnp.where(lane == base + e_lo, logits, 0.0), axis=1, keepdims=True)
    v_hi = jnp.sum(jnp.where(lane == base + e_hi, logits, 0.0), axis=1, keepdims=True)
    top = jnp.maximum(v_lo, v_hi)
    p_lo = jnp.exp(v_lo - top)
    p_hi = jnp.exp(v_hi - top)
    scale = 1.0 / (den * (p_lo + p_hi))
    return p_lo * scale, p_hi * scale


def _load_token_blocks(ref, n_tok):
    return jnp.concatenate([ref[pl.ds(s, n_tok, stride=D_ROWS), :] for s in range(D_ROWS)], axis=1)


def _store_token_blocks(ref, x, n_tok):
    for s in range(D_ROWS):
        ref[pl.ds(s, n_tok, stride=D_ROWS), :] = x[:, s * LANES:(s + 1) * LANES]


def _route_tile(h, g_ffn_ref, w_rt_ref, b_r_ref, stage_h, cls_scr, sub):
    c = _rms(h, g_ffn_ref[...]).astype(BF16)
    lt = lax.dot_general(w_rt_ref[...], c, (((1,), (1,)), ((), ())), preferred_element_type=F32)
    cls_scr[sub] = jnp.broadcast_to(_route(lt + b_r_ref[...]), (SUBLANES, TP))
    ng = TP // SUBLANES
    for s in range(D_ROWS):
        stage_h[sub, :, s * SUBLANES:(s + 1) * SUBLANES, :] = h[:, s * LANES:(s + 1) * LANES].reshape(
            ng, SUBLANES, LANES)


def _sort_window(tri_ref, carry_ref, h1_hbm, h1_buf, h1_sem, lp_ref, tab_ref, stage_h, cls_scr, lp_vmem, lp_smem,
                 w, is_last_step, n_sub):
    rows = lax.broadcasted_iota(I32, (CLS_ROWS, TP), 0)
    onehot, counts = [], []
    lstart = jnp.zeros((CLS_ROWS, 1), F32)
    if n_sub < WIN_TILES:
        lp_ref[0, :, n_sub * TP:] = jnp.zeros((SUBLANES, (WIN_TILES - n_sub) * TP), I32)
        lp_vmem[:, n_sub * TP:] = jnp.zeros((SUBLANES, (WIN_TILES - n_sub) * TP), I32)
    for s in range(n_sub):
        cls = cls_scr[s, 0:1, :]
        onehot.append(jnp.where(rows == cls, 1.0, 0.0).astype(F32))
        counts.append(jnp.sum(onehot[s], axis=1, keepdims=True))
        lstart = lstart + jnp.sum(jnp.where(cls < rows, 1.0, 0.0).astype(F32), axis=1, keepdims=True)
    base = lstart
    for s in range(n_sub):
        pre = jnp.dot(onehot[s].astype(BF16), tri_ref[...], preferred_element_type=F32)
        lpos = jnp.sum(onehot[s] * (pre - 1.0 + base), axis=0, keepdims=True).astype(I32)
        lpos = lpos * D_ROWS
        lp_ref[0, :, s * TP:(s + 1) * TP] = jnp.broadcast_to(lpos, (SUBLANES, TP))
        lp_vmem[:, s * TP:(s + 1) * TP] = jnp.broadcast_to(lpos, (SUBLANES, TP))
        base = base + counts[s]
    tot = base - lstart
    carry = carry_ref[:, 0:1]
    lane = lax.broadcasted_iota(I32, (CLS_ROWS, LANES), 1)
    tab_ref[0] = jnp.where(lane == 0, carry, jnp.where(lane == 1, tot, jnp.where(lane == 2, lstart, 0.0)))
    carry_ref[...] = jnp.broadcast_to(carry + tot, (CLS_ROWS, LANES))
    pltpu.sync_copy(lp_vmem.at[pl.ds(0, 1)], lp_smem)

    def window_copy(win, n_tiles):
        rows = n_tiles * TP * D_ROWS
        start = win * (WIN * D_ROWS)
        start = start if isinstance(start, int) else pl.multiple_of(start, D_ROWS)
        return pltpu.make_async_copy(h1_buf.at[pl.ds(0, rows), :], h1_hbm.at[pl.ds(start, rows), :], h1_sem)

    @pl.when(w > 0)
    def _():
        window_copy(0, WIN_TILES).wait()

    for s in range(n_sub):
        def group(g, carry_, s=s):
            for k in range(SUBLANES):
                dst = pl.multiple_of(lp_smem[0, s * TP + g * SUBLANES + k], D_ROWS)
                h1_buf[pl.ds(dst, D_ROWS), :] = stage_h[s, g, pl.ds(k, D_ROWS, stride=SUBLANES), :]
            return carry_

        lax.fori_loop(0, TP // SUBLANES, group, 0, unroll=4)

    window_copy(w, n_sub).start()

    @pl.when(is_last_step)
    def _():
        window_copy(w, n_sub).wait()


def _mixer_front(x, g_mix_ref, w_in_ref):
    a = _rms(x, g_mix_ref[...]).astype(BF16)
    z = jnp.dot(a, w_in_ref[...], preferred_element_type=F32)
    glu = z[:, 0:C_CONV] * jax.nn.sigmoid(z[:, C_CONV:2 * C_CONV])
    u = jax.nn.gelu(z[:, 2 * C_CONV:2 * C_CONV + C_GMLP])
    gv = jax.nn.gelu(z[:, 2 * C_CONV + C_GMLP:])
    return glu, u, gv


def _mixer_body(xp_ref, xs_ref, hist_ref, g_mix_ref, w_in_ref, cw_ref, cb_ref, clg_ref, clb_ref, glg_ref, glb_ref,
                ws_ref, bs_ref, wsl_ref, bsl_ref, w_out_ref, g_ffn_ref, w_rt_ref, b_r_ref, tri_ref,
                eg_ref, eu_ref, ed_ref,
                h1_hbm, lp_ref, tab_ref, cstp_ref, csts_ref, vs_ref, egb_ref, eub_ref, edb_ref,
                glu_scr, xs_scr, cat_scr, carry_ref, stage_h, cls_scr, lp_vmem, lp_smem, h1_buf, h1_sem,
                *, n_prompt_tiles, nj, dec_seq, sblk, n_tail_tiles):
    i = pl.program_id(0)
    sub = lax.rem(i, WIN_TILES)

    egb_ref[...] = eg_ref[...].astype(BF16)
    eub_ref[...] = eu_ref[...].astype(BF16)
    edb_ref[...] = ed_ref[...].astype(BF16)

    @pl.when(i == 0)
    def _():
        carry_ref[...] = jnp.zeros_like(carry_ref)

    def tail(x):
        h = x + jnp.dot(cat_scr[...], w_out_ref[...], preferred_element_type=F32)
        _route_tile(h, g_ffn_ref, w_rt_ref, b_r_ref, stage_h, cls_scr, sub)

    @pl.when(i < n_prompt_tiles)
    def _prompt():
        j = lax.rem(i, nj)
        x = xp_ref[0]
        glu, u, gv = _mixer_front(x, g_mix_ref, w_in_ref)

        def time_rows(t0, n):
            return pl.ds(CONV_PITCH * t0, n, stride=CONV_PITCH)

        @pl.when(j == 0)
        def _():
            for s in range(C_CONV // LANES):
                glu_scr[s, time_rows(0, HIST), :] = jnp.zeros((HIST, LANES), F32)

        @pl.when(j > 0)
        def _():
            for s in range(C_CONV // LANES):
                glu_scr[s, time_rows(0, HIST), :] = glu_scr[s, time_rows(TP, HIST), :]

        for s in range(C_CONV // LANES):
            glu_scr[s, time_rows(HIST, TP), :] = glu[:, s * LANES:(s + 1) * LANES]
        cstp_ref[0] = jnp.concatenate([glu_scr[s, time_rows(TP, HIST), :] for s in range(C_CONV // LANES)], axis=1)

        cb = cb_ref[...]
        for r0 in range(0, TP, CONV_ROWS):
            slabs = []
            for s in range(C_CONV // LANES):
                acc = None
                for k in range(CONV_WIDTH):
                    term = (cw_ref[k:k + 1, s * LANES:(s + 1) * LANES]
                            * glu_scr[s, time_rows(r0 + HIST_OFF + k, CONV_ROWS), :])
                    acc = term if acc is None else acc + term
                slabs.append(acc)
            ya = jax.nn.silu(_ln(jnp.concatenate(slabs, axis=1) + cb, clg_ref[...], clb_ref[...]))
            cat_scr[r0:r0 + CONV_ROWS, 0:C_CONV] = ya.astype(BF16)

        v = _ln(gv, glg_ref[...], glb_ref[...])
        vb = v.astype(BF16)
        lane = lax.broadcasted_iota(I32, (CHUNK, LANES), 1)
        is_lo = lane < HEAD_DIM
        zero = jnp.zeros((CHUNK, LANES), BF16)
        for c in range(TP // CHUNK):
            for p in range(N_HEADS // 2):
                blk = vb[c * CHUNK:(c + 1) * CHUNK, p * LANES:(p + 1) * LANES]
                rhs = jnp.concatenate([jnp.where(is_lo, blk, zero), jnp.where(is_lo, zero, blk)], axis=0)
                mixed = jnp.dot(ws_ref[p], rhs, preferred_element_type=F32) + bs_ref[:, p * LANES:(p + 1) * LANES]
                yb = u[c * CHUNK:(c + 1) * CHUNK, p * LANES:(p + 1) * LANES] * mixed
                cat_scr[c * CHUNK:(c + 1) * CHUNK, C_CONV + p * LANES:C_CONV + (p + 1) * LANES] = yb.astype(BF16)
        tail(x)

    @pl.when(i >= n_prompt_tiles)
    def _sample():
        x_nat = xs_ref[...].reshape(TP, D_MODEL)
        slabs = []
        for s in range(D_ROWS):
            half, row0 = s // 2, (s % 2) * TP
            glu_scr[half, row0:row0 + TP, :] = x_nat[:, s * LANES:(s + 1) * LANES]
            slabs.append(jnp.concatenate(
                [glu_scr[half, pl.ds(row0 + t, sblk, stride=dec_seq), :] for t in range(dec_seq)], axis=0))
        x = jnp.concatenate(slabs, axis=1)
        glu, u, gv = _mixer_front(x, g_mix_ref, w_in_ref)

        nh = CONV_WIDTH - 1
        xs_scr[0:nh] = hist_ref[...]
        xs_scr[nh:nh + dec_seq] = glu.reshape(dec_seq, sblk, C_CONV)
        csts_ref[...] = xs_scr[dec_seq:dec_seq + nh]

        cb = cb_ref[...]
        for t in range(dec_seq):
            acc = jnp.zeros((sblk, C_CONV), F32)
            for k in range(CONV_WIDTH):
                acc = acc + cw_ref[k:k + 1, :] * xs_scr[t + k]
            ya = jax.nn.silu(_ln(acc + cb, clg_ref[...], clb_ref[...]))
            cat_scr[t * sblk:(t + 1) * sblk, 0:C_CONV] = ya.astype(BF16)

        v = _ln(gv, glg_ref[...], glb_ref[...])
        vs_ref[...] = v.reshape(dec_seq, sblk, C_GMLP)
        for t in range(dec_seq):
            mixed = jnp.broadcast_to(bsl_ref[t:t + 1, :], (sblk, C_GMLP))
            for tp in range(t + 1):
                r = t * dec_seq + tp
                mixed = mixed + wsl_ref[r:r + 1, :] * v[tp * sblk:(tp + 1) * sblk, :]
            yb = u[t * sblk:(t + 1) * sblk, :] * mixed
            cat_scr[t * sblk:(t + 1) * sblk, C_CONV:] = yb.astype(BF16)
        tail(x)

    is_last_step = i == pl.num_programs(0) - 1
    sort_args = (tri_ref, carry_ref, h1_hbm, h1_buf, h1_sem, lp_ref, tab_ref, stage_h, cls_scr, lp_vmem, lp_smem,
                 i // WIN_TILES, is_last_step)

    @pl.when(sub == WIN_TILES - 1)
    def _():
        _sort_window(*sort_args, WIN_TILES)

    if n_tail_tiles:
        @pl.when(is_last_step)
        def _():
            _sort_window(*sort_args, n_tail_tiles)


def _for_pieces(ln, fn):
    def emit(sizes):
        for size in sizes:
            shift = size.bit_length()
            off = lax.shift_left(lax.shift_right_logical(ln, shift), shift)

            @pl.when((ln & size) != 0)
            def _(off=off, size=size):
                fn(off, size)

    large = tuple(s for s in PIECES if s >= LARGE_PIECE)

    @pl.when(ln >= LARGE_PIECE)
    def _():
        emit(large)

    emit(tuple(s for s in PIECES if s < LARGE_PIECE))


def _for_runs(k, tcls_ref, r0_ref, nval_ref, jlo_ref, jhi_ref, rank0_ref, cnt_ref, lsrc_ref, fn):
    c = tcls_ref[k]
    ra = r0_ref[k]
    rb = ra + nval_ref[k]

    def body(j, carry):
        idx = j * N_CLASSES + c
        s = rank0_ref[idx]
        lo = jnp.maximum(s, ra)
        hi = jnp.minimum(s + cnt_ref[idx], rb)
        ln = hi - lo

        @pl.when(ln > 0)
        def _():
            src = lsrc_ref[idx] + (lo - s)
            dst = lo - ra
            _for_pieces(ln, lambda off, size: fn(src + off, dst + off, size))

        return carry

    lax.fori_loop(jlo_ref[k], jhi_ref[k], body, 0)


def _blocks(ref, first, n):
    start = first * D_ROWS if isinstance(first, int) else pl.multiple_of(first * D_ROWS, D_ROWS)
    return ref.at[pl.ds(start, n * D_ROWS), :]


def _moe_body(grp_ref, lo_ref, hi_ref, nval_ref, tcls_ref, r0_ref, jlo_ref, jhi_ref, rank0_ref, cnt_ref, lsrc_ref,
              h1_hbm, g_ffn_ref, w_r_ref, b_r_ref, wg_ref, wu_ref, wd_ref,
              h2_hbm, hbuf, obuf, gsem, ssem):
    k = pl.program_id(0)
    n = pl.num_programs(0)
    slot = lax.rem(k, 2)
    tables = (tcls_ref, r0_ref, nval_ref, jlo_ref, jhi_ref, rank0_ref, cnt_ref, lsrc_ref)

    def gather_tile(tile, to_slot):
        def copy(src, dst, size):
            pltpu.make_async_copy(_blocks(h1_hbm, src, size), _blocks(hbuf.at[to_slot], dst, size),
                                  gsem.at[to_slot]).start()
        _for_runs(tile, *tables, copy)

    def wait_rows(n_rows, make_copy):
        _for_pieces(n_rows, lambda off, size: make_copy(size).wait())

    @pl.when(k == 0)
    def _():
        hbuf[...] = jnp.zeros_like(hbuf)
        gather_tile(0, 0)

    @pl.when(jnp.logical_and(k >= 2, nval_ref[jnp.maximum(k - 2, 0)] > 0))
    def _():
        wait_rows(nval_ref[jnp.maximum(k - 2, 0)],
                  lambda size: pltpu.make_async_copy(_blocks(obuf.at[slot], 0, size), _blocks(h2_hbm, 0, size),
                                                     ssem.at[slot]))

    nxt = jnp.minimum(k + 1, n - 1)

    @pl.when(jnp.logical_and(k + 1 < n, nval_ref[nxt] > 0))
    def _():
        gather_tile(nxt, 1 - slot)

    @pl.when(nval_ref[k] > 0)
    def _():
        wait_rows(nval_ref[k],
                  lambda size: pltpu.make_async_copy(_blocks(h1_hbm, 0, size), _blocks(hbuf.at[slot], 0, size),
                                                     gsem.at[slot]))
        e_lo = lo_ref[k]
        e_hi = hi_ref[k]

        def experts(m):
            h = _load_token_blocks(hbuf.at[slot], m)
            c = _rms(h, g_ffn_ref[...]).astype(BF16)
            logits = jnp.dot(c, w_r_ref[...], preferred_element_type=F32) + b_r_ref[...]
            wl, wh = _combine_weights(logits, grp_ref[k], e_lo, e_hi)

            def hidden(e):
                gate = jnp.dot(c, wg_ref[e], preferred_element_type=F32)
                return (jax.nn.silu(gate) * jnp.dot(c, wu_ref[e], preferred_element_type=F32)).astype(BF16)

            yl = jnp.dot(hidden(e_lo), wd_ref[e_lo], preferred_element_type=F32)
            yh = jnp.dot(hidden(e_hi), wd_ref[e_hi], preferred_element_type=F32)
            _store_token_blocks(obuf.at[slot], h + (wl * yl + wh * yh), m)

        @pl.when(nval_ref[k] > TM // 2)
        def _():
            experts(TM)

        @pl.when(nval_ref[k] <= TM // 2)
        def _():
            experts(TM // 2)

        def copy_back(src, dst, size):
            pltpu.make_async_copy(_blocks(obuf.at[slot], dst, size), _blocks(h2_hbm, src, size),
                                  ssem.at[slot]).start()
        _for_runs(k, *tables, copy_back)


def _ple_body(lp_ref, h2_hbm, p_ref, g_ple_ref, w_gate_ref, w_proj_ref, g_fin_ref, y_ref, stage_h, win_buf, win_sem,
              *rest, tile0, win_tiles, seq_rows):
    ng = TP // SUBLANES
    i = pl.program_id(0)
    n = pl.num_programs(0)
    sub = lax.rem(i, win_tiles)
    w_local = i // win_tiles
    slot = lax.rem(w_local, 2)
    first = sub * TP

    def window_copy(w, to_slot):
        rows = win_tiles * TP * D_ROWS
        start = pl.multiple_of((tile0 // WIN_TILES + w) * (WIN * D_ROWS), D_ROWS)
        return pltpu.make_async_copy(h2_hbm.at[pl.ds(start, rows), :], win_buf.at[to_slot, pl.ds(0, rows), :],
                                     win_sem.at[to_slot])

    @pl.when(i == 0)
    def _():
        window_copy(0, 0).start()

    @pl.when(sub == 0)
    def _():
        window_copy(w_local, slot).wait()

        @pl.when(i + win_tiles < n)
        def _():
            window_copy(w_local + 1, 1 - slot).start()

    def group(g, carry):
        for k in range(SUBLANES):
            src = pl.multiple_of(lp_ref[0, 0, first + g * SUBLANES + k], D_ROWS)
            stage_h[g, pl.ds(k, D_ROWS, stride=SUBLANES), :] = win_buf[slot, pl.ds(src, D_ROWS), :]
        return carry

    lax.fori_loop(0, ng, group, 0, unroll=4)
    h = jnp.concatenate([stage_h[:, s * SUBLANES:(s + 1) * SUBLANES, :].reshape(TP, LANES) for s in range(D_ROWS)],
                        axis=1)
    c = _rms(h, g_ple_ref[...]).astype(BF16)
    gate = jax.nn.sigmoid(jnp.dot(c, w_gate_ref[...], preferred_element_type=F32))
    proj = jnp.dot(p_ref[...].astype(BF16), w_proj_ref[...], preferred_element_type=F32)
    y = _rms(h + proj * gate, g_fin_ref[...])
    if seq_rows is None:
        y_ref[...] = y
    else:
        sblk, dec_seq = seq_rows
        out_scr = rest[0]
        for s in range(D_ROWS):
            for t in range(dec_seq):
                out_scr[s, pl.ds(t, sblk, stride=dec_seq), :] = y[t * sblk:(t + 1) * sblk, s * LANES:(s + 1) * LANES]
            y_ref[:, s * LANES:(s + 1) * LANES] = out_scr[s]


def _full(shape, single=False):
    nd = len(shape)
    if single:
        return pl.BlockSpec(shape, lambda *_: (0,) * nd, pipeline_mode=pl.Buffered(1))
    return pl.BlockSpec(shape, lambda *_: (0,) * nd)


def _mixer(x_prompt, x_sample, hist_tm, weights, tri, expert_w):
    batch, seq, _ = x_prompt.shape
    dec_batch, dec_seq, _ = x_sample.shape
    nj = seq // TP
    npt = batch * nj
    sblk = TP // dec_seq
    nst = dec_batch // sblk
    n_tiles = npt + nst
    n_tail = n_tiles % WIN_TILES
    assert npt % WIN_TILES == 0 and n_tail <= nst
    n_windows = pl.cdiv(n_tiles, WIN_TILES)
    nh = CONV_WIDTH - 1
    w_specs = [_full(w.shape, single=True) for w in weights]
    body = functools.partial(_mixer_body, n_prompt_tiles=npt, nj=nj, dec_seq=dec_seq, sblk=sblk, n_tail_tiles=n_tail)

    def p_idx(i):
        return jnp.minimum(i, npt - 1)

    def s_idx(i):
        return jnp.maximum(i - npt, 0)

    assert npt % N_EXPERTS == 0
    parts = npt // N_EXPERTS
    e_specs = [pl.BlockSpec((1, w.shape[1] // parts, w.shape[2]), lambda i: (p_idx(i) // parts, p_idx(i) % parts, 0))
               for w in expert_w]
    e_shapes = [jax.ShapeDtypeStruct(w.shape, BF16) for w in expert_w]

    return pl.pallas_call(
        body,
        grid=(n_tiles,),
        in_specs=[pl.BlockSpec((1, TP, D_MODEL), lambda i: (p_idx(i) // nj, p_idx(i) % nj, 0)),
                  pl.BlockSpec((sblk, dec_seq, D_MODEL), lambda i: (s_idx(i), 0, 0), pipeline_mode=pl.Buffered(1)),
                  pl.BlockSpec((nh, sblk, C_CONV), lambda i: (0, s_idx(i), 0), pipeline_mode=pl.Buffered(1))]
        + w_specs + [_full(tri.shape, single=True)] + e_specs,
        out_specs=[
            pl.BlockSpec(memory_space=pl.ANY),
            pl.BlockSpec((1, SUBLANES, WIN), lambda i: (i // WIN_TILES, 0, 0)),
            pl.BlockSpec((1, CLS_ROWS, LANES), lambda i: (i // WIN_TILES, 0, 0)),
            pl.BlockSpec((1, HIST, C_CONV), lambda i: (p_idx(i) // nj, 0, 0)),
            pl.BlockSpec((nh, sblk, C_CONV), lambda i: (0, s_idx(i), 0), pipeline_mode=pl.Buffered(1)),
            pl.BlockSpec((dec_seq, sblk, C_GMLP), lambda i: (0, s_idx(i), 0), pipeline_mode=pl.Buffered(1)),
        ] + e_specs,
        out_shape=[
            jax.ShapeDtypeStruct((n_tiles * TP * D_ROWS, LANES), F32),
            jax.ShapeDtypeStruct((n_windows, SUBLANES, WIN), I32),
            jax.ShapeDtypeStruct((n_windows, CLS_ROWS, LANES), F32),
            jax.ShapeDtypeStruct((batch, HIST, C_CONV), F32),
            jax.ShapeDtypeStruct((nh, dec_batch, C_CONV), F32),
            jax.ShapeDtypeStruct((dec_seq, dec_batch, C_GMLP), F32),
        ] + e_shapes,
        scratch_shapes=[
            pltpu.VMEM((C_CONV // LANES, CONV_PITCH * (HIST + TP), LANES), F32),
            pltpu.VMEM((nh + dec_seq, sblk, C_CONV), F32),
            pltpu.VMEM((TP, D_MODEL), BF16),
            pltpu.VMEM((CLS_ROWS, LANES), F32),
            pltpu.VMEM((WIN_TILES, TP // SUBLANES, GROUP_ROWS, LANES), F32),
            pltpu.VMEM((WIN_TILES, SUBLANES, TP), I32),
            pltpu.VMEM((SUBLANES, WIN), I32),
            pltpu.SMEM((1, WIN), I32),
            pltpu.VMEM((WIN * D_ROWS, LANES), F32),
            pltpu.SemaphoreType.DMA(()),
        ],
        compiler_params=pltpu.CompilerParams(
            dimension_semantics=("arbitrary",), vmem_limit_bytes=VMEM_LIMIT),
        name="mixer",
    )(x_prompt, x_sample, hist_tm, *weights, tri, *expert_w)


def _moe(tile_tabs, run_tabs, h1, g_ffn, w_r, b_r, w_g, w_u, w_d):
    n_tiles = tile_tabs[0].shape[0]
    n_prefetch = len(tile_tabs) + len(run_tabs)
    grid_spec = pltpu.PrefetchScalarGridSpec(
        num_scalar_prefetch=n_prefetch,
        grid=(n_tiles,),
        in_specs=[
            pl.BlockSpec(memory_space=pl.ANY),
            pl.BlockSpec((1, D_MODEL), lambda i, *_: (0, 0)),
            pl.BlockSpec((D_MODEL, LANES), lambda i, *_: (0, 0)),
            pl.BlockSpec((1, LANES), lambda i, *_: (0, 0)),
            pl.BlockSpec((EPG, D_MODEL, D_EXPERT), lambda i, grp, *_: (grp[i], 0, 0)),
            pl.BlockSpec((EPG, D_MODEL, D_EXPERT), lambda i, grp, *_: (grp[i], 0, 0)),
            pl.BlockSpec((EPG, D_EXPERT, D_MODEL), lambda i, grp, *_: (grp[i], 0, 0)),
        ],
        out_specs=pl.BlockSpec(memory_space=pl.ANY),
        scratch_shapes=[pltpu.VMEM((2, TM * D_ROWS, LANES), F32),
                        pltpu.VMEM((2, TM * D_ROWS, LANES), F32),
                        pltpu.SemaphoreType.DMA((2,)),
                        pltpu.SemaphoreType.DMA((2,))],
    )
    return pl.pallas_call(
        _moe_body,
        grid_spec=grid_spec,
        out_shape=jax.ShapeDtypeStruct(h1.shape, F32),
        compiler_params=pltpu.CompilerParams(
            dimension_semantics=("arbitrary",), vmem_limit_bytes=VMEM_LIMIT),
        name="moe",
    )(*tile_tabs, *run_tabs, h1, g_ffn, w_r, b_r, w_g, w_u, w_d)


def _ple(lp, h2, tile0, p, g_ple, w_gate, w_proj, g_fin, name, seq_rows=None):
    n_tiles = p.shape[0] // TP
    win_tiles = min(WIN_TILES, n_tiles)
    assert tile0 % WIN_TILES == 0 and n_tiles % win_tiles == 0
    extra = [] if seq_rows is None else [pltpu.VMEM((D_ROWS, TP, LANES), F32)]
    return pl.pallas_call(
        functools.partial(_ple_body, tile0=tile0, win_tiles=win_tiles, seq_rows=seq_rows),
        grid=(n_tiles,),
        in_specs=[
            pl.BlockSpec((1, 1, WIN), lambda i: ((tile0 + i) // WIN_TILES, 0, 0), memory_space=pltpu.SMEM),
            pl.BlockSpec(memory_space=pl.ANY),
            pl.BlockSpec((TP, PLE_DIM), lambda i: (i, 0)),
            _full(g_ple.shape), _full(w_gate.shape), _full(w_proj.shape), _full(g_fin.shape),
        ],
        out_specs=pl.BlockSpec((TP, D_MODEL), lambda i: (i, 0)),
        out_shape=jax.ShapeDtypeStruct((n_tiles * TP, D_MODEL), F32),
        scratch_shapes=[pltpu.VMEM((TP // SUBLANES, GROUP_ROWS, LANES), F32),
                        pltpu.VMEM((2, win_tiles * TP * D_ROWS, LANES), F32),
                        pltpu.SemaphoreType.DMA((2,))] + extra,
        compiler_params=pltpu.CompilerParams(
            dimension_semantics=("arbitrary",), vmem_limit_bytes=VMEM_LIMIT),
        name=name,
    )(lp, h2, p, g_ple, w_gate, w_proj, g_fin)


def kernel(x_prompt, x_sample, p_prompt, p_sample, state_conv, norm_mix_g, w_in, conv_w, conv_b, conv_ln_g, conv_ln_b, gmlp_ln_g, gmlp_ln_b, w_s, b_s, w_out, norm_ffn_g, w_router_group, b_router_group, w_router_expert, b_router_expert, w_exp_gate, w_exp_up, w_exp_down, norm_ple_g, w_ple_gate, w_ple_proj, norm_final_g):
    depth = w_in.shape[0]
    assert depth == 1, "single-layer pipeline"
    batch, seq, _ = x_prompt.shape
    dec_batch, dec_seq, _ = x_sample.shape
    assert seq % TP == 0 and TP % CHUNK == 0 and TP % dec_seq == 0 and dec_batch % (TP // dec_seq) == 0
    sblk = TP // dec_seq
    n_prompt = batch * seq
    n_sample = dec_batch * dec_seq
    n_tok = n_prompt + n_sample

    row = lambda a: a.reshape(1, -1)
    w_in_b = w_in[0].astype(BF16)
    w_out_b = w_out[0].astype(BF16)
    cw = jnp.concatenate([conv_w[0], jnp.zeros((1, C_CONV), F32)], axis=0)
    tril = jnp.asarray(np.tril(np.ones((CHUNK, CHUNK), bool)))
    ws_m = jnp.where(tril[None], w_s[0], 0.0)
    ws_cat = jnp.concatenate([ws_m[0::2], ws_m[1::2]], axis=2).astype(BF16)
    bs_lane = jnp.repeat(jnp.transpose(b_s[0]), HEAD_DIM, axis=1)
    w_rt = jnp.zeros((CLS_ROWS, D_MODEL), F32)
    w_rt = w_rt.at[0:N_GROUPS].set(jnp.transpose(w_router_group[0]))
    w_rt = w_rt.at[N_GROUPS:N_GROUPS + N_EXPERTS].set(jnp.transpose(w_router_expert[0])).astype(BF16)
    b_r = jnp.zeros((CLS_ROWS, 1), F32)
    b_r = b_r.at[0:N_GROUPS, 0].set(b_router_group[0]).at[N_GROUPS:N_GROUPS + N_EXPERTS, 0].set(b_router_expert[0])
    n_logit = N_GROUPS + N_EXPERTS
    w_rn = jnp.concatenate([w_router_group[0], w_router_expert[0], jnp.zeros((D_MODEL, LANES - n_logit), F32)],
                           axis=1).astype(BF16)
    b_rn = jnp.concatenate([b_router_group[0], b_router_expert[0], jnp.zeros((LANES - n_logit,), F32)]).reshape(1, LANES)
    tri = jnp.asarray(np.triu(np.ones((TP, TP), np.float32)), dtype=BF16)
    wsl = jnp.where(jnp.asarray(np.tril(np.ones((dec_seq, dec_seq), bool)))[None], w_s[0][:, :dec_seq, :dec_seq], 0.0)
    wsl = jnp.repeat(jnp.transpose(wsl, (1, 2, 0)).reshape(dec_seq * dec_seq, N_HEADS), HEAD_DIM, axis=1)
    bsl = jnp.repeat(jnp.transpose(b_s[0][:, :dec_seq]), HEAD_DIM, axis=1)

    weights = (row(norm_mix_g[0]), w_in_b, cw, row(conv_b[0]), row(conv_ln_g[0]), row(conv_ln_b[0]),
               row(gmlp_ln_g[0]), row(gmlp_ln_b[0]), ws_cat, bs_lane, wsl, bsl,
               w_out_b, row(norm_ffn_g[0]), w_rt, b_r)

    hist_tm = jnp.transpose(state_conv[0], (1, 0, 2))
    h1, lp, tab, cst_p, cst_s, v_s, w_g, w_u, w_d = _mixer(
        x_prompt, x_sample, hist_tm, weights, tri, (w_exp_gate[0], w_exp_up[0], w_exp_down[0]))

    rank0 = tab[:, :N_CLASSES, 0].astype(I32)
    cnt = tab[:, :N_CLASSES, 1].astype(I32)
    lsrc = tab[:, :N_CLASSES, 2].astype(I32) + (jnp.arange(tab.shape[0], dtype=I32) * WIN)[:, None]
    total = rank0[-1] + cnt[-1]
    padded = ((total + TM - 1) // TM) * TM
    ids = jnp.arange(N_CLASSES, dtype=I32)
    ends = jnp.sum(jnp.where(ids[None, :] <= ids[:, None], padded[None, :], 0), axis=1)
    offs = ends - padded
    all_rows = jnp.sum(padded)
    n_tiles = (n_tok + N_CLASSES * (TM - 1)) // TM + 2
    tile_start = jnp.arange(n_tiles, dtype=I32) * TM
    used = tile_start < all_rows
    last_cls = jnp.max(jnp.where(padded > 0, ids, 0))
    tile_cls = jnp.sum((tile_start[:, None] >= ends[None, :]).astype(I32), axis=1)
    tile_cls = jnp.where(used, tile_cls, last_cls)
    of_tile = tile_cls[:, None] == ids[None, :]
    tile_r0 = tile_start - jnp.sum(jnp.where(of_tile, offs[None, :], 0), axis=1)
    tile_total = jnp.sum(jnp.where(of_tile, total[None, :], 0), axis=1)
    tile_nval = jnp.where(used, jnp.clip(tile_total - tile_r0, 0, TM), 0).astype(I32)
    pair = tile_cls % 6
    assert _PAIRS == ((0, 1), (0, 2), (0, 3), (1, 2), (1, 3), (2, 3))
    tile_lo = jnp.where(pair < 3, 0, jnp.where(pair < 5, 1, 2)).astype(I32)
    tile_hi = jnp.where(pair == 0, 1, jnp.where(jnp.logical_or(pair == 1, pair == 3), 2, 3)).astype(I32)
    run_beg = jnp.sum(jnp.where(of_tile[None], rank0[:, None, :], 0), axis=2)
    run_end = run_beg + jnp.sum(jnp.where(of_tile[None], cnt[:, None, :], 0), axis=2)
    tile_jlo = jnp.sum((run_end <= tile_r0[None, :]).astype(I32), axis=0)
    tile_jhi = jnp.sum((run_beg < (tile_r0 + tile_nval)[None, :]).astype(I32), axis=0)
    tile_jhi = jnp.maximum(tile_jhi, tile_jlo)

    h2 = _moe((tile_cls // 6, tile_lo, tile_hi, tile_nval, tile_cls, tile_r0, tile_jlo, tile_jhi),
              (rank0.reshape(-1), cnt.reshape(-1), lsrc.reshape(-1)),
              h1, row(norm_ffn_g[0]), w_rn, b_rn, w_g, w_u, w_d)

    lp = lp[:, 0:1, :]
    ple_w = (row(norm_ple_g[0]), w_ple_gate[0].astype(BF16), w_ple_proj[0].astype(BF16), row(norm_final_g))
    y_p = _ple(lp, h2, 0, p_prompt[0].reshape(n_prompt, PLE_DIM), *ple_w, name="ple_prompt")
    p_s_tm = jnp.transpose(p_sample[0].reshape(dec_batch // sblk, sblk, dec_seq, PLE_DIM), (0, 2, 1, 3))
    y_s = _ple(lp, h2, n_prompt // TP, p_s_tm.reshape(n_sample, PLE_DIM), *ple_w, name="ple_sample",
               seq_rows=(sblk, dec_seq))

    y_prompt = y_p.reshape(batch, seq, D_MODEL)
    y_sample = y_s.reshape(dec_batch, dec_seq, D_MODEL)
    state_conv_prompt = cst_p[:, HIST_OFF:, :][None]
    state_conv_sample = jnp.transpose(cst_s, (1, 0, 2))[None]
    state_gmlp_v_sample = jnp.transpose(v_s, (1, 0, 2))[None]
    return (y_prompt, y_sample, state_conv_prompt, state_conv_sample, state_gmlp_v_sample)
```

```python
import functools

import numpy as np
import jax
import jax.numpy as jnp
from jax import lax
from jax.experimental import pallas as pl
from jax.experimental.pallas import tpu as pltpu

F32 = jnp.float32
BF16 = jnp.bfloat16
I32 = jnp.int32

D_MODEL = 1024
C_CONV = 512
C_GMLP = 512
N_HEADS = 8
HEAD_DIM = 64
CONV_WIDTH = 31
CHUNK = 128
PLE_DIM = 256
N_GROUPS = 4
EPG = 4
N_EXPERTS = 16
D_EXPERT = 512
EPS = 1e-6
LANES = 128
SUBLANES = 8

N_CLASSES = N_GROUPS * 6
CLS_ROWS = 32
HIST = 32
HIST_OFF = HIST - (CONV_WIDTH - 1)
D_ROWS = D_MODEL // LANES
assert D_ROWS == SUBLANES
GROUP_ROWS = D_ROWS * SUBLANES

TP = 512
WIN_TILES = 4
WIN = WIN_TILES * TP
TM = 256
CONV_ROWS = 64
CONV_PITCH = 2
PIECES = tuple(TM >> b for b in range(TM.bit_length()))
LARGE_PIECE = 32

VMEM_LIMIT = 58 * 1024 * 1024

_PAIRS = ((0, 1), (0, 2), (0, 3), (1, 2), (1, 3), (2, 3))


def _rms(x, g):
    ms = jnp.mean(x * x, axis=-1, keepdims=True)
    return x * lax.rsqrt(ms + EPS) * g


def _ln(x, g, b):
    mu = jnp.mean(x, axis=-1, keepdims=True)
    xc = x - mu
    var = jnp.mean(xc * xc, axis=-1, keepdims=True)
    return xc * lax.rsqrt(var + EPS) * g + b


def _max4(v):
    return jnp.maximum(jnp.maximum(v[0], v[1]), jnp.maximum(v[2], v[3]))


def _first4(v, m):
    return jnp.where(v[0] == m, 0, jnp.where(v[1] == m, 1, jnp.where(v[2] == m, 2, 3))).astype(I32)


def _route(lt):
    gl = [lt[i:i + 1, :] for i in range(N_GROUPS)]
    g = _first4(gl, _max4(gl))
    a = []
    for j in range(EPG):
        rows = [lt[N_GROUPS + EPG * q + j:N_GROUPS + EPG * q + j + 1, :] for q in range(N_GROUPS)]
        a.append(jnp.where(g == 0, rows[0], jnp.where(g == 1, rows[1], jnp.where(g == 2, rows[2], rows[3]))))
    v1 = _max4(a)
    i1 = _first4(a, v1)
    a2 = [jnp.where(i1 == j, -jnp.inf, a[j]) for j in range(EPG)]
    i2 = _first4(a2, _max4(a2))
    lo = jnp.minimum(i1, i2)
    hi = jnp.maximum(i1, i2)
    pair = jnp.where(lo == 0, hi - 1, jnp.where(lo == 1, hi + 1, 5))
    return g * 6 + pair


def _combine_weights(logits, grp, e_lo, e_hi):
    lane = lax.broadcasted_iota(I32, logits.shape, 1)
    is_group = lane < N_GROUPS
    m = jnp.max(jnp.where(is_group, logits, -jnp.inf), axis=1, keepdims=True)
    den = jnp.sum(jnp.where(is_group, jnp.exp(logits - m), 0.0), axis=1, keepdims=True)
    base = N_GROUPS + EPG * grp
    v_lo = jnp.sum(jnp.where(lane == base + e_lo, logits, 0.0), axis=1, keepdims=True)
    v_hi = jnp.sum(jnp.where(lane == base + e_hi, logits, 0.0), axis=1, keepdims=True)
    top = jnp.maximum(v_lo, v_hi)
    p_lo = jnp.exp(v_lo - top)
    p_hi = jnp.exp(v_hi - top)
    scale = 1.0 / (den * (p_lo + p_hi))
    return p_lo * scale, p_hi * scale


def _load_token_blocks(ref, n_tok):
    return jnp.concatenate([ref[pl.ds(s, n_tok, stride=D_ROWS), :] for s in range(D_ROWS)], axis=1)


def _store_token_blocks(ref, x, n_tok):
    for s in range(D_ROWS):
        ref[pl.ds(s, n_tok, stride=D_ROWS), :] = x[:, s * LANES:(s + 1) * LANES]


def _route_tile(h, g_ffn_ref, w_rt_ref, b_r_ref, stage_h, cls_scr, sub):
    c = _rms(h, g_ffn_ref[...]).astype(BF16)
    lt = lax.dot_general(w_rt_ref[...], c, (((1,), (1,)), ((), ())), preferred_element_type=F32)
    cls_scr[sub] = jnp.broadcast_to(_route(lt + b_r_ref[...]), (SUBLANES, TP))
    ng = TP // SUBLANES
    for s in range(D_ROWS):
        stage_h[sub, :, s * SUBLANES:(s + 1) * SUBLANES, :] = h[:, s * LANES:(s + 1) * LANES].reshape(
            ng, SUBLANES, LANES)


def _sort_window(tri_ref, carry_ref, h1_hbm, h1_buf, h1_sem, lp_ref, tab_ref, stage_h, cls_scr, lp_vmem, lp_smem,
                 w, is_last_step, n_sub):
    rows = lax.broadcasted_iota(I32, (CLS_ROWS, TP), 0)
    onehot, counts = [], []
    lstart = jnp.zeros((CLS_ROWS, 1), F32)
    if n_sub < WIN_TILES:
        lp_ref[0, :, n_sub * TP:] = jnp.zeros((SUBLANES, (WIN_TILES - n_sub) * TP), I32)
        lp_vmem[:, n_sub * TP:] = jnp.zeros((SUBLANES, (WIN_TILES - n_sub) * TP), I32)
    for s in range(n_sub):
        cls = cls_scr[s, 0:1, :]
        onehot.append(jnp.where(rows == cls, 1.0, 0.0).astype(F32))
        counts.append(jnp.sum(onehot[s], axis=1, keepdims=True))
        lstart = lstart + jnp.sum(jnp.where(cls < rows, 1.0, 0.0).astype(F32), axis=1, keepdims=True)
    base = lstart
    for s in range(n_sub):
        pre = jnp.dot(onehot[s].astype(BF16), tri_ref[...], preferred_element_type=F32)
        lpos = jnp.sum(onehot[s] * (pre - 1.0 + base), axis=0, keepdims=True).astype(I32)
        lpos = lpos * D_ROWS
        lp_ref[0, :, s * TP:(s + 1) * TP] = jnp.broadcast_to(lpos, (SUBLANES, TP))
        lp_vmem[:, s * TP:(s + 1) * TP] = jnp.broadcast_to(lpos, (SUBLANES, TP))
        base = base + counts[s]
    tot = base - lstart
    carry = carry_ref[:, 0:1]
    lane = lax.broadcasted_iota(I32, (CLS_ROWS, LANES), 1)
    tab_ref[0] = jnp.where(lane == 0, carry, jnp.where(lane == 1, tot, jnp.where(lane == 2, lstart, 0.0)))
    carry_ref[...] = jnp.broadcast_to(carry + tot, (CLS_ROWS, LANES))
    pltpu.sync_copy(lp_vmem.at[pl.ds(0, 1)], lp_smem)

    def window_copy(win, n_tiles):
        rows = n_tiles * TP * D_ROWS
        start = win * (WIN * D_ROWS)
        start = start if isinstance(start, int) else pl.multiple_of(start, D_ROWS)
        return pltpu.make_async_copy(h1_buf.at[pl.ds(0, rows), :], h1_hbm.at[pl.ds(start, rows), :], h1_sem)

    @pl.when(w > 0)
    def _():
        window_copy(0, WIN_TILES).wait()

    for s in range(n_sub):
        def group(g, carry_, s=s):
            for k in range(SUBLANES):
                dst = pl.multiple_of(lp_smem[0, s * TP + g * SUBLANES + k], D_ROWS)
                h1_buf[pl.ds(dst, D_ROWS), :] = stage_h[s, g, pl.ds(k, D_ROWS, stride=SUBLANES), :]
            return carry_

        lax.fori_loop(0, TP // SUBLANES, group, 0, unroll=4)

    window_copy(w, n_sub).start()

    @pl.when(is_last_step)
    def _():
        window_copy(w, n_sub).wait()


def _mixer_front(x, g_mix_ref, w_in_ref):
    a = _rms(x, g_mix_ref[...]).astype(BF16)
    z = jnp.dot(a, w_in_ref[...], preferred_element_type=F32)
    glu = z[:, 0:C_CONV] * jax.nn.sigmoid(z[:, C_CONV:2 * C_CONV])
    u = jax.nn.gelu(z[:, 2 * C_CONV:2 * C_CONV + C_GMLP])
    gv = jax.nn.gelu(z[:, 2 * C_CONV + C_GMLP:])
    return glu, u, gv


def _mixer_body(xp_ref, xs_ref, hist_ref, g_mix_ref, w_in_ref, cw_ref, cb_ref, clg_ref, clb_ref, glg_ref, glb_ref,
                ws_ref, bs_ref, wsl_ref, bsl_ref, w_out_ref, g_ffn_ref, w_rt_ref, b_r_ref, tri_ref,
                eg_ref, eu_ref, ed_ref,
                h1_hbm, lp_ref, tab_ref, cstp_ref, csts_ref, vs_ref, egb_ref, eub_ref, edb_ref,
                glu_scr, xs_scr, cat_scr, carry_ref, stage_h, cls_scr, lp_vmem, lp_smem, h1_buf, h1_sem,
                *, n_prompt_tiles, nj, dec_seq, sblk, n_tail_tiles):
    i = pl.program_id(0)
    sub = lax.rem(i, WIN_TILES)

    egb_ref[...] = eg_ref[...].astype(BF16)
    eub_ref[...] = eu_ref[...].astype(BF16)
    edb_ref[...] = ed_ref[...].astype(BF16)

    @pl.when(i == 0)
    def _():
        carry_ref[...] = jnp.zeros_like(carry_ref)

    def tail(x):
        h = x + jnp.dot(cat_scr[...], w_out_ref[...], preferred_element_type=F32)
        _route_tile(h, g_ffn_ref, w_rt_ref, b_r_ref, stage_h, cls_scr, sub)

    @pl.when(i < n_prompt_tiles)
    def _prompt():
        j = lax.rem(i, nj)
        x = xp_ref[0]
        glu, u, gv = _mixer_front(x, g_mix_ref, w_in_ref)

        def time_rows(t0, n):
            return pl.ds(CONV_PITCH * t0, n, stride=CONV_PITCH)

        @pl.when(j == 0)
        def _():
            for s in range(C_CONV // LANES):
                glu_scr[s, time_rows(0, HIST), :] = jnp.zeros((HIST, LANES), F32)

        @pl.when(j > 0)
        def _():
            for s in range(C_CONV // LANES):
                glu_scr[s, time_rows(0, HIST), :] = glu_scr[s, time_rows(TP, HIST), :]

        for s in range(C_CONV // LANES):
            glu_scr[s, time_rows(HIST, TP), :] = glu[:, s * LANES:(s + 1) * LANES]
        cstp_ref[0] = jnp.concatenate([glu_scr[s, time_rows(TP, HIST), :] for s in range(C_CONV // LANES)], axis=1)

        cb = cb_ref[...]
        for r0 in range(0, TP, CONV_ROWS):
            slabs = []
            for s in range(C_CONV // LANES):
                acc = None
                for k in range(CONV_WIDTH):
                    term = (cw_ref[k:k + 1, s * LANES:(s + 1) * LANES]
                            * glu_scr[s, time_rows(r0 + HIST_OFF + k, CONV_ROWS), :])
                    acc = term if acc is None else acc + term
                slabs.append(acc)
            ya = jax.nn.silu(_ln(jnp.concatenate(slabs, axis=1) + cb, clg_ref[...], clb_ref[...]))
            cat_scr[r0:r0 + CONV_ROWS, 0:C_CONV] = ya.astype(BF16)

        v = _ln(gv, glg_ref[...], glb_ref[...])
        vb = v.astype(BF16)
        lane = lax.broadcasted_iota(I32, (CHUNK, LANES), 1)
        is_lo = lane < HEAD_DIM
        zero = jnp.zeros((CHUNK, LANES), BF16)
        for c in range(TP // CHUNK):
            for p in range(N_HEADS // 2):
                blk = vb[c * CHUNK:(c + 1) * CHUNK, p * LANES:(p + 1) * LANES]
                rhs = jnp.concatenate([jnp.where(is_lo, blk, zero), jnp.where(is_lo, zero, blk)], axis=0)
                mixed = jnp.dot(ws_ref[p], rhs, preferred_element_type=F32) + bs_ref[:, p * LANES:(p + 1) * LANES]
                yb = u[c * CHUNK:(c + 1) * CHUNK, p * LANES:(p + 1) * LANES] * mixed
                cat_scr[c * CHUNK:(c + 1) * CHUNK, C_CONV + p * LANES:C_CONV + (p + 1) * LANES] = yb.astype(BF16)
        tail(x)

    @pl.when(i >= n_prompt_tiles)
    def _sample():
        x_nat = xs_ref[...].reshape(TP, D_MODEL)
        slabs = []
        for s in range(D_ROWS):
            half, row0 = s // 2, (s % 2) * TP
            glu_scr[half, row0:row0 + TP, :] = x_nat[:, s * LANES:(s + 1) * LANES]
            slabs.append(jnp.concatenate(
                [glu_scr[half, pl.ds(row0 + t, sblk, stride=dec_seq), :] for t in range(dec_seq)], axis=0))
        x = jnp.concatenate(slabs, axis=1)
        glu, u, gv = _mixer_front(x, g_mix_ref, w_in_ref)

        nh = CONV_WIDTH - 1
        xs_scr[0:nh] = hist_ref[...]
        xs_scr[nh:nh + dec_seq] = glu.reshape(dec_seq, sblk, C_CONV)
        csts_ref[...] = xs_scr[dec_seq:dec_seq + nh]

        cb = cb_ref[...]
        for t in range(dec_seq):
            acc = jnp.zeros((sblk, C_CONV), F32)
            for k in range(CONV_WIDTH):
                acc = acc + cw_ref[k:k + 1, :] * xs_scr[t + k]
            ya = jax.nn.silu(_ln(acc + cb, clg_ref[...], clb_ref[...]))
            cat_scr[t * sblk:(t + 1) * sblk, 0:C_CONV] = ya.astype(BF16)

        v = _ln(gv, glg_ref[...], glb_ref[...])
        vs_ref[...] = v.reshape(dec_seq, sblk, C_GMLP)
        for t in range(dec_seq):
            mixed = jnp.broadcast_to(bsl_ref[t:t + 1, :], (sblk, C_GMLP))
            for tp in range(t + 1):
                r = t * dec_seq + tp
                mixed = mixed + wsl_ref[r:r + 1, :] * v[tp * sblk:(tp + 1) * sblk, :]
            yb = u[t * sblk:(t + 1) * sblk, :] * mixed
            cat_scr[t * sblk:(t + 1) * sblk, C_CONV:] = yb.astype(BF16)
        tail(x)

    is_last_step = i == pl.num_programs(0) - 1
    sort_args = (tri_ref, carry_ref, h1_hbm, h1_buf, h1_sem, lp_ref, tab_ref, stage_h, cls_scr, lp_vmem, lp_smem,
                 i // WIN_TILES, is_last_step)

    @pl.when(sub == WIN_TILES - 1)
    def _():
        _sort_window(*sort_args, WIN_TILES)

    if n_tail_tiles:
        @pl.when(is_last_step)
        def _():
            _sort_window(*sort_args, n_tail_tiles)


def _for_pieces(ln, fn):
    def emit(sizes):
        for size in sizes:
            shift = size.bit_length()
            off = lax.shift_left(lax.shift_right_logical(ln, shift), shift)

            @pl.when((ln & size) != 0)
            def _(off=off, size=size):
                fn(off, size)

    large = tuple(s for s in PIECES if s >= LARGE_PIECE)

    @pl.when(ln >= LARGE_PIECE)
    def _():
        emit(large)

    emit(tuple(s for s in PIECES if s < LARGE_PIECE))


def _for_runs(k, tcls_ref, r0_ref, nval_ref, jlo_ref, jhi_ref, rank0_ref, cnt_ref, lsrc_ref, fn):
    c = tcls_ref[k]
    ra = r0_ref[k]
    rb = ra + nval_ref[k]

    def body(j, carry):
        idx = j * N_CLASSES + c
        s = rank0_ref[idx]
        lo = jnp.maximum(s, ra)
        hi = jnp.minimum(s + cnt_ref[idx], rb)
        ln = hi - lo

        @pl.when(ln > 0)
        def _():
            src = lsrc_ref[idx] + (lo - s)
            dst = lo - ra
            _for_pieces(ln, lambda off, size: fn(src + off, dst + off, size))

        return carry

    lax.fori_loop(jlo_ref[k], jhi_ref[k], body, 0)


def _blocks(ref, first, n):
    start = first * D_ROWS if isinstance(first, int) else pl.multiple_of(first * D_ROWS, D_ROWS)
    return ref.at[pl.ds(start, n * D_ROWS), :]


def _moe_body(grp_ref, lo_ref, hi_ref, nval_ref, tcls_ref, r0_ref, jlo_ref, jhi_ref, rank0_ref, cnt_ref, lsrc_ref,
              h1_hbm, g_ffn_ref, w_r_ref, b_r_ref, wg_ref, wu_ref, wd_ref,
              h2_hbm, hbuf, obuf, gsem, ssem):
    k = pl.program_id(0)
    n = pl.num_programs(0)
    slot = lax.rem(k, 2)
    tables = (tcls_ref, r0_ref, nval_ref, jlo_ref, jhi_ref, rank0_ref, cnt_ref, lsrc_ref)

    def gather_tile(tile, to_slot):
        def copy(src, dst, size):
            pltpu.make_async_copy(_blocks(h1_hbm, src, size), _blocks(hbuf.at[to_slot], dst, size),
                                  gsem.at[to_slot]).start()
        _for_runs(tile, *tables, copy)

    def wait_rows(n_rows, make_copy):
        _for_pieces(n_rows, lambda off, size: make_copy(size).wait())

    @pl.when(k == 0)
    def _():
        hbuf[...] = jnp.zeros_like(hbuf)
        gather_tile(0, 0)

    @pl.when(jnp.logical_and(k >= 2, nval_ref[jnp.maximum(k - 2, 0)] > 0))
    def _():
        wait_rows(nval_ref[jnp.maximum(k - 2, 0)],
                  lambda size: pltpu.make_async_copy(_blocks(obuf.at[slot], 0, size), _blocks(h2_hbm, 0, size),
                                                     ssem.at[slot]))

    nxt = jnp.minimum(k + 1, n - 1)

    @pl.when(jnp.logical_and(k + 1 < n, nval_ref[nxt] > 0))
    def _():
        gather_tile(nxt, 1 - slot)

    @pl.when(nval_ref[k] > 0)
    def _():
        wait_rows(nval_ref[k],
                  lambda size: pltpu.make_async_copy(_blocks(h1_hbm, 0, size), _blocks(hbuf.at[slot], 0, size),
                                                     gsem.at[slot]))
        e_lo = lo_ref[k]
        e_hi = hi_ref[k]

        def experts(m):
            h = _load_token_blocks(hbuf.at[slot], m)
            c = _rms(h, g_ffn_ref[...]).astype(BF16)
            logits = jnp.dot(c, w_r_ref[...], preferred_element_type=F32) + b_r_ref[...]
            wl, wh = _combine_weights(logits, grp_ref[k], e_lo, e_hi)

            def hidden(e):
                gate = jnp.dot(c, wg_ref[e], preferred_element_type=F32)
                return (jax.nn.silu(gate) * jnp.dot(c, wu_ref[e], preferred_element_type=F32)).astype(BF16)

            yl = jnp.dot(hidden(e_lo), wd_ref[e_lo], preferred_element_type=F32)
            yh = jnp.dot(hidden(e_hi), wd_ref[e_hi], preferred_element_type=F32)
            _store_token_blocks(obuf.at[slot], h + (wl * yl + wh * yh), m)

        @pl.when(nval_ref[k] > TM // 2)
        def _():
            experts(TM)

        @pl.when(nval_ref[k] <= TM // 2)
        def _():
            experts(TM // 2)

        def copy_back(src, dst, size):
            pltpu.make_async_copy(_blocks(obuf.at[slot], dst, size), _blocks(h2_hbm, src, size),
                                  ssem.at[slot]).start()
        _for_runs(k, *tables, copy_back)


def _ple_body(lp_ref, lpn_ref, h2_hbm, p_ref, g_ple_ref, w_gate_ref, w_proj_ref, g_fin_ref, y_ref,
              stage_h, win_buf, win_sem, *rest, tile0, win_tiles, seq_rows):
    ng = TP // SUBLANES
    i = pl.program_id(0)
    n = pl.num_programs(0)
    sub = lax.rem(i, win_tiles)
    w_local = i // win_tiles
    wslot = lax.rem(w_local, 2)
    hslot = lax.rem(i, 2)

    def window_copy(w, to_slot):
        rows = win_tiles * TP * D_ROWS
        start = pl.multiple_of((tile0 // WIN_TILES + w) * (WIN * D_ROWS), D_ROWS)
        return pltpu.make_async_copy(h2_hbm.at[pl.ds(start, rows), :], win_buf.at[to_slot, pl.ds(0, rows), :],
                                     win_sem.at[to_slot])

    def unsort_block(lp, tile, to_slot, g, k):
        src = pl.multiple_of(lp[0, 0, lax.rem(tile, win_tiles) * TP + g * SUBLANES + k], D_ROWS)
        stage_h[to_slot, g, pl.ds(k, D_ROWS, stride=SUBLANES), :] = win_buf[
            lax.rem(tile // win_tiles, 2), pl.ds(src, D_ROWS), :]

    @pl.when(i == 0)
    def _():
        window_copy(0, 0).start()
        window_copy(0, 0).wait()

        def group(g, carry):
            for k in range(SUBLANES):
                unsort_block(lp_ref, 0, 0, g, k)
            return carry

        lax.fori_loop(0, ng, group, 0, unroll=4)

    @pl.when(jnp.logical_and(sub == 0, i + win_tiles < n))
    def _():
        window_copy(w_local + 1, 1 - wslot).start()

    @pl.when(jnp.logical_and(sub == win_tiles - 1, i + 1 < n))
    def _():
        window_copy(w_local + 1, 1 - wslot).wait()

    h = jnp.concatenate([stage_h[hslot, :, s * SUBLANES:(s + 1) * SUBLANES, :].reshape(TP, LANES)
                         for s in range(D_ROWS)], axis=1)
    c = _rms(h, g_ple_ref[...]).astype(BF16)
    nxt = jnp.minimum(i + 1, n - 1)
    for g in range(ng):
        for k in range(SUBLANES):
            unsort_block(lpn_ref, nxt, 1 - hslot, g, k)
    gate = jax.nn.sigmoid(jnp.dot(c, w_gate_ref[...], preferred_element_type=F32))
    proj = jnp.dot(p_ref[...].astype(BF16), w_proj_ref[...], preferred_element_type=F32)
    y = _rms(h + proj * gate, g_fin_ref[...])
    if seq_rows is None:
        y_ref[...] = y
    else:
        sblk, dec_seq = seq_rows
        out_scr = rest[0]
        for s in range(D_ROWS):
            for t in range(dec_seq):
                out_scr[s, pl.ds(t, sblk, stride=dec_seq), :] = y[t * sblk:(t + 1) * sblk, s * LANES:(s + 1) * LANES]
            y_ref[:, s * LANES:(s + 1) * LANES] = out_scr[s]


def _full(shape, single=False):
    nd = len(shape)
    if single:
        return pl.BlockSpec(shape, lambda *_: (0,) * nd, pipeline_mode=pl.Buffered(1))
    return pl.BlockSpec(shape, lambda *_: (0,) * nd)


def _mixer(x_prompt, x_sample, hist_tm, weights, tri, expert_w):
    batch, seq, _ = x_prompt.shape
    dec_batch, dec_seq, _ = x_sample.shape
    nj = seq // TP
    npt = batch * nj
    sblk = TP // dec_seq
    nst = dec_batch // sblk
    n_tiles = npt + nst
    n_tail = n_tiles % WIN_TILES
    assert npt % WIN_TILES == 0 and n_tail <= nst
    n_windows = pl.cdiv(n_tiles, WIN_TILES)
    nh = CONV_WIDTH - 1
    w_specs = [_full(w.shape, single=True) for w in weights]
    body = functools.partial(_mixer_body, n_prompt_tiles=npt, nj=nj, dec_seq=dec_seq, sblk=sblk, n_tail_tiles=n_tail)

    def p_idx(i):
        return jnp.minimum(i, npt - 1)

    def s_idx(i):
        return jnp.maximum(i - npt, 0)

    assert npt % N_EXPERTS == 0
    parts = npt // N_EXPERTS
    e_specs = [pl.BlockSpec((1, w.shape[1] // parts, w.shape[2]), lambda i: (p_idx(i) // parts, p_idx(i) % parts, 0))
               for w in expert_w]
    e_shapes = [jax.ShapeDtypeStruct(w.shape, BF16) for w in expert_w]

    return pl.pallas_call(
        body,
        grid=(n_tiles,),
        in_specs=[pl.BlockSpec((1, TP, D_MODEL), lambda i: (p_idx(i) // nj, p_idx(i) % nj, 0)),
                  pl.BlockSpec((sblk, dec_seq, D_MODEL), lambda i: (s_idx(i), 0, 0), pipeline_mode=pl.Buffered(1)),
                  pl.BlockSpec((nh, sblk, C_CONV), lambda i: (0, s_idx(i), 0), pipeline_mode=pl.Buffered(1))]
        + w_specs + [_full(tri.shape, single=True)] + e_specs,
        out_specs=[
            pl.BlockSpec(memory_space=pl.ANY),
            pl.BlockSpec((1, SUBLANES, WIN), lambda i: (i // WIN_TILES, 0, 0)),
            pl.BlockSpec((1, CLS_ROWS, LANES), lambda i: (i // WIN_TILES, 0, 0)),
            pl.BlockSpec((1, HIST, C_CONV), lambda i: (p_idx(i) // nj, 0, 0)),
            pl.BlockSpec((nh, sblk, C_CONV), lambda i: (0, s_idx(i), 0), pipeline_mode=pl.Buffered(1)),
            pl.BlockSpec((dec_seq, sblk, C_GMLP), lambda i: (0, s_idx(i), 0), pipeline_mode=pl.Buffered(1)),
        ] + e_specs,
        out_shape=[
            jax.ShapeDtypeStruct((n_tiles * TP * D_ROWS, LANES), F32),
            jax.ShapeDtypeStruct((n_windows, SUBLANES, WIN), I32),
            jax.ShapeDtypeStruct((n_windows, CLS_ROWS, LANES), F32),
            jax.ShapeDtypeStruct((batch, HIST, C_CONV), F32),
            jax.ShapeDtypeStruct((nh, dec_batch, C_CONV), F32),
            jax.ShapeDtypeStruct((dec_seq, dec_batch, C_GMLP), F32),
        ] + e_shapes,
        scratch_shapes=[
            pltpu.VMEM((C_CONV // LANES, CONV_PITCH * (HIST + TP), LANES), F32),
            pltpu.VMEM((nh + dec_seq, sblk, C_CONV), F32),
            pltpu.VMEM((TP, D_MODEL), BF16),
            pltpu.VMEM((CLS_ROWS, LANES), F32),
            pltpu.VMEM((WIN_TILES, TP // SUBLANES, GROUP_ROWS, LANES), F32),
            pltpu.VMEM((WIN_TILES, SUBLANES, TP), I32),
            pltpu.VMEM((SUBLANES, WIN), I32),
            pltpu.SMEM((1, WIN), I32),
            pltpu.VMEM((WIN * D_ROWS, LANES), F32),
            pltpu.SemaphoreType.DMA(()),
        ],
        compiler_params=pltpu.CompilerParams(
            dimension_semantics=("arbitrary",), vmem_limit_bytes=VMEM_LIMIT),
        name="mixer",
    )(x_prompt, x_sample, hist_tm, *weights, tri, *expert_w)


def _moe(tile_tabs, run_tabs, h1, g_ffn, w_r, b_r, w_g, w_u, w_d):
    n_tiles = tile_tabs[0].shape[0]
    n_prefetch = len(tile_tabs) + len(run_tabs)
    grid_spec = pltpu.PrefetchScalarGridSpec(
        num_scalar_prefetch=n_prefetch,
        grid=(n_tiles,),
        in_specs=[
            pl.BlockSpec(memory_space=pl.ANY),
            pl.BlockSpec((1, D_MODEL), lambda i, *_: (0, 0)),
            pl.BlockSpec((D_MODEL, LANES), lambda i, *_: (0, 0)),
            pl.BlockSpec((1, LANES), lambda i, *_: (0, 0)),
            pl.BlockSpec((EPG, D_MODEL, D_EXPERT), lambda i, grp, *_: (grp[i], 0, 0)),
            pl.BlockSpec((EPG, D_MODEL, D_EXPERT), lambda i, grp, *_: (grp[i], 0, 0)),
            pl.BlockSpec((EPG, D_EXPERT, D_MODEL), lambda i, grp, *_: (grp[i], 0, 0)),
        ],
        out_specs=pl.BlockSpec(memory_space=pl.ANY),
        scratch_shapes=[pltpu.VMEM((2, TM * D_ROWS, LANES), F32),
                        pltpu.VMEM((2, TM * D_ROWS, LANES), F32),
                        pltpu.SemaphoreType.DMA((2,)),
                        pltpu.SemaphoreType.DMA((2,))],
    )
    return pl.pallas_call(
        _moe_body,
        grid_spec=grid_spec,
        out_shape=jax.ShapeDtypeStruct(h1.shape, F32),
        compiler_params=pltpu.CompilerParams(
            dimension_semantics=("arbitrary",), vmem_limit_bytes=VMEM_LIMIT),
        name="moe",
    )(*tile_tabs, *run_tabs, h1, g_ffn, w_r, b_r, w_g, w_u, w_d)


def _ple(lp, h2, tile0, p, g_ple, w_gate, w_proj, g_fin, name, seq_rows=None):
    n_tiles = p.shape[0] // TP
    win_tiles = min(WIN_TILES, n_tiles)
    assert tile0 % WIN_TILES == 0 and n_tiles % win_tiles == 0
    extra = [] if seq_rows is None else [pltpu.VMEM((D_ROWS, TP, LANES), F32)]
    return pl.pallas_call(
        functools.partial(_ple_body, tile0=tile0, win_tiles=win_tiles, seq_rows=seq_rows),
        grid=(n_tiles,),
        in_specs=[
            pl.BlockSpec((1, 1, WIN), lambda i: ((tile0 + i) // WIN_TILES, 0, 0), memory_space=pltpu.SMEM),
            pl.BlockSpec((1, 1, WIN), lambda i: ((tile0 + jnp.minimum(i + 1, n_tiles - 1)) // WIN_TILES, 0, 0),
                         memory_space=pltpu.SMEM),
            pl.BlockSpec(memory_space=pl.ANY),
            pl.BlockSpec((TP, PLE_DIM), lambda i: (i, 0)),
            _full(g_ple.shape), _full(w_gate.shape), _full(w_proj.shape), _full(g_fin.shape),
        ],
        out_specs=pl.BlockSpec((TP, D_MODEL), lambda i: (i, 0)),
        out_shape=jax.ShapeDtypeStruct((n_tiles * TP, D_MODEL), F32),
        scratch_shapes=[pltpu.VMEM((2, TP // SUBLANES, GROUP_ROWS, LANES), F32),
                        pltpu.VMEM((2, win_tiles * TP * D_ROWS, LANES), F32),
                        pltpu.SemaphoreType.DMA((2,))] + extra,
        compiler_params=pltpu.CompilerParams(
            dimension_semantics=("arbitrary",), vmem_limit_bytes=VMEM_LIMIT),
        name=name,
    )(lp, lp, h2, p, g_ple, w_gate, w_proj, g_fin)


def kernel(x_prompt, x_sample, p_prompt, p_sample, state_conv, norm_mix_g, w_in, conv_w, conv_b, conv_ln_g, conv_ln_b, gmlp_ln_g, gmlp_ln_b, w_s, b_s, w_out, norm_ffn_g, w_router_group, b_router_group, w_router_expert, b_router_expert, w_exp_gate, w_exp_up, w_exp_down, norm_ple_g, w_ple_gate, w_ple_proj, norm_final_g):
    depth = w_in.shape[0]
    assert depth == 1, "single-layer pipeline"
    batch, seq, _ = x_prompt.shape
    dec_batch, dec_seq, _ = x_sample.shape
    assert seq % TP == 0 and TP % CHUNK == 0 and TP % dec_seq == 0 and dec_batch % (TP // dec_seq) == 0
    sblk = TP // dec_seq
    n_prompt = batch * seq
    n_sample = dec_batch * dec_seq
    n_tok = n_prompt + n_sample

    row = lambda a: a.reshape(1, -1)
    w_in_b = w_in[0].astype(BF16)
    w_out_b = w_out[0].astype(BF16)
    cw = jnp.concatenate([conv_w[0], jnp.zeros((1, C_CONV), F32)], axis=0)
    tril = jnp.asarray(np.tril(np.ones((CHUNK, CHUNK), bool)))
    ws_m = jnp.where(tril[None], w_s[0], 0.0)
    ws_cat = jnp.concatenate([ws_m[0::2], ws_m[1::2]], axis=2).astype(BF16)
    bs_lane = jnp.repeat(jnp.transpose(b_s[0]), HEAD_DIM, axis=1)
    w_rt = jnp.zeros((CLS_ROWS, D_MODEL), F32)
    w_rt = w_rt.at[0:N_GROUPS].set(jnp.transpose(w_router_group[0]))
    w_rt = w_rt.at[N_GROUPS:N_GROUPS + N_EXPERTS].set(jnp.transpose(w_router_expert[0])).astype(BF16)
    b_r = jnp.zeros((CLS_ROWS, 1), F32)
    b_r = b_r.at[0:N_GROUPS, 0].set(b_router_group[0]).at[N_GROUPS:N_GROUPS + N_EXPERTS, 0].set(b_router_expert[0])
    n_logit = N_GROUPS + N_EXPERTS
    w_rn = jnp.concatenate([w_router_group[0], w_router_expert[0], jnp.zeros((D_MODEL, LANES - n_logit), F32)],
                           axis=1).astype(BF16)
    b_rn = jnp.concatenate([b_router_group[0], b_router_expert[0], jnp.zeros((LANES - n_logit,), F32)]).reshape(1, LANES)
    tri = jnp.asarray(np.triu(np.ones((TP, TP), np.float32)), dtype=BF16)
    wsl = jnp.where(jnp.asarray(np.tril(np.ones((dec_seq, dec_seq), bool)))[None], w_s[0][:, :dec_seq, :dec_seq], 0.0)
    wsl = jnp.repeat(jnp.transpose(wsl, (1, 2, 0)).reshape(dec_seq * dec_seq, N_HEADS), HEAD_DIM, axis=1)
    bsl = jnp.repeat(jnp.transpose(b_s[0][:, :dec_seq]), HEAD_DIM, axis=1)

    weights = (row(norm_mix_g[0]), w_in_b, cw, row(conv_b[0]), row(conv_ln_g[0]), row(conv_ln_b[0]),
               row(gmlp_ln_g[0]), row(gmlp_ln_b[0]), ws_cat, bs_lane, wsl, bsl,
               w_out_b, row(norm_ffn_g[0]), w_rt, b_r)

    hist_tm = jnp.transpose(state_conv[0], (1, 0, 2))
    h1, lp, tab, cst_p, cst_s, v_s, w_g, w_u, w_d = _mixer(
        x_prompt, x_sample, hist_tm, weights, tri, (w_exp_gate[0], w_exp_up[0], w_exp_down[0]))

    rank0 = tab[:, :N_CLASSES, 0].astype(I32)
    cnt = tab[:, :N_CLASSES, 1].astype(I32)
    lsrc = tab[:, :N_CLASSES, 2].astype(I32) + (jnp.arange(tab.shape[0], dtype=I32) * WIN)[:, None]
    total = rank0[-1] + cnt[-1]
    padded = ((total + TM - 1) // TM) * TM
    ids = jnp.arange(N_CLASSES, dtype=I32)
    ends = jnp.sum(jnp.where(ids[None, :] <= ids[:, None], padded[None, :], 0), axis=1)
    offs = ends - padded
    all_rows = jnp.sum(padded)
    n_tiles = (n_tok + N_CLASSES * (TM - 1)) // TM + 2
    tile_start = jnp.arange(n_tiles, dtype=I32) * TM
    used = tile_start < all_rows
    last_cls = jnp.max(jnp.where(padded > 0, ids, 0))
    tile_cls = jnp.sum((tile_start[:, None] >= ends[None, :]).astype(I32), axis=1)
    tile_cls = jnp.where(used, tile_cls, last_cls)
    of_tile = tile_cls[:, None] == ids[None, :]
    tile_r0 = tile_start - jnp.sum(jnp.where(of_tile, offs[None, :], 0), axis=1)
    tile_total = jnp.sum(jnp.where(of_tile, total[None, :], 0), axis=1)
    tile_nval = jnp.where(used, jnp.clip(tile_total - tile_r0, 0, TM), 0).astype(I32)
    pair = tile_cls % 6
    assert _PAIRS == ((0, 1), (0, 2), (0, 3), (1, 2), (1, 3), (2, 3))
    tile_lo = jnp.where(pair < 3, 0, jnp.where(pair < 5, 1, 2)).astype(I32)
    tile_hi = jnp.where(pair == 0, 1, jnp.where(jnp.logical_or(pair == 1, pair == 3), 2, 3)).astype(I32)
    run_beg = jnp.sum(jnp.where(of_tile[None], rank0[:, None, :], 0), axis=2)
    run_end = run_beg + jnp.sum(jnp.where(of_tile[None], cnt[:, None, :], 0), axis=2)
    tile_jlo = jnp.sum((run_end <= tile_r0[None, :]).astype(I32), axis=0)
    tile_jhi = jnp.sum((run_beg < (tile_r0 + tile_nval)[None, :]).astype(I32), axis=0)
    tile_jhi = jnp.maximum(tile_jhi, tile_jlo)

    h2 = _moe((tile_cls // 6, tile_lo, tile_hi, tile_nval, tile_cls, tile_r0, tile_jlo, tile_jhi),
              (rank0.reshape(-1), cnt.reshape(-1), lsrc.reshape(-1)),
              h1, row(norm_ffn_g[0]), w_rn, b_rn, w_g, w_u, w_d)

    lp = lp[:, 0:1, :]
    ple_w = (row(norm_ple_g[0]), w_ple_gate[0].astype(BF16), w_ple_proj[0].astype(BF16), row(norm_final_g))
    y_p = _ple(lp, h2, 0, p_prompt[0].reshape(n_prompt, PLE_DIM), *ple_w, name="ple_prompt")
    p_s_tm = jnp.transpose(p_sample[0].reshape(dec_batch // sblk, sblk, dec_seq, PLE_DIM), (0, 2, 1, 3))
    y_s = _ple(lp, h2, n_prompt // TP, p_s_tm.reshape(n_sample, PLE_DIM), *ple_w, name="ple_sample",
               seq_rows=(sblk, dec_seq))

    y_prompt = y_p.reshape(batch, seq, D_MODEL)
    y_sample = y_s.reshape(dec_batch, dec_seq, D_MODEL)
    state_conv_prompt = cst_p[:, HIST_OFF:, :][None]
    state_conv_sample = jnp.transpose(cst_s, (1, 0, 2))[None]
    state_gmlp_v_sample = jnp.transpose(v_s, (1, 0, 2))[None]
    return (y_prompt, y_sample, state_conv_prompt, state_conv_sample, state_gmlp_v_sample)
```

```python
import functools

import numpy as np
import jax
import jax.numpy as jnp
from jax import lax
from jax.experimental import pallas as pl
from jax.experimental.pallas import tpu as pltpu

F32 = jnp.float32
BF16 = jnp.bfloat16
I32 = jnp.int32

D_MODEL = 1024
C_CONV = 512
C_GMLP = 512
N_HEADS = 8
HEAD_DIM = 64
CONV_WIDTH = 31
CHUNK = 128
PLE_DIM = 256
N_GROUPS = 4
EPG = 4
N_EXPERTS = 16
D_EXPERT = 512
EPS = 1e-6
LANES = 128
SUBLANES = 8

N_CLASSES = N_GROUPS * 6
CLS_ROWS = 32
HIST = 32
HIST_OFF = HIST - (CONV_WIDTH - 1)
D_ROWS = D_MODEL // LANES
assert D_ROWS == SUBLANES
GROUP_ROWS = D_ROWS * SUBLANES

TP = 512
WIN_TILES = 4
WIN = WIN_TILES * TP
TM = 256
CONV_ROWS = 64
CONV_PITCH = 2
PIECES = tuple(TM >> b for b in range(TM.bit_length()))
LARGE_PIECE = 32

VMEM_LIMIT = 58 * 1024 * 1024

_PAIRS = ((0, 1), (0, 2), (0, 3), (1, 2), (1, 3), (2, 3))


def _rms(x, g):
    ms = jnp.mean(x * x, axis=-1, keepdims=True)
    return x * lax.rsqrt(ms + EPS) * g


def _ln(x, g, b):
    mu = jnp.mean(x, axis=-1, keepdims=True)
    xc = x - mu
    var = jnp.mean(xc * xc, axis=-1, keepdims=True)
    return xc * lax.rsqrt(var + EPS) * g + b


def _max4(v):
    return jnp.maximum(jnp.maximum(v[0], v[1]), jnp.maximum(v[2], v[3]))


def _first4(v, m):
    return jnp.where(v[0] == m, 0, jnp.where(v[1] == m, 1, jnp.where(v[2] == m, 2, 3))).astype(I32)


def _route(lt):
    gl = [lt[i:i + 1, :] for i in range(N_GROUPS)]
    g = _first4(gl, _max4(gl))
    a = []
    for j in range(EPG):
        rows = [lt[N_GROUPS + EPG * q + j:N_GROUPS + EPG * q + j + 1, :] for q in range(N_GROUPS)]
        a.append(jnp.where(g == 0, rows[0], jnp.where(g == 1, rows[1], jnp.where(g == 2, rows[2], rows[3]))))
    v1 = _max4(a)
    i1 = _first4(a, v1)
    a2 = [jnp.where(i1 == j, -jnp.inf, a[j]) for j in range(EPG)]
    i2 = _first4(a2, _max4(a2))
    lo = jnp.minimum(i1, i2)
    hi = jnp.maximum(i1, i2)
    pair = jnp.where(lo == 0, hi - 1, jnp.where(lo == 1, hi + 1, 5))
    return g * 6 + pair


def _combine_weights(logits, grp, e_lo, e_hi):
    lane = lax.broadcasted_iota(I32, logits.shape, 1)
    is_group = lane < N_GROUPS
    m = jnp.max(jnp.where(is_group, logits, -jnp.inf), axis=1, keepdims=True)
    den = jnp.sum(jnp.where(is_group, jnp.exp(logits - m), 0.0), axis=1, keepdims=True)
    base = N_GROUPS + EPG * grp
    v_lo = jnp.sum(jnp.where(lane == base + e_lo, logits, 0.0), axis=1, keepdims=True)
    v_hi = jnp.sum(jnp.where(lane == base + e_hi, logits, 0.0), axis=1, keepdims=True)
    top = jnp.maximum(v_lo, v_hi)
    p_lo = jnp.exp(v_lo - top)
    p_hi = jnp.exp(v_hi - top)
    scale = 1.0 / (den * (p_lo + p_hi))
    return p_lo * scale, p_hi * scale


def _load_token_blocks(ref, n_tok):
    return jnp.concatenate([ref[pl.ds(s, n_tok, stride=D_ROWS), :] for s in range(D_ROWS)], axis=1)


def _store_token_blocks(ref, x, n_tok):
    for s in range(D_ROWS):
        ref[pl.ds(s, n_tok, stride=D_ROWS), :] = x[:, s * LANES:(s + 1) * LANES]


def _route_tile(h, g_ffn_ref, w_rt_ref, b_r_ref, stage_h, cls_scr, sub):
    c = _rms(h, g_ffn_ref[...]).astype(BF16)
    lt = lax.dot_general(w_rt_ref[...], c, (((1,), (1,)), ((), ())), preferred_element_type=F32)
    cls_scr[sub] = jnp.broadcast_to(_route(lt + b_r_ref[...]), (SUBLANES, TP))
    ng = TP // SUBLANES
    for s in range(D_ROWS):
        stage_h[sub, :, s * SUBLANES:(s + 1) * SUBLANES, :] = h[:, s * LANES:(s + 1) * LANES].reshape(
            ng, SUBLANES, LANES)


def _sort_window(tri_ref, carry_ref, h1_hbm, h1_buf, h1_sem, lp_ref, tab_ref, stage_h, cls_scr, lp_vmem, lp_smem,
                 w, is_last_step, n_sub):
    rows = lax.broadcasted_iota(I32, (CLS_ROWS, TP), 0)
    onehot, counts = [], []
    lstart = jnp.zeros((CLS_ROWS, 1), F32)
    if n_sub < WIN_TILES:
        lp_ref[0, :, n_sub * TP:] = jnp.zeros((SUBLANES, (WIN_TILES - n_sub) * TP), I32)
        lp_vmem[:, n_sub * TP:] = jnp.zeros((SUBLANES, (WIN_TILES - n_sub) * TP), I32)
    for s in range(n_sub):
        cls = cls_scr[s, 0:1, :]
        onehot.append(jnp.where(rows == cls, 1.0, 0.0).astype(F32))
        counts.append(jnp.sum(onehot[s], axis=1, keepdims=True))
        lstart = lstart + jnp.sum(jnp.where(cls < rows, 1.0, 0.0).astype(F32), axis=1, keepdims=True)
    base = lstart
    for s in range(n_sub):
        pre = jnp.dot(onehot[s].astype(BF16), tri_ref[...], preferred_element_type=F32)
        lpos = jnp.sum(onehot[s] * (pre - 1.0 + base), axis=0, keepdims=True).astype(I32)
        lpos = lpos * D_ROWS
        lp_ref[0, :, s * TP:(s + 1) * TP] = jnp.broadcast_to(lpos, (SUBLANES, TP))
        lp_vmem[:, s * TP:(s + 1) * TP] = jnp.broadcast_to(lpos, (SUBLANES, TP))
        base = base + counts[s]
    tot = base - lstart
    carry = carry_ref[:, 0:1]
    lane = lax.broadcasted_iota(I32, (CLS_ROWS, LANES), 1)
    tab_ref[0] = jnp.where(lane == 0, carry, jnp.where(lane == 1, tot, jnp.where(lane == 2, lstart, 0.0)))
    carry_ref[...] = jnp.broadcast_to(carry + tot, (CLS_ROWS, LANES))
    pltpu.sync_copy(lp_vmem.at[pl.ds(0, 1)], lp_smem)

    def window_copy(win, n_tiles):
        rows = n_tiles * TP * D_ROWS
        start = win * (WIN * D_ROWS)
        start = start if isinstance(start, int) else pl.multiple_of(start, D_ROWS)
        return pltpu.make_async_copy(h1_buf.at[pl.ds(0, rows), :], h1_hbm.at[pl.ds(start, rows), :], h1_sem)

    @pl.when(w > 0)
    def _():
        window_copy(0, WIN_TILES).wait()

    for s in range(n_sub):
        def group(g, carry_, s=s):
            for k in range(SUBLANES):
                dst = pl.multiple_of(lp_smem[0, s * TP + g * SUBLANES + k], D_ROWS)
                h1_buf[pl.ds(dst, D_ROWS), :] = stage_h[s, g, pl.ds(k, D_ROWS, stride=SUBLANES), :]
            return carry_

        lax.fori_loop(0, TP // SUBLANES, group, 0, unroll=4)

    window_copy(w, n_sub).start()

    @pl.when(is_last_step)
    def _():
        window_copy(w, n_sub).wait()


def _mixer_front(x, g_mix_ref, w_in_ref):
    a = _rms(x, g_mix_ref[...]).astype(BF16)
    z = jnp.dot(a, w_in_ref[...], preferred_element_type=F32)
    glu = z[:, 0:C_CONV] * jax.nn.sigmoid(z[:, C_CONV:2 * C_CONV])
    u = jax.nn.gelu(z[:, 2 * C_CONV:2 * C_CONV + C_GMLP])
    gv = jax.nn.gelu(z[:, 2 * C_CONV + C_GMLP:])
    return glu, u, gv


def _mixer_body(xp_ref, xs_ref, hist_ref, g_mix_ref, w_in_ref, cw_ref, cb_ref, clg_ref, clb_ref, glg_ref, glb_ref,
                ws_ref, bs_ref, wsl_ref, bsl_ref, w_out_ref, g_ffn_ref, w_rt_ref, b_r_ref, tri_ref,
                eg_ref, eu_ref, ed_ref,
                h1_hbm, lp_ref, tab_ref, cstp_ref, csts_ref, vs_ref, egb_ref, eub_ref, edb_ref,
                glu_scr, xs_scr, cat_scr, carry_ref, stage_h, cls_scr, lp_vmem, lp_smem, h1_buf, h1_sem,
                *, n_prompt_tiles, nj, dec_seq, sblk, n_tail_tiles):
    i = pl.program_id(0)
    sub = lax.rem(i, WIN_TILES)

    egb_ref[...] = eg_ref[...].astype(BF16)
    eub_ref[...] = eu_ref[...].astype(BF16)
    edb_ref[...] = ed_ref[...].astype(BF16)

    @pl.when(i == 0)
    def _():
        carry_ref[...] = jnp.zeros_like(carry_ref)

    def tail(x):
        h = x + jnp.dot(cat_scr[...], w_out_ref[...], preferred_element_type=F32)
        _route_tile(h, g_ffn_ref, w_rt_ref, b_r_ref, stage_h, cls_scr, sub)

    @pl.when(i < n_prompt_tiles)
    def _prompt():
        j = lax.rem(i, nj)
        x = xp_ref[0]
        glu, u, gv = _mixer_front(x, g_mix_ref, w_in_ref)

        def time_rows(t0, n):
            return pl.ds(CONV_PITCH * t0, n, stride=CONV_PITCH)

        @pl.when(j == 0)
        def _():
            for s in range(C_CONV // LANES):
                glu_scr[s, time_rows(0, HIST), :] = jnp.zeros((HIST, LANES), F32)

        @pl.when(j > 0)
        def _():
            for s in range(C_CONV // LANES):
                glu_scr[s, time_rows(0, HIST), :] = glu_scr[s, time_rows(TP, HIST), :]

        for s in range(C_CONV // LANES):
            glu_scr[s, time_rows(HIST, TP), :] = glu[:, s * LANES:(s + 1) * LANES]
        cstp_ref[0] = jnp.concatenate([glu_scr[s, time_rows(TP, HIST), :] for s in range(C_CONV // LANES)], axis=1)

        cb = cb_ref[...]
        for r0 in range(0, TP, CONV_ROWS):
            slabs = []
            for s in range(C_CONV // LANES):
                acc = None
                for k in range(CONV_WIDTH):
                    term = (cw_ref[k:k + 1, s * LANES:(s + 1) * LANES]
                            * glu_scr[s, time_rows(r0 + HIST_OFF + k, CONV_ROWS), :])
                    acc = term if acc is None else acc + term
                slabs.append(acc)
            ya = jax.nn.silu(_ln(jnp.concatenate(slabs, axis=1) + cb, clg_ref[...], clb_ref[...]))
            cat_scr[r0:r0 + CONV_ROWS, 0:C_CONV] = ya.astype(BF16)

        v = _ln(gv, glg_ref[...], glb_ref[...])
        vb = v.astype(BF16)
        lane = lax.broadcasted_iota(I32, (CHUNK, LANES), 1)
        is_lo = lane < HEAD_DIM
        zero = jnp.zeros((CHUNK, LANES), BF16)
        for c in range(TP // CHUNK):
            for p in range(N_HEADS // 2):
                blk = vb[c * CHUNK:(c + 1) * CHUNK, p * LANES:(p + 1) * LANES]
                rhs = jnp.concatenate([jnp.where(is_lo, blk, zero), jnp.where(is_lo, zero, blk)], axis=0)
                mixed = jnp.dot(ws_ref[p], rhs, preferred_element_type=F32) + bs_ref[:, p * LANES:(p + 1) * LANES]
                yb = u[c * CHUNK:(c + 1) * CHUNK, p * LANES:(p + 1) * LANES] * mixed
                cat_scr[c * CHUNK:(c + 1) * CHUNK, C_CONV + p * LANES:C_CONV + (p + 1) * LANES] = yb.astype(BF16)
        tail(x)

    @pl.when(i >= n_prompt_tiles)
    def _sample():
        x_nat = xs_ref[...].reshape(TP, D_MODEL)
        slabs = []
        for s in range(D_ROWS):
            half, row0 = s // 2, (s % 2) * TP
            glu_scr[half, row0:row0 + TP, :] = x_nat[:, s * LANES:(s + 1) * LANES]
            slabs.append(jnp.concatenate(
                [glu_scr[half, pl.ds(row0 + t, sblk, stride=dec_seq), :] for t in range(dec_seq)], axis=0))
        x = jnp.concatenate(slabs, axis=1)
        glu, u, gv = _mixer_front(x, g_mix_ref, w_in_ref)

        nh = CONV_WIDTH - 1
        xs_scr[0:nh] = hist_ref[...]
        xs_scr[nh:nh + dec_seq] = glu.reshape(dec_seq, sblk, C_CONV)
        csts_ref[...] = xs_scr[dec_seq:dec_seq + nh]

        cb = cb_ref[...]
        for t in range(dec_seq):
            acc = jnp.zeros((sblk, C_CONV), F32)
            for k in range(CONV_WIDTH):
                acc = acc + cw_ref[k:k + 1, :] * xs_scr[t + k]
            ya = jax.nn.silu(_ln(acc + cb, clg_ref[...], clb_ref[...]))
            cat_scr[t * sblk:(t + 1) * sblk, 0:C_CONV] = ya.astype(BF16)

        v = _ln(gv, glg_ref[...], glb_ref[...])
        vs_ref[...] = v.reshape(dec_seq, sblk, C_GMLP)
        for t in range(dec_seq):
            mixed = jnp.broadcast_to(bsl_ref[t:t + 1, :], (sblk, C_GMLP))
            for tp in range(t + 1):
                r = t * dec_seq + tp
                mixed = mixed + wsl_ref[r:r + 1, :] * v[tp * sblk:(tp + 1) * sblk, :]
            yb = u[t * sblk:(t + 1) * sblk, :] * mixed
            cat_scr[t * sblk:(t + 1) * sblk, C_CONV:] = yb.astype(BF16)
        tail(x)

    is_last_step = i == pl.num_programs(0) - 1
    sort_args = (tri_ref, carry_ref, h1_hbm, h1_buf, h1_sem, lp_ref, tab_ref, stage_h, cls_scr, lp_vmem, lp_smem,
                 i // WIN_TILES, is_last_step)

    @pl.when(sub == WIN_TILES - 1)
    def _():
        _sort_window(*sort_args, WIN_TILES)

    if n_tail_tiles:
        @pl.when(is_last_step)
        def _():
            _sort_window(*sort_args, n_tail_tiles)


def _for_pieces(ln, fn):
    def emit(sizes):
        for size in sizes:
            shift = size.bit_length()
            off = lax.shift_left(lax.shift_right_logical(ln, shift), shift)

            @pl.when((ln & size) != 0)
            def _(off=off, size=size):
                fn(off, size)

    large = tuple(s for s in PIECES if s >= LARGE_PIECE)

    @pl.when(ln >= LARGE_PIECE)
    def _():
        emit(large)

    emit(tuple(s for s in PIECES if s < LARGE_PIECE))


def _for_runs(k, tcls_ref, r0_ref, nval_ref, jlo_ref, jhi_ref, rank0_ref, cnt_ref, lsrc_ref, fn):
    c = tcls_ref[k]
    ra = r0_ref[k]
    rb = ra + nval_ref[k]

    def body(j, carry):
        idx = j * N_CLASSES + c
        s = rank0_ref[idx]
        lo = jnp.maximum(s, ra)
        hi = jnp.minimum(s + cnt_ref[idx], rb)
        ln = hi - lo

        @pl.when(ln > 0)
        def _():
            src = lsrc_ref[idx] + (lo - s)
            dst = lo - ra
            _for_pieces(ln, lambda off, size: fn(src + off, dst + off, size))

        return carry

    lax.fori_loop(jlo_ref[k], jhi_ref[k], body, 0)


def _blocks(ref, first, n):
    start = first * D_ROWS if isinstance(first, int) else pl.multiple_of(first * D_ROWS, D_ROWS)
    return ref.at[pl.ds(start, n * D_ROWS), :]


def _moe_body(grp_ref, lo_ref, hi_ref, nval_ref, tcls_ref, r0_ref, jlo_ref, jhi_ref, rank0_ref, cnt_ref, lsrc_ref,
              h1_hbm, g_ffn_ref, w_r_ref, b_r_ref, wg_ref, wu_ref, wd_ref,
              h2_hbm, hbuf, obuf, gsem, ssem):
    k = pl.program_id(0)
    n = pl.num_programs(0)
    slot = lax.rem(k, 2)
    tables = (tcls_ref, r0_ref, nval_ref, jlo_ref, jhi_ref, rank0_ref, cnt_ref, lsrc_ref)

    def gather_tile(tile, to_slot):
        def copy(src, dst, size):
            pltpu.make_async_copy(_blocks(h1_hbm, src, size), _blocks(hbuf.at[to_slot], dst, size),
                                  gsem.at[to_slot]).start()
        _for_runs(tile, *tables, copy)

    def wait_rows(n_rows, make_copy):
        @pl.when(n_rows == TM)
        def _():
            make_copy(TM).wait()

        @pl.when(n_rows != TM)
        def _():
            _for_pieces(n_rows, lambda off, size: make_copy(size).wait())

    @pl.when(k == 0)
    def _():
        hbuf[...] = jnp.zeros_like(hbuf)
        gather_tile(0, 0)

    @pl.when(jnp.logical_and(k >= 2, nval_ref[jnp.maximum(k - 2, 0)] > 0))
    def _():
        wait_rows(nval_ref[jnp.maximum(k - 2, 0)],
                  lambda size: pltpu.make_async_copy(_blocks(obuf.at[slot], 0, size), _blocks(h2_hbm, 0, size),
                                                     ssem.at[slot]))

    nxt = jnp.minimum(k + 1, n - 1)

    @pl.when(jnp.logical_and(k + 1 < n, nval_ref[nxt] > 0))
    def _():
        gather_tile(nxt, 1 - slot)

    @pl.when(nval_ref[k] > 0)
    def _():
        wait_rows(nval_ref[k],
                  lambda size: pltpu.make_async_copy(_blocks(h1_hbm, 0, size), _blocks(hbuf.at[slot], 0, size),
                                                     gsem.at[slot]))
        e_lo = lo_ref[k]
        e_hi = hi_ref[k]

        def experts(m):
            h = _load_token_blocks(hbuf.at[slot], m)
            c = _rms(h, g_ffn_ref[...]).astype(BF16)
            logits = jnp.dot(c, w_r_ref[...], preferred_element_type=F32) + b_r_ref[...]
            wl, wh = _combine_weights(logits, grp_ref[k], e_lo, e_hi)

            def hidden(e):
                gate = jnp.dot(c, wg_ref[e], preferred_element_type=F32)
                return (jax.nn.silu(gate) * jnp.dot(c, wu_ref[e], preferred_element_type=F32)).astype(BF16)

            yl = jnp.dot(hidden(e_lo), wd_ref[e_lo], preferred_element_type=F32)
            yh = jnp.dot(hidden(e_hi), wd_ref[e_hi], preferred_element_type=F32)
            _store_token_blocks(obuf.at[slot], h + (wl * yl + wh * yh), m)

        @pl.when(nval_ref[k] > TM // 2)
        def _():
            experts(TM)

        @pl.when(nval_ref[k] <= TM // 2)
        def _():
            experts(TM // 2)

        def copy_back(src, dst, size):
            pltpu.make_async_copy(_blocks(obuf.at[slot], dst, size), _blocks(h2_hbm, src, size),
                                  ssem.at[slot]).start()
        _for_runs(k, *tables, copy_back)


def _ple_body(lp_ref, lpn_ref, h2_hbm, p_ref, g_ple_ref, w_gate_ref, w_proj_ref, g_fin_ref, y_ref,
              stage_h, win_buf, win_sem, *rest, tile0, win_tiles, seq_rows):
    ng = TP // SUBLANES
    i = pl.program_id(0)
    n = pl.num_programs(0)
    sub = lax.rem(i, win_tiles)
    w_local = i // win_tiles
    wslot = lax.rem(w_local, 2)
    hslot = lax.rem(i, 2)

    def window_copy(w, to_slot):
        rows = win_tiles * TP * D_ROWS
        start = pl.multiple_of((tile0 // WIN_TILES + w) * (WIN * D_ROWS), D_ROWS)
        return pltpu.make_async_copy(h2_hbm.at[pl.ds(start, rows), :], win_buf.at[to_slot, pl.ds(0, rows), :],
                                     win_sem.at[to_slot])

    def unsort_block(lp, tile, to_slot, g, k):
        src = pl.multiple_of(lp[0, 0, lax.rem(tile, win_tiles) * TP + g * SUBLANES + k], D_ROWS)
        stage_h[to_slot, g, pl.ds(k, D_ROWS, stride=SUBLANES), :] = win_buf[
            lax.rem(tile // win_tiles, 2), pl.ds(src, D_ROWS), :]

    @pl.when(i == 0)
    def _():
        window_copy(0, 0).start()
        window_copy(0, 0).wait()

        def group(g, carry):
            for k in range(SUBLANES):
                unsort_block(lp_ref, 0, 0, g, k)
            return carry

        lax.fori_loop(0, ng, group, 0, unroll=4)

    @pl.when(jnp.logical_and(sub == 0, i + win_tiles < n))
    def _():
        window_copy(w_local + 1, 1 - wslot).start()

    @pl.when(jnp.logical_and(sub == win_tiles - 1, i + 1 < n))
    def _():
        window_copy(w_local + 1, 1 - wslot).wait()

    h = jnp.concatenate([stage_h[hslot, :, s * SUBLANES:(s + 1) * SUBLANES, :].reshape(TP, LANES)
                         for s in range(D_ROWS)], axis=1)
    c = _rms(h, g_ple_ref[...]).astype(BF16)
    nxt = jnp.minimum(i + 1, n - 1)
    for g in range(ng):
        for k in range(SUBLANES):
            unsort_block(lpn_ref, nxt, 1 - hslot, g, k)
    gate = jax.nn.sigmoid(jnp.dot(c, w_gate_ref[...], preferred_element_type=F32))
    proj = jnp.dot(p_ref[...].astype(BF16), w_proj_ref[...], preferred_element_type=F32)
    y = _rms(h + proj * gate, g_fin_ref[...])
    if seq_rows is None:
        y_ref[...] = y
    else:
        sblk, dec_seq = seq_rows
        out_scr = rest[0]
        for s in range(D_ROWS):
            for t in range(dec_seq):
                out_scr[s, pl.ds(t, sblk, stride=dec_seq), :] = y[t * sblk:(t + 1) * sblk, s * LANES:(s + 1) * LANES]
            y_ref[:, s * LANES:(s + 1) * LANES] = out_scr[s]


def _full(shape, single=False):
    nd = len(shape)
    if single:
        return pl.BlockSpec(shape, lambda *_: (0,) * nd, pipeline_mode=pl.Buffered(1))
    return pl.BlockSpec(shape, lambda *_: (0,) * nd)


def _mixer(x_prompt, x_sample, hist_tm, weights, tri, expert_w):
    batch, seq, _ = x_prompt.shape
    dec_batch, dec_seq, _ = x_sample.shape
    nj = seq // TP
    npt = batch * nj
    sblk = TP // dec_seq
    nst = dec_batch // sblk
    n_tiles = npt + nst
    n_tail = n_tiles % WIN_TILES
    assert npt % WIN_TILES == 0 and n_tail <= nst
    n_windows = pl.cdiv(n_tiles, WIN_TILES)
    nh = CONV_WIDTH - 1
    w_specs = [_full(w.shape, single=True) for w in weights]
    body = functools.partial(_mixer_body, n_prompt_tiles=npt, nj=nj, dec_seq=dec_seq, sblk=sblk, n_tail_tiles=n_tail)

    def p_idx(i):
        return jnp.minimum(i, npt - 1)

    def s_idx(i):
        return jnp.maximum(i - npt, 0)

    assert npt % N_EXPERTS == 0
    parts = npt // N_EXPERTS
    e_specs = [pl.BlockSpec((1, w.shape[1] // parts, w.shape[2]), lambda i: (p_idx(i) // parts, p_idx(i) % parts, 0))
               for w in expert_w]
    e_shapes = [jax.ShapeDtypeStruct(w.shape, BF16) for w in expert_w]

    return pl.pallas_call(
        body,
        grid=(n_tiles,),
        in_specs=[pl.BlockSpec((1, TP, D_MODEL), lambda i: (p_idx(i) // nj, p_idx(i) % nj, 0)),
                  pl.BlockSpec((sblk, dec_seq, D_MODEL), lambda i: (s_idx(i), 0, 0), pipeline_mode=pl.Buffered(1)),
                  pl.BlockSpec((nh, sblk, C_CONV), lambda i: (0, s_idx(i), 0), pipeline_mode=pl.Buffered(1))]
        + w_specs + [_full(tri.shape, single=True)] + e_specs,
        out_specs=[
            pl.BlockSpec(memory_space=pl.ANY),
            pl.BlockSpec((1, SUBLANES, WIN), lambda i: (i // WIN_TILES, 0, 0)),
            pl.BlockSpec((1, CLS_ROWS, LANES), lambda i: (i // WIN_TILES, 0, 0)),
            pl.BlockSpec((1, HIST, C_CONV), lambda i: (p_idx(i) // nj, 0, 0)),
            pl.BlockSpec((nh, sblk, C_CONV), lambda i: (0, s_idx(i), 0), pipeline_mode=pl.Buffered(1)),
            pl.BlockSpec((dec_seq, sblk, C_GMLP), lambda i: (0, s_idx(i), 0), pipeline_mode=pl.Buffered(1)),
        ] + e_specs,
        out_shape=[
            jax.ShapeDtypeStruct((n_tiles * TP * D_ROWS, LANES), F32),
            jax.ShapeDtypeStruct((n_windows, SUBLANES, WIN), I32),
            jax.ShapeDtypeStruct((n_windows, CLS_ROWS, LANES), F32),
            jax.ShapeDtypeStruct((batch, HIST, C_CONV), F32),
            jax.ShapeDtypeStruct((nh, dec_batch, C_CONV), F32),
            jax.ShapeDtypeStruct((dec_seq, dec_batch, C_GMLP), F32),
        ] + e_shapes,
        scratch_shapes=[
            pltpu.VMEM((C_CONV // LANES, CONV_PITCH * (HIST + TP), LANES), F32),
            pltpu.VMEM((nh + dec_seq, sblk, C_CONV), F32),
            pltpu.VMEM((TP, D_MODEL), BF16),
            pltpu.VMEM((CLS_ROWS, LANES), F32),
            pltpu.VMEM((WIN_TILES, TP // SUBLANES, GROUP_ROWS, LANES), F32),
            pltpu.VMEM((WIN_TILES, SUBLANES, TP), I32),
            pltpu.VMEM((SUBLANES, WIN), I32),
            pltpu.SMEM((1, WIN), I32),
            pltpu.VMEM((WIN * D_ROWS, LANES), F32),
            pltpu.SemaphoreType.DMA(()),
        ],
        compiler_params=pltpu.CompilerParams(
            dimension_semantics=("arbitrary",), vmem_limit_bytes=VMEM_LIMIT),
        name="mixer",
    )(x_prompt, x_sample, hist_tm, *weights, tri, *expert_w)


def _moe(tile_tabs, run_tabs, h1, g_ffn, w_r, b_r, w_g, w_u, w_d):
    n_tiles = tile_tabs[0].shape[0]
    n_prefetch = len(tile_tabs) + len(run_tabs)
    grid_spec = pltpu.PrefetchScalarGridSpec(
        num_scalar_prefetch=n_prefetch,
        grid=(n_tiles,),
        in_specs=[
            pl.BlockSpec(memory_space=pl.ANY),
            pl.BlockSpec((1, D_MODEL), lambda i, *_: (0, 0)),
            pl.BlockSpec((D_MODEL, LANES), lambda i, *_: (0, 0)),
            pl.BlockSpec((1, LANES), lambda i, *_: (0, 0)),
            pl.BlockSpec((EPG, D_MODEL, D_EXPERT), lambda i, grp, *_: (grp[i], 0, 0)),
            pl.BlockSpec((EPG, D_MODEL, D_EXPERT), lambda i, grp, *_: (grp[i], 0, 0)),
            pl.BlockSpec((EPG, D_EXPERT, D_MODEL), lambda i, grp, *_: (grp[i], 0, 0)),
        ],
        out_specs=pl.BlockSpec(memory_space=pl.ANY),
        scratch_shapes=[pltpu.VMEM((2, TM * D_ROWS, LANES), F32),
                        pltpu.VMEM((2, TM * D_ROWS, LANES), F32),
                        pltpu.SemaphoreType.DMA((2,)),
                        pltpu.SemaphoreType.DMA((2,))],
    )
    return pl.pallas_call(
        _moe_body,
        grid_spec=grid_spec,
        out_shape=jax.ShapeDtypeStruct(h1.shape, F32),
        compiler_params=pltpu.CompilerParams(
            dimension_semantics=("arbitrary",), vmem_limit_bytes=VMEM_LIMIT),
        name="moe",
    )(*tile_tabs, *run_tabs, h1, g_ffn, w_r, b_r, w_g, w_u, w_d)


def _ple(lp, h2, tile0, p, g_ple, w_gate, w_proj, g_fin, name, seq_rows=None):
    n_tiles = p.shape[0] // TP
    win_tiles = min(WIN_TILES, n_tiles)
    assert tile0 % WIN_TILES == 0 and n_tiles % win_tiles == 0
    extra = [] if seq_rows is None else [pltpu.VMEM((D_ROWS, TP, LANES), F32)]
    return pl.pallas_call(
        functools.partial(_ple_body, tile0=tile0, win_tiles=win_tiles, seq_rows=seq_rows),
        grid=(n_tiles,),
        in_specs=[
            pl.BlockSpec((1, 1, WIN), lambda i: ((tile0 + i) // WIN_TILES, 0, 0), memory_space=pltpu.SMEM),
            pl.BlockSpec((1, 1, WIN), lambda i: ((tile0 + jnp.minimum(i + 1, n_tiles - 1)) // WIN_TILES, 0, 0),
                         memory_space=pltpu.SMEM),
            pl.BlockSpec(memory_space=pl.ANY),
            pl.BlockSpec((TP, PLE_DIM), lambda i: (i, 0)),
            _full(g_ple.shape), _full(w_gate.shape), _full(w_proj.shape), _full(g_fin.shape),
        ],
        out_specs=pl.BlockSpec((TP, D_MODEL), lambda i: (i, 0)),
        out_shape=jax.ShapeDtypeStruct((n_tiles * TP, D_MODEL), F32),
        scratch_shapes=[pltpu.VMEM((2, TP // SUBLANES, GROUP_ROWS, LANES), F32),
                        pltpu.VMEM((2, win_tiles * TP * D_ROWS, LANES), F32),
                        pltpu.SemaphoreType.DMA((2,))] + extra,
        compiler_params=pltpu.CompilerParams(
            dimension_semantics=("arbitrary",), vmem_limit_bytes=VMEM_LIMIT),
        name=name,
    )(lp, lp, h2, p, g_ple, w_gate, w_proj, g_fin)


def kernel(x_prompt, x_sample, p_prompt, p_sample, state_conv, norm_mix_g, w_in, conv_w, conv_b, conv_ln_g, conv_ln_b, gmlp_ln_g, gmlp_ln_b, w_s, b_s, w_out, norm_ffn_g, w_router_group, b_router_group, w_router_expert, b_router_expert, w_exp_gate, w_exp_up, w_exp_down, norm_ple_g, w_ple_gate, w_ple_proj, norm_final_g):
    depth = w_in.shape[0]
    assert depth == 1, "single-layer pipeline"
    batch, seq, _ = x_prompt.shape
    dec_batch, dec_seq, _ = x_sample.shape
    assert seq % TP == 0 and TP % CHUNK == 0 and TP % dec_seq == 0 and dec_batch % (TP // dec_seq) == 0
    sblk = TP // dec_seq
    n_prompt = batch * seq
    n_sample = dec_batch * dec_seq
    n_tok = n_prompt + n_sample

    row = lambda a: a.reshape(1, -1)
    w_in_b = w_in[0].astype(BF16)
    w_out_b = w_out[0].astype(BF16)
    cw = jnp.concatenate([conv_w[0], jnp.zeros((1, C_CONV), F32)], axis=0)
    tril = jnp.asarray(np.tril(np.ones((CHUNK, CHUNK), bool)))
    ws_m = jnp.where(tril[None], w_s[0], 0.0)
    ws_cat = jnp.concatenate([ws_m[0::2], ws_m[1::2]], axis=2).astype(BF16)
    bs_lane = jnp.repeat(jnp.transpose(b_s[0]), HEAD_DIM, axis=1)
    w_rt = jnp.zeros((CLS_ROWS, D_MODEL), F32)
    w_rt = w_rt.at[0:N_GROUPS].set(jnp.transpose(w_router_group[0]))
    w_rt = w_rt.at[N_GROUPS:N_GROUPS + N_EXPERTS].set(jnp.transpose(w_router_expert[0])).astype(BF16)
    b_r = jnp.zeros((CLS_ROWS, 1), F32)
    b_r = b_r.at[0:N_GROUPS, 0].set(b_router_group[0]).at[N_GROUPS:N_GROUPS + N_EXPERTS, 0].set(b_router_expert[0])
    n_logit = N_GROUPS + N_EXPERTS
    w_rn = jnp.concatenate([w_router_group[0], w_router_expert[0], jnp.zeros((D_MODEL, LANES - n_logit), F32)],
                           axis=1).astype(BF16)
    b_rn = jnp.concatenate([b_router_group[0], b_router_expert[0], jnp.zeros((LANES - n_logit,), F32)]).reshape(1, LANES)
    tri = jnp.asarray(np.triu(np.ones((TP, TP), np.float32)), dtype=BF16)
    wsl = jnp.where(jnp.asarray(np.tril(np.ones((dec_seq, dec_seq), bool)))[None], w_s[0][:, :dec_seq, :dec_seq], 0.0)
    wsl = jnp.repeat(jnp.transpose(wsl, (1, 2, 0)).reshape(dec_seq * dec_seq, N_HEADS), HEAD_DIM, axis=1)
    bsl = jnp.repeat(jnp.transpose(b_s[0][:, :dec_seq]), HEAD_DIM, axis=1)

    weights = (row(norm_mix_g[0]), w_in_b, cw, row(conv_b[0]), row(conv_ln_g[0]), row(conv_ln_b[0]),
               row(gmlp_ln_g[0]), row(gmlp_ln_b[0]), ws_cat, bs_lane, wsl, bsl,
               w_out_b, row(norm_ffn_g[0]), w_rt, b_r)

    hist_tm = jnp.transpose(state_conv[0], (1, 0, 2))
    h1, lp, tab, cst_p, cst_s, v_s, w_g, w_u, w_d = _mixer(
        x_prompt, x_sample, hist_tm, weights, tri, (w_exp_gate[0], w_exp_up[0], w_exp_down[0]))

    rank0 = tab[:, :N_CLASSES, 0].astype(I32)
    cnt = tab[:, :N_CLASSES, 1].astype(I32)
    lsrc = tab[:, :N_CLASSES, 2].astype(I32) + (jnp.arange(tab.shape[0], dtype=I32) * WIN)[:, None]
    total = rank0[-1] + cnt[-1]
    padded = ((total + TM - 1) // TM) * TM
    ids = jnp.arange(N_CLASSES, dtype=I32)
    ends = jnp.sum(jnp.where(ids[None, :] <= ids[:, None], padded[None, :], 0), axis=1)
    offs = ends - padded
    all_rows = jnp.sum(padded)
    n_tiles = (n_tok + N_CLASSES * (TM - 1)) // TM + 2
    tile_start = jnp.arange(n_tiles, dtype=I32) * TM
    used = tile_start < all_rows
    last_cls = jnp.max(jnp.where(padded > 0, ids, 0))
    tile_cls = jnp.sum((tile_start[:, None] >= ends[None, :]).astype(I32), axis=1)
    tile_cls = jnp.where(used, tile_cls, last_cls)
    of_tile = tile_cls[:, None] == ids[None, :]
    tile_r0 = tile_start - jnp.sum(jnp.where(of_tile, offs[None, :], 0), axis=1)
    tile_total = jnp.sum(jnp.where(of_tile, total[None, :], 0), axis=1)
    tile_nval = jnp.where(used, jnp.clip(tile_total - tile_r0, 0, TM), 0).astype(I32)
    pair = tile_cls % 6
    assert _PAIRS == ((0, 1), (0, 2), (0, 3), (1, 2), (1, 3), (2, 3))
    tile_lo = jnp.where(pair < 3, 0, jnp.where(pair < 5, 1, 2)).astype(I32)
    tile_hi = jnp.where(pair == 0, 1, jnp.where(jnp.logical_or(pair == 1, pair == 3), 2, 3)).astype(I32)
    run_beg = jnp.sum(jnp.where(of_tile[None], rank0[:, None, :], 0), axis=2)
    run_end = run_beg + jnp.sum(jnp.where(of_tile[None], cnt[:, None, :], 0), axis=2)
    tile_jlo = jnp.sum((run_end <= tile_r0[None, :]).astype(I32), axis=0)
    tile_jhi = jnp.sum((run_beg < (tile_r0 + tile_nval)[None, :]).astype(I32), axis=0)
    tile_jhi = jnp.maximum(tile_jhi, tile_jlo)

    h2 = _moe((tile_cls // 6, tile_lo, tile_hi, tile_nval, tile_cls, tile_r0, tile_jlo, tile_jhi),
              (rank0.reshape(-1), cnt.reshape(-1), lsrc.reshape(-1)),
              h1, row(norm_ffn_g[0]), w_rn, b_rn, w_g, w_u, w_d)

    lp = lp[:, 0:1, :]
    ple_w = (row(norm_ple_g[0]), w_ple_gate[0].astype(BF16), w_ple_proj[0].astype(BF16), row(norm_final_g))
    y_p = _ple(lp, h2, 0, p_prompt[0].reshape(n_prompt, PLE_DIM), *ple_w, name="ple_prompt")
    p_s_tm = jnp.transpose(p_sample[0].reshape(dec_batch // sblk, sblk, dec_seq, PLE_DIM), (0, 2, 1, 3))
    y_s = _ple(lp, h2, n_prompt // TP, p_s_tm.reshape(n_sample, PLE_DIM), *ple_w, name="ple_sample",
               seq_rows=(sblk, dec_seq))

    y_prompt = y_p.reshape(batch, seq, D_MODEL)
    y_sample = y_s.reshape(dec_batch, dec_seq, D_MODEL)
    state_conv_prompt = cst_p[:, HIST_OFF:, :][None]
    state_conv_sample = jnp.transpose(cst_s, (1, 0, 2))[None]
    state_gmlp_v_sample = jnp.transpose(v_s, (1, 0, 2))[None]
    return (y_prompt, y_sample, state_conv_prompt, state_conv_sample, state_gmlp_v_sample)
```

```python
import functools

import numpy as np
import jax
import jax.numpy as jnp
from jax import lax
from jax.experimental import pallas as pl
from jax.experimental.pallas import tpu as pltpu

F32 = jnp.float32
BF16 = jnp.bfloat16
I32 = jnp.int32

D_MODEL = 1024
C_CONV = 512
C_GMLP = 512
N_HEADS = 8
HEAD_DIM = 64
CONV_WIDTH = 31
CHUNK = 128
PLE_DIM = 256
N_GROUPS = 4
EPG = 4
N_EXPERTS = 16
D_EXPERT = 512
EPS = 1e-6
LANES = 128
SUBLANES = 8

N_CLASSES = N_GROUPS * 6
CLS_ROWS = 32
HIST = 32
HIST_OFF = HIST - (CONV_WIDTH - 1)
D_ROWS = D_MODEL // LANES
assert D_ROWS == SUBLANES
GROUP_ROWS = D_ROWS * SUBLANES

TP = 512
WIN_TILES = 4
WIN = WIN_TILES * TP
TM = 256
CONV_ROWS = 64
CONV_PITCH = 2
PIECES = tuple(TM >> b for b in range(TM.bit_length()))
LARGE_PIECE = 32

VMEM_LIMIT = 58 * 1024 * 1024

_PAIRS = ((0, 1), (0, 2), (0, 3), (1, 2), (1, 3), (2, 3))


def _rms(x, g):
    ms = jnp.mean(x * x, axis=-1, keepdims=True)
    return x * lax.rsqrt(ms + EPS) * g


def _ln(x, g, b):
    mu = jnp.mean(x, axis=-1, keepdims=True)
    xc = x - mu
    var = jnp.mean(xc * xc, axis=-1, keepdims=True)
    return xc * lax.rsqrt(var + EPS) * g + b


def _max4(v):
    return jnp.maximum(jnp.maximum(v[0], v[1]), jnp.maximum(v[2], v[3]))


def _first4(v, m):
    return jnp.where(v[0] == m, 0, jnp.where(v[1] == m, 1, jnp.where(v[2] == m, 2, 3))).astype(I32)


def _route(lt):
    gl = [lt[i:i + 1, :] for i in range(N_GROUPS)]
    g = _first4(gl, _max4(gl))
    a = []
    for j in range(EPG):
        rows = [lt[N_GROUPS + EPG * q + j:N_GROUPS + EPG * q + j + 1, :] for q in range(N_GROUPS)]
        a.append(jnp.where(g == 0, rows[0], jnp.where(g == 1, rows[1], jnp.where(g == 2, rows[2], rows[3]))))
    v1 = _max4(a)
    i1 = _first4(a, v1)
    a2 = [jnp.where(i1 == j, -jnp.inf, a[j]) for j in range(EPG)]
    i2 = _first4(a2, _max4(a2))
    lo = jnp.minimum(i1, i2)
    hi = jnp.maximum(i1, i2)
    pair = jnp.where(lo == 0, hi - 1, jnp.where(lo == 1, hi + 1, 5))
    return g * 6 + pair


def _combine_weights(logits, grp, e_lo, e_hi):
    lane = lax.broadcasted_iota(I32, logits.shape, 1)
    is_group = lane < N_GROUPS
    m = jnp.max(jnp.where(is_group, logits, -jnp.inf), axis=1, keepdims=True)
    den = jnp.sum(jnp.where(is_group, jnp.exp(logits - m), 0.0), axis=1, keepdims=True)
    base = N_GROUPS + EPG * grp
    v_lo = jnp.sum(jnp.where(lane == base + e_lo, logits, 0.0), axis=1, keepdims=True)
    v_hi = jnp.sum(jnp.where(lane == base + e_hi, logits, 0.0), axis=1, keepdims=True)
    top = jnp.maximum(v_lo, v_hi)
    p_lo = jnp.exp(v_lo - top)
    p_hi = jnp.exp(v_hi - top)
    scale = 1.0 / (den * (p_lo + p_hi))
    return p_lo * scale, p_hi * scale


def _load_token_blocks(ref, n_tok):
    return jnp.concatenate([ref[pl.ds(s, n_tok, stride=D_ROWS), :] for s in range(D_ROWS)], axis=1)


def _store_token_blocks(ref, x, n_tok):
    for s in range(D_ROWS):
        ref[pl.ds(s, n_tok, stride=D_ROWS), :] = x[:, s * LANES:(s + 1) * LANES]


def _route_tile(h, g_ffn_ref, w_rt_ref, b_r_ref, stage_h, cls_scr, sub):
    c = _rms(h, g_ffn_ref[...]).astype(BF16)
    lt = lax.dot_general(w_rt_ref[...], c, (((1,), (1,)), ((), ())), preferred_element_type=F32)
    cls_scr[sub] = jnp.broadcast_to(_route(lt + b_r_ref[...]), (SUBLANES, TP))
    ng = TP // SUBLANES
    for s in range(D_ROWS):
        stage_h[sub, :, s * SUBLANES:(s + 1) * SUBLANES, :] = h[:, s * LANES:(s + 1) * LANES].reshape(
            ng, SUBLANES, LANES)


def _sort_window(tri_ref, carry_ref, h1_hbm, h1_buf, h1_sem, lp_ref, tab_ref, stage_h, cls_scr, lp_vmem, lp_smem,
                 w, is_last_step, n_sub):
    rows = lax.broadcasted_iota(I32, (CLS_ROWS, TP), 0)
    onehot, counts = [], []
    lstart = jnp.zeros((CLS_ROWS, 1), F32)
    if n_sub < WIN_TILES:
        lp_ref[0, :, n_sub * TP:] = jnp.zeros((SUBLANES, (WIN_TILES - n_sub) * TP), I32)
        lp_vmem[:, n_sub * TP:] = jnp.zeros((SUBLANES, (WIN_TILES - n_sub) * TP), I32)
    for s in range(n_sub):
        cls = cls_scr[s, 0:1, :]
        onehot.append(jnp.where(rows == cls, 1.0, 0.0).astype(F32))
        counts.append(jnp.sum(onehot[s], axis=1, keepdims=True))
        lstart = lstart + jnp.sum(jnp.where(cls < rows, 1.0, 0.0).astype(F32), axis=1, keepdims=True)
    base = lstart
    for s in range(n_sub):
        pre = jnp.dot(onehot[s].astype(BF16), tri_ref[...], preferred_element_type=F32)
        lpos = jnp.sum(onehot[s] * (pre - 1.0 + base), axis=0, keepdims=True).astype(I32)
        lpos = lpos * D_ROWS
        lp_ref[0, :, s * TP:(s + 1) * TP] = jnp.broadcast_to(lpos, (SUBLANES, TP))
        lp_vmem[:, s * TP:(s + 1) * TP] = jnp.broadcast_to(lpos, (SUBLANES, TP))
        base = base + counts[s]
    tot = base - lstart
    carry = carry_ref[:, 0:1]
    lane = lax.broadcasted_iota(I32, (CLS_ROWS, LANES), 1)
    tab_ref[0] = jnp.where(lane == 0, carry, jnp.where(lane == 1, tot, jnp.where(lane == 2, lstart, 0.0)))
    carry_ref[...] = jnp.broadcast_to(carry + tot, (CLS_ROWS, LANES))
    pltpu.sync_copy(lp_vmem.at[pl.ds(0, 1)], lp_smem)

    def window_copy(win, n_tiles):
        rows = n_tiles * TP * D_ROWS
        start = win * (WIN * D_ROWS)
        start = start if isinstance(start, int) else pl.multiple_of(start, D_ROWS)
        return pltpu.make_async_copy(h1_buf.at[pl.ds(0, rows), :], h1_hbm.at[pl.ds(start, rows), :], h1_sem)

    @pl.when(w > 0)
    def _():
        window_copy(0, WIN_TILES).wait()

    for s in range(n_sub):
        def group(g, carry_, s=s):
            for k in range(SUBLANES):
                dst = pl.multiple_of(lp_smem[0, s * TP + g * SUBLANES + k], D_ROWS)
                h1_buf[pl.ds(dst, D_ROWS), :] = stage_h[s, g, pl.ds(k, D_ROWS, stride=SUBLANES), :]
            return carry_

        lax.fori_loop(0, TP // SUBLANES, group, 0, unroll=4)

    window_copy(w, n_sub).start()

    @pl.when(is_last_step)
    def _():
        window_copy(w, n_sub).wait()


def _mixer_front(x, g_mix_ref, w_in_ref):
    a = _rms(x, g_mix_ref[...]).astype(BF16)
    z = jnp.dot(a, w_in_ref[...], preferred_element_type=F32)
    glu = z[:, 0:C_CONV] * jax.nn.sigmoid(z[:, C_CONV:2 * C_CONV])
    u = jax.nn.gelu(z[:, 2 * C_CONV:2 * C_CONV + C_GMLP])
    gv = jax.nn.gelu(z[:, 2 * C_CONV + C_GMLP:])
    return glu, u, gv


def _mixer_body(xp_ref, xs_ref, hist_ref, g_mix_ref, w_in_ref, cw_ref, cb_ref, clg_ref, clb_ref, glg_ref, glb_ref,
                ws_ref, bs_ref, wsl_ref, bsl_ref, w_out_ref, g_ffn_ref, w_rt_ref, b_r_ref, tri_ref,
                eg_ref, eu_ref, ed_ref,
                h1_hbm, lp_ref, tab_ref, cstp_ref, csts_ref, vs_ref, egb_ref, eub_ref, edb_ref,
                glu_scr, xs_scr, cat_scr, carry_ref, stage_h, cls_scr, lp_vmem, lp_smem, h1_buf, h1_sem,
                *, n_prompt_tiles, nj, dec_seq, sblk, n_tail_tiles):
    i = pl.program_id(0)
    sub = lax.rem(i, WIN_TILES)

    egb_ref[...] = eg_ref[...].astype(BF16)
    eub_ref[...] = eu_ref[...].astype(BF16)
    edb_ref[...] = ed_ref[...].astype(BF16)

    @pl.when(i == 0)
    def _():
        carry_ref[...] = jnp.zeros_like(carry_ref)

    def tail(x):
        h = x + jnp.dot(cat_scr[...], w_out_ref[...], preferred_element_type=F32)
        _route_tile(h, g_ffn_ref, w_rt_ref, b_r_ref, stage_h, cls_scr, sub)

    @pl.when(i < n_prompt_tiles)
    def _prompt():
        j = lax.rem(i, nj)
        x = xp_ref[0]
        glu, u, gv = _mixer_front(x, g_mix_ref, w_in_ref)

        def time_rows(t0, n):
            return pl.ds(CONV_PITCH * t0, n, stride=CONV_PITCH)

        @pl.when(j == 0)
        def _():
            for s in range(C_CONV // LANES):
                glu_scr[s, time_rows(0, HIST), :] = jnp.zeros((HIST, LANES), F32)

        @pl.when(j > 0)
        def _():
            for s in range(C_CONV // LANES):
                glu_scr[s, time_rows(0, HIST), :] = glu_scr[s, time_rows(TP, HIST), :]

        for s in range(C_CONV // LANES):
            glu_scr[s, time_rows(HIST, TP), :] = glu[:, s * LANES:(s + 1) * LANES]
        cstp_ref[0] = jnp.concatenate([glu_scr[s, time_rows(TP, HIST), :] for s in range(C_CONV // LANES)], axis=1)

        cb = cb_ref[...]
        for r0 in range(0, TP, CONV_ROWS):
            slabs = []
            for s in range(C_CONV // LANES):
                acc = None
                for k in range(CONV_WIDTH):
                    term = (cw_ref[k:k + 1, s * LANES:(s + 1) * LANES]
                            * glu_scr[s, time_rows(r0 + HIST_OFF + k, CONV_ROWS), :])
                    acc = term if acc is None else acc + term
                slabs.append(acc)
            ya = jax.nn.silu(_ln(jnp.concatenate(slabs, axis=1) + cb, clg_ref[...], clb_ref[...]))
            cat_scr[r0:r0 + CONV_ROWS, 0:C_CONV] = ya.astype(BF16)

        v = _ln(gv, glg_ref[...], glb_ref[...])
        vb = v.astype(BF16)
        lane = lax.broadcasted_iota(I32, (CHUNK, LANES), 1)
        is_lo = lane < HEAD_DIM
        zero = jnp.zeros((CHUNK, LANES), BF16)
        for c in range(TP // CHUNK):
            for p in range(N_HEADS // 2):
                blk = vb[c * CHUNK:(c + 1) * CHUNK, p * LANES:(p + 1) * LANES]
                rhs = jnp.concatenate([jnp.where(is_lo, blk, zero), jnp.where(is_lo, zero, blk)], axis=0)
                mixed = jnp.dot(ws_ref[p], rhs, preferred_element_type=F32) + bs_ref[:, p * LANES:(p + 1) * LANES]
                yb = u[c * CHUNK:(c + 1) * CHUNK, p * LANES:(p + 1) * LANES] * mixed
                cat_scr[c * CHUNK:(c + 1) * CHUNK, C_CONV + p * LANES:C_CONV + (p + 1) * LANES] = yb.astype(BF16)
        tail(x)

    @pl.when(i >= n_prompt_tiles)
    def _sample():
        x_nat = xs_ref[...].reshape(TP, D_MODEL)
        slabs = []
        for s in range(D_ROWS):
            half, row0 = s // 2, (s % 2) * TP
            glu_scr[half, row0:row0 + TP, :] = x_nat[:, s * LANES:(s + 1) * LANES]
            slabs.append(jnp.concatenate(
                [glu_scr[half, pl.ds(row0 + t, sblk, stride=dec_seq), :] for t in range(dec_seq)], axis=0))
        x = jnp.concatenate(slabs, axis=1)
        glu, u, gv = _mixer_front(x, g_mix_ref, w_in_ref)

        nh = CONV_WIDTH - 1
        xs_scr[0:nh] = hist_ref[...]
        xs_scr[nh:nh + dec_seq] = glu.reshape(dec_seq, sblk, C_CONV)
        csts_ref[...] = xs_scr[dec_seq:dec_seq + nh]

        cb = cb_ref[...]
        for t in range(dec_seq):
            acc = jnp.zeros((sblk, C_CONV), F32)
            for k in range(CONV_WIDTH):
                acc = acc + cw_ref[k:k + 1, :] * xs_scr[t + k]
            ya = jax.nn.silu(_ln(acc + cb, clg_ref[...], clb_ref[...]))
            cat_scr[t * sblk:(t + 1) * sblk, 0:C_CONV] = ya.astype(BF16)

        v = _ln(gv, glg_ref[...], glb_ref[...])
        vs_ref[...] = v.reshape(dec_seq, sblk, C_GMLP)
        for t in range(dec_seq):
            mixed = jnp.broadcast_to(bsl_ref[t:t + 1, :], (sblk, C_GMLP))
            for tp in range(t + 1):
                r = t * dec_seq + tp
                mixed = mixed + wsl_ref[r:r + 1, :] * v[tp * sblk:(tp + 1) * sblk, :]
            yb = u[t * sblk:(t + 1) * sblk, :] * mixed
            cat_scr[t * sblk:(t + 1) * sblk, C_CONV:] = yb.astype(BF16)
        tail(x)

    is_last_step = i == pl.num_programs(0) - 1
    sort_args = (tri_ref, carry_ref, h1_hbm, h1_buf, h1_sem, lp_ref, tab_ref, stage_h, cls_scr, lp_vmem, lp_smem,
                 i // WIN_TILES, is_last_step)

    @pl.when(sub == WIN_TILES - 1)
    def _():
        _sort_window(*sort_args, WIN_TILES)

    if n_tail_tiles:
        @pl.when(is_last_step)
        def _():
            _sort_window(*sort_args, n_tail_tiles)


def _for_pieces(ln, fn):
    def emit(sizes):
        for size in sizes:
            shift = size.bit_length()
            off = lax.shift_left(lax.shift_right_logical(ln, shift), shift)

            @pl.when((ln & size) != 0)
            def _(off=off, size=size):
                fn(off, size)

    large = tuple(s for s in PIECES if s >= LARGE_PIECE)

    @pl.when(ln >= LARGE_PIECE)
    def _():
        emit(large)

    emit(tuple(s for s in PIECES if s < LARGE_PIECE))


def _for_runs(k, tcls_ref, r0_ref, nval_ref, jlo_ref, jhi_ref, rank0_ref, cnt_ref, lsrc_ref, fn):
    c = tcls_ref[k]
    ra = r0_ref[k]
    rb = ra + nval_ref[k]

    def body(j, carry):
        idx = j * N_CLASSES + c
        s = rank0_ref[idx]
        lo = jnp.maximum(s, ra)
        hi = jnp.minimum(s + cnt_ref[idx], rb)
        ln = hi - lo

        @pl.when(ln > 0)
        def _():
            src = lsrc_ref[idx] + (lo - s)
            dst = lo - ra
            _for_pieces(ln, lambda off, size: fn(src + off, dst + off, size))

        return carry

    lax.fori_loop(jlo_ref[k], jhi_ref[k], body, 0)


def _blocks(ref, first, n):
    start = first * D_ROWS if isinstance(first, int) else pl.multiple_of(first * D_ROWS, D_ROWS)
    return ref.at[pl.ds(start, n * D_ROWS), :]


def _moe_body(grp_ref, lo_ref, hi_ref, nval_ref, tcls_ref, r0_ref, jlo_ref, jhi_ref, rank0_ref, cnt_ref, lsrc_ref,
              h1_hbm, g_ffn_ref, w_r_ref, b_r_ref, wg_ref, wu_ref, wd_ref,
              h2_hbm, hbuf, obuf, gsem, ssem):
    k = pl.program_id(0)
    n = pl.num_programs(0)
    slot = lax.rem(k, 2)
    tables = (tcls_ref, r0_ref, nval_ref, jlo_ref, jhi_ref, rank0_ref, cnt_ref, lsrc_ref)

    def gather_tile(tile, to_slot):
        def copy(src, dst, size):
            pltpu.make_async_copy(_blocks(h1_hbm, src, size), _blocks(hbuf.at[to_slot], dst, size),
                                  gsem.at[to_slot]).start()
        _for_runs(tile, *tables, copy)

    def wait_rows(n_rows, make_copy):
        _for_pieces(n_rows, lambda off, size: make_copy(size).wait())

    @pl.when(k == 0)
    def _():
        hbuf[...] = jnp.zeros_like(hbuf)
        gather_tile(0, 0)

    @pl.when(jnp.logical_and(k >= 2, nval_ref[jnp.maximum(k - 2, 0)] > 0))
    def _():
        wait_rows(nval_ref[jnp.maximum(k - 2, 0)],
                  lambda size: pltpu.make_async_copy(_blocks(obuf.at[slot], 0, size), _blocks(h2_hbm, 0, size),
                                                     ssem.at[slot]))

    nxt = jnp.minimum(k + 1, n - 1)

    @pl.when(jnp.logical_and(k + 1 < n, nval_ref[nxt] > 0))
    def _():
        gather_tile(nxt, 1 - slot)

    @pl.when(nval_ref[k] > 0)
    def _():
        wait_rows(nval_ref[k],
                  lambda size: pltpu.make_async_copy(_blocks(h1_hbm, 0, size), _blocks(hbuf.at[slot], 0, size),
                                                     gsem.at[slot]))
        e_lo = lo_ref[k]
        e_hi = hi_ref[k]

        def experts(m):
            h = _load_token_blocks(hbuf.at[slot], m)
            c = _rms(h, g_ffn_ref[...]).astype(BF16)
            logits = jnp.dot(c, w_r_ref[...], preferred_element_type=F32) + b_r_ref[...]
            wl, wh = _combine_weights(logits, grp_ref[k], e_lo, e_hi)

            def hidden(e):
                halves = []
                for n0 in range(0, D_EXPERT, D_EXPERT // 2):
                    cols = slice(n0, n0 + D_EXPERT // 2)
                    gate = jnp.dot(c, wg_ref[e, :, cols], preferred_element_type=F32)
                    up = jnp.dot(c, wu_ref[e, :, cols], preferred_element_type=F32)
                    halves.append((jax.nn.silu(gate) * up).astype(BF16))
                return jnp.concatenate(halves, axis=1)

            yl = jnp.dot(hidden(e_lo), wd_ref[e_lo], preferred_element_type=F32)
            yh = jnp.dot(hidden(e_hi), wd_ref[e_hi], preferred_element_type=F32)
            _store_token_blocks(obuf.at[slot], h + (wl * yl + wh * yh), m)

        @pl.when(nval_ref[k] > TM // 2)
        def _():
            experts(TM)

        @pl.when(nval_ref[k] <= TM // 2)
        def _():
            experts(TM // 2)

        def copy_back(src, dst, size):
            pltpu.make_async_copy(_blocks(obuf.at[slot], dst, size), _blocks(h2_hbm, src, size),
                                  ssem.at[slot]).start()
        _for_runs(k, *tables, copy_back)


def _ple_body(lp_ref, lpn_ref, h2_hbm, p_ref, g_ple_ref, w_gate_ref, w_proj_ref, g_fin_ref, y_ref,
              stage_h, win_buf, win_sem, *rest, tile0, win_tiles, seq_rows):
    ng = TP // SUBLANES
    i = pl.program_id(0)
    n = pl.num_programs(0)
    sub = lax.rem(i, win_tiles)
    w_local = i // win_tiles
    wslot = lax.rem(w_local, 2)
    hslot = lax.rem(i, 2)

    def window_copy(w, to_slot):
        rows = win_tiles * TP * D_ROWS
        start = pl.multiple_of((tile0 // WIN_TILES + w) * (WIN * D_ROWS), D_ROWS)
        return pltpu.make_async_copy(h2_hbm.at[pl.ds(start, rows), :], win_buf.at[to_slot, pl.ds(0, rows), :],
                                     win_sem.at[to_slot])

    def unsort_block(lp, tile, to_slot, g, k):
        src = pl.multiple_of(lp[0, 0, lax.rem(tile, win_tiles) * TP + g * SUBLANES + k], D_ROWS)
        stage_h[to_slot, g, pl.ds(k, D_ROWS, stride=SUBLANES), :] = win_buf[
            lax.rem(tile // win_tiles, 2), pl.ds(src, D_ROWS), :]

    @pl.when(i == 0)
    def _():
        window_copy(0, 0).start()
        window_copy(0, 0).wait()

        def group(g, carry):
            for k in range(SUBLANES):
                unsort_block(lp_ref, 0, 0, g, k)
            return carry

        lax.fori_loop(0, ng, group, 0, unroll=4)

    @pl.when(jnp.logical_and(sub == 0, i + win_tiles < n))
    def _():
        window_copy(w_local + 1, 1 - wslot).start()

    @pl.when(jnp.logical_and(sub == win_tiles - 1, i + 1 < n))
    def _():
        window_copy(w_local + 1, 1 - wslot).wait()

    h = jnp.concatenate([stage_h[hslot, :, s * SUBLANES:(s + 1) * SUBLANES, :].reshape(TP, LANES)
                         for s in range(D_ROWS)], axis=1)
    c = _rms(h, g_ple_ref[...]).astype(BF16)
    nxt = jnp.minimum(i + 1, n - 1)
    for g in range(ng):
        for k in range(SUBLANES):
            unsort_block(lpn_ref, nxt, 1 - hslot, g, k)
    gate = jax.nn.sigmoid(jnp.dot(c, w_gate_ref[...], preferred_element_type=F32))
    proj = jnp.dot(p_ref[...].astype(BF16), w_proj_ref[...], preferred_element_type=F32)
    y = _rms(h + proj * gate, g_fin_ref[...])
    if seq_rows is None:
        y_ref[...] = y
    else:
        sblk, dec_seq = seq_rows
        out_scr = rest[0]
        for s in range(D_ROWS):
            for t in range(dec_seq):
                out_scr[s, pl.ds(t, sblk, stride=dec_seq), :] = y[t * sblk:(t + 1) * sblk, s * LANES:(s + 1) * LANES]
            y_ref[:, s * LANES:(s + 1) * LANES] = out_scr[s]


def _full(shape, single=False):
    nd = len(shape)
    if single:
        return pl.BlockSpec(shape, lambda *_: (0,) * nd, pipeline_mode=pl.Buffered(1))
    return pl.BlockSpec(shape, lambda *_: (0,) * nd)


def _mixer(x_prompt, x_sample, hist_tm, weights, tri, expert_w):
    batch, seq, _ = x_prompt.shape
    dec_batch, dec_seq, _ = x_sample.shape
    nj = seq // TP
    npt = batch * nj
    sblk = TP // dec_seq
    nst = dec_batch // sblk
    n_tiles = npt + nst
    n_tail = n_tiles % WIN_TILES
    assert npt % WIN_TILES == 0 and n_tail <= nst
    n_windows = pl.cdiv(n_tiles, WIN_TILES)
    nh = CONV_WIDTH - 1
    w_specs = [_full(w.shape, single=True) for w in weights]
    body = functools.partial(_mixer_body, n_prompt_tiles=npt, nj=nj, dec_seq=dec_seq, sblk=sblk, n_tail_tiles=n_tail)

    def p_idx(i):
        return jnp.minimum(i, npt - 1)

    def s_idx(i):
        return jnp.maximum(i - npt, 0)

    assert npt % N_EXPERTS == 0
    parts = npt // N_EXPERTS
    e_specs = [pl.BlockSpec((1, w.shape[1] // parts, w.shape[2]), lambda i: (p_idx(i) // parts, p_idx(i) % parts, 0))
               for w in expert_w]
    e_shapes = [jax.ShapeDtypeStruct(w.shape, BF16) for w in expert_w]

    return pl.pallas_call(
        body,
        grid=(n_tiles,),
        in_specs=[pl.BlockSpec((1, TP, D_MODEL), lambda i: (p_idx(i) // nj, p_idx(i) % nj, 0)),
                  pl.BlockSpec((sblk, dec_seq, D_MODEL), lambda i: (s_idx(i), 0, 0), pipeline_mode=pl.Buffered(1)),
                  pl.BlockSpec((nh, sblk, C_CONV), lambda i: (0, s_idx(i), 0), pipeline_mode=pl.Buffered(1))]
        + w_specs + [_full(tri.shape, single=True)] + e_specs,
        out_specs=[
            pl.BlockSpec(memory_space=pl.ANY),
            pl.BlockSpec((1, SUBLANES, WIN), lambda i: (i // WIN_TILES, 0, 0)),
            pl.BlockSpec((1, CLS_ROWS, LANES), lambda i: (i // WIN_TILES, 0, 0)),
            pl.BlockSpec((1, HIST, C_CONV), lambda i: (p_idx(i) // nj, 0, 0)),
            pl.BlockSpec((nh, sblk, C_CONV), lambda i: (0, s_idx(i), 0), pipeline_mode=pl.Buffered(1)),
            pl.BlockSpec((dec_seq, sblk, C_GMLP), lambda i: (0, s_idx(i), 0), pipeline_mode=pl.Buffered(1)),
        ] + e_specs,
        out_shape=[
            jax.ShapeDtypeStruct((n_tiles * TP * D_ROWS, LANES), F32),
            jax.ShapeDtypeStruct((n_windows, SUBLANES, WIN), I32),
            jax.ShapeDtypeStruct((n_windows, CLS_ROWS, LANES), F32),
            jax.ShapeDtypeStruct((batch, HIST, C_CONV), F32),
            jax.ShapeDtypeStruct((nh, dec_batch, C_CONV), F32),
            jax.ShapeDtypeStruct((dec_seq, dec_batch, C_GMLP), F32),
        ] + e_shapes,
        scratch_shapes=[
            pltpu.VMEM((C_CONV // LANES, CONV_PITCH * (HIST + TP), LANES), F32),
            pltpu.VMEM((nh + dec_seq, sblk, C_CONV), F32),
            pltpu.VMEM((TP, D_MODEL), BF16),
            pltpu.VMEM((CLS_ROWS, LANES), F32),
            pltpu.VMEM((WIN_TILES, TP // SUBLANES, GROUP_ROWS, LANES), F32),
            pltpu.VMEM((WIN_TILES, SUBLANES, TP), I32),
            pltpu.VMEM((SUBLANES, WIN), I32),
            pltpu.SMEM((1, WIN), I32),
            pltpu.VMEM((WIN * D_ROWS, LANES), F32),
            pltpu.SemaphoreType.DMA(()),
        ],
        compiler_params=pltpu.CompilerParams(
            dimension_semantics=("arbitrary",), vmem_limit_bytes=VMEM_LIMIT),
        name="mixer",
    )(x_prompt, x_sample, hist_tm, *weights, tri, *expert_w)


def _moe(tile_tabs, run_tabs, h1, g_ffn, w_r, b_r, w_g, w_u, w_d):
    n_tiles = tile_tabs[0].shape[0]
    n_prefetch = len(tile_tabs) + len(run_tabs)
    grid_spec = pltpu.PrefetchScalarGridSpec(
        num_scalar_prefetch=n_prefetch,
        grid=(n_tiles,),
        in_specs=[
            pl.BlockSpec(memory_space=pl.ANY),
            pl.BlockSpec((1, D_MODEL), lambda i, *_: (0, 0)),
            pl.BlockSpec((D_MODEL, LANES), lambda i, *_: (0, 0)),
            pl.BlockSpec((1, LANES), lambda i, *_: (0, 0)),
            pl.BlockSpec((EPG, D_MODEL, D_EXPERT), lambda i, grp, *_: (grp[i], 0, 0)),
            pl.BlockSpec((EPG, D_MODEL, D_EXPERT), lambda i, grp, *_: (grp[i], 0, 0)),
            pl.BlockSpec((EPG, D_EXPERT, D_MODEL), lambda i, grp, *_: (grp[i], 0, 0)),
        ],
        out_specs=pl.BlockSpec(memory_space=pl.ANY),
        scratch_shapes=[pltpu.VMEM((2, TM * D_ROWS, LANES), F32),
                        pltpu.VMEM((2, TM * D_ROWS, LANES), F32),
                        pltpu.SemaphoreType.DMA((2,)),
                        pltpu.SemaphoreType.DMA((2,))],
    )
    return pl.pallas_call(
        _moe_body,
        grid_spec=grid_spec,
        out_shape=jax.ShapeDtypeStruct(h1.shape, F32),
        compiler_params=pltpu.CompilerParams(
            dimension_semantics=("arbitrary",), vmem_limit_bytes=VMEM_LIMIT),
        name="moe",
    )(*tile_tabs, *run_tabs, h1, g_ffn, w_r, b_r, w_g, w_u, w_d)


def _ple(lp, h2, tile0, p, g_ple, w_gate, w_proj, g_fin, name, seq_rows=None):
    n_tiles = p.shape[0] // TP
    win_tiles = min(WIN_TILES, n_tiles)
    assert tile0 % WIN_TILES == 0 and n_tiles % win_tiles == 0
    extra = [] if seq_rows is None else [pltpu.VMEM((D_ROWS, TP, LANES), F32)]
    return pl.pallas_call(
        functools.partial(_ple_body, tile0=tile0, win_tiles=win_tiles, seq_rows=seq_rows),
        grid=(n_tiles,),
        in_specs=[
            pl.BlockSpec((1, 1, WIN), lambda i: ((tile0 + i) // WIN_TILES, 0, 0), memory_space=pltpu.SMEM),
            pl.BlockSpec((1, 1, WIN), lambda i: ((tile0 + jnp.minimum(i + 1, n_tiles - 1)) // WIN_TILES, 0, 0),
                         memory_space=pltpu.SMEM),
            pl.BlockSpec(memory_space=pl.ANY),
            pl.BlockSpec((TP, PLE_DIM), lambda i: (i, 0)),
            _full(g_ple.shape), _full(w_gate.shape), _full(w_proj.shape), _full(g_fin.shape),
        ],
        out_specs=pl.BlockSpec((TP, D_MODEL), lambda i: (i, 0)),
        out_shape=jax.ShapeDtypeStruct((n_tiles * TP, D_MODEL), F32),
        scratch_shapes=[pltpu.VMEM((2, TP // SUBLANES, GROUP_ROWS, LANES), F32),
                        pltpu.VMEM((2, win_tiles * TP * D_ROWS, LANES), F32),
                        pltpu.SemaphoreType.DMA((2,))] + extra,
        compiler_params=pltpu.CompilerParams(
            dimension_semantics=("arbitrary",), vmem_limit_bytes=VMEM_LIMIT),
        name=name,
    )(lp, lp, h2, p, g_ple, w_gate, w_proj, g_fin)


def kernel(x_prompt, x_sample, p_prompt, p_sample, state_conv, norm_mix_g, w_in, conv_w, conv_b, conv_ln_g, conv_ln_b, gmlp_ln_g, gmlp_ln_b, w_s, b_s, w_out, norm_ffn_g, w_router_group, b_router_group, w_router_expert, b_router_expert, w_exp_gate, w_exp_up, w_exp_down, norm_ple_g, w_ple_gate, w_ple_proj, norm_final_g):
    depth = w_in.shape[0]
    assert depth == 1, "single-layer pipeline"
    batch, seq, _ = x_prompt.shape
    dec_batch, dec_seq, _ = x_sample.shape
    assert seq % TP == 0 and TP % CHUNK == 0 and TP % dec_seq == 0 and dec_batch % (TP // dec_seq) == 0
    sblk = TP // dec_seq
    n_prompt = batch * seq
    n_sample = dec_batch * dec_seq
    n_tok = n_prompt + n_sample

    row = lambda a: a.reshape(1, -1)
    w_in_b = w_in[0].astype(BF16)
    w_out_b = w_out[0].astype(BF16)
    cw = jnp.concatenate([conv_w[0], jnp.zeros((1, C_CONV), F32)], axis=0)
    tril = jnp.asarray(np.tril(np.ones((CHUNK, CHUNK), bool)))
    ws_m = jnp.where(tril[None], w_s[0], 0.0)
    ws_cat = jnp.concatenate([ws_m[0::2], ws_m[1::2]], axis=2).astype(BF16)
    bs_lane = jnp.repeat(jnp.transpose(b_s[0]), HEAD_DIM, axis=1)
    w_rt = jnp.zeros((CLS_ROWS, D_MODEL), F32)
    w_rt = w_rt.at[0:N_GROUPS].set(jnp.transpose(w_router_group[0]))
    w_rt = w_rt.at[N_GROUPS:N_GROUPS + N_EXPERTS].set(jnp.transpose(w_router_expert[0])).astype(BF16)
    b_r = jnp.zeros((CLS_ROWS, 1), F32)
    b_r = b_r.at[0:N_GROUPS, 0].set(b_router_group[0]).at[N_GROUPS:N_GROUPS + N_EXPERTS, 0].set(b_router_expert[0])
    n_logit = N_GROUPS + N_EXPERTS
    w_rn = jnp.concatenate([w_router_group[0], w_router_expert[0], jnp.zeros((D_MODEL, LANES - n_logit), F32)],
                           axis=1).astype(BF16)
    b_rn = jnp.concatenate([b_router_group[0], b_router_expert[0], jnp.zeros((LANES - n_logit,), F32)]).reshape(1, LANES)
    tri = jnp.asarray(np.triu(np.ones((TP, TP), np.float32)), dtype=BF16)
    wsl = jnp.where(jnp.asarray(np.tril(np.ones((dec_seq, dec_seq), bool)))[None], w_s[0][:, :dec_seq, :dec_seq], 0.0)
    wsl = jnp.repeat(jnp.transpose(wsl, (1, 2, 0)).reshape(dec_seq * dec_seq, N_HEADS), HEAD_DIM, axis=1)
    bsl = jnp.repeat(jnp.transpose(b_s[0][:, :dec_seq]), HEAD_DIM, axis=1)

    weights = (row(norm_mix_g[0]), w_in_b, cw, row(conv_b[0]), row(conv_ln_g[0]), row(conv_ln_b[0]),
               row(gmlp_ln_g[0]), row(gmlp_ln_b[0]), ws_cat, bs_lane, wsl, bsl,
               w_out_b, row(norm_ffn_g[0]), w_rt, b_r)

    hist_tm = jnp.transpose(state_conv[0], (1, 0, 2))
    h1, lp, tab, cst_p, cst_s, v_s, w_g, w_u, w_d = _mixer(
        x_prompt, x_sample, hist_tm, weights, tri, (w_exp_gate[0], w_exp_up[0], w_exp_down[0]))

    rank0 = tab[:, :N_CLASSES, 0].astype(I32)
    cnt = tab[:, :N_CLASSES, 1].astype(I32)
    lsrc = tab[:, :N_CLASSES, 2].astype(I32) + (jnp.arange(tab.shape[0], dtype=I32) * WIN)[:, None]
    total = rank0[-1] + cnt[-1]
    padded = ((total + TM - 1) // TM) * TM
    ids = jnp.arange(N_CLASSES, dtype=I32)
    ends = jnp.sum(jnp.where(ids[None, :] <= ids[:, None], padded[None, :], 0), axis=1)
    offs = ends - padded
    all_rows = jnp.sum(padded)
    n_tiles = (n_tok + N_CLASSES * (TM - 1)) // TM + 2
    tile_start = jnp.arange(n_tiles, dtype=I32) * TM
    used = tile_start < all_rows
    last_cls = jnp.max(jnp.where(padded > 0, ids, 0))
    tile_cls = jnp.sum((tile_start[:, None] >= ends[None, :]).astype(I32), axis=1)
    tile_cls = jnp.where(used, tile_cls, last_cls)
    of_tile = tile_cls[:, None] == ids[None, :]
    tile_r0 = tile_start - jnp.sum(jnp.where(of_tile, offs[None, :], 0), axis=1)
    tile_total = jnp.sum(jnp.where(of_tile, total[None, :], 0), axis=1)
    tile_nval = jnp.where(used, jnp.clip(tile_total - tile_r0, 0, TM), 0).astype(I32)
    pair = tile_cls % 6
    assert _PAIRS == ((0, 1), (0, 2), (0, 3), (1, 2), (1, 3), (2, 3))
    tile_lo = jnp.where(pair < 3, 0, jnp.where(pair < 5, 1, 2)).astype(I32)
    tile_hi = jnp.where(pair == 0, 1, jnp.where(jnp.logical_or(pair == 1, pair == 3), 2, 3)).astype(I32)
    run_beg = jnp.sum(jnp.where(of_tile[None], rank0[:, None, :], 0), axis=2)
    run_end = run_beg + jnp.sum(jnp.where(of_tile[None], cnt[:, None, :], 0), axis=2)
    tile_jlo = jnp.sum((run_end <= tile_r0[None, :]).astype(I32), axis=0)
    tile_jhi = jnp.sum((run_beg < (tile_r0 + tile_nval)[None, :]).astype(I32), axis=0)
    tile_jhi = jnp.maximum(tile_jhi, tile_jlo)

    h2 = _moe((tile_cls // 6, tile_lo, tile_hi, tile_nval, tile_cls, tile_r0, tile_jlo, tile_jhi),
              (rank0.reshape(-1), cnt.reshape(-1), lsrc.reshape(-1)),
              h1, row(norm_ffn_g[0]), w_rn, b_rn, w_g, w_u, w_d)

    lp = lp[:, 0:1, :]
    ple_w = (row(norm_ple_g[0]), w_ple_gate[0].astype(BF16), w_ple_proj[0].astype(BF16), row(norm_final_g))
    y_p = _ple(lp, h2, 0, p_prompt[0].reshape(n_prompt, PLE_DIM), *ple_w, name="ple_prompt")
    p_s_tm = jnp.transpose(p_sample[0].reshape(dec_batch // sblk, sblk, dec_seq, PLE_DIM), (0, 2, 1, 3))
    y_s = _ple(lp, h2, n_prompt // TP, p_s_tm.reshape(n_sample, PLE_DIM), *ple_w, name="ple_sample",
               seq_rows=(sblk, dec_seq))

    y_prompt = y_p.reshape(batch, seq, D_MODEL)
    y_sample = y_s.reshape(dec_batch, dec_seq, D_MODEL)
    state_conv_prompt = cst_p[:, HIST_OFF:, :][None]
    state_conv_sample = jnp.transpose(cst_s, (1, 0, 2))[None]
    state_gmlp_v_sample = jnp.transpose(v_s, (1, 0, 2))[None]
    return (y_prompt, y_sample, state_conv_prompt, state_conv_sample, state_gmlp_v_sample)
```
